```python
import jax, jax.numpy as jnp
from jax import lax
import numpy as np

D_MODEL = 2048
BATCH = 2
SEQ = 4096
DEPTH = 2

HEAD_DIM = 128
GROUP_HEADS = 4
GROUP_WIDTH = GROUP_HEADS * HEAD_DIM
N_GROUPS = 5
MIX_WIDTH = N_GROUPS * GROUP_WIDTH
IN_SLICES = 4 + 3 + 4 + 4 + 2
IN_COLS = IN_SLICES * GROUP_WIDTH
ROPE_THETA = 10000.0
NORM_EPS = 1e-6
NEG_INF = -1e30

DILATED_PATTERNS = ((128, 1), (512, 4), (2048, 16))
SGU_CHUNK = 128
MOBA_BLOCK = 256
MOBA_TOPK = 3
MOBA_QCHUNK = 64
CONV_WIDTH = 3
N_MEM = 256
MEM_HEADS = 4

kernel_name = 'hymba_style_hybrid_dilated_sgu_moba_conv'


def rms_norm(x, g):
    xf = x.astype(jnp.float32)
    y = xf * lax.rsqrt(jnp.mean(xf * xf, axis=-1, keepdims=True) + NORM_EPS)
    return (y * g.astype(jnp.float32)).astype(x.dtype)


def layer_norm(x, g, b):
    xf = x.astype(jnp.float32)
    mu = jnp.mean(xf, axis=-1, keepdims=True)
    var = jnp.mean(jnp.square(xf - mu), axis=-1, keepdims=True)
    y = (xf - mu) * lax.rsqrt(var + NORM_EPS) * g.astype(jnp.float32) + b.astype(jnp.float32)
    return y.astype(x.dtype)


def rotary(x, pos):
    half = HEAD_DIM // 2
    inv_freq = ROPE_THETA ** (-jnp.arange(half, dtype=jnp.float32) / half)
    ang = pos.astype(jnp.float32)[:, None] * inv_freq[None, :]
    cos = jnp.cos(ang)[None, :, None, :]
    sin = jnp.sin(ang)[None, :, None, :]
    xf = x.astype(jnp.float32)
    x1, x2 = xf[..., :half], xf[..., half:]
    return jnp.concatenate([x1 * cos - x2 * sin, x2 * cos + x1 * sin], axis=-1).astype(x.dtype)


def pad_seq(x, mult, axis):
    p = (-x.shape[axis]) % mult
    if p == 0:
        return x
    widths = [(0, 0)] * x.ndim
    widths[axis] = (0, p)
    return jnp.pad(x, widths)


def dilated_window_attention(q, k, v, window, dilation):
    b, s, h, hd = q.shape
    band = window // dilation
    qp, kp, vp = (pad_seq(t, dilation * band, 1) for t in (q, k, v))
    sp = qp.shape[1]
    length = sp // dilation
    nb = length // band

    def to_strided(t):
        t = t.reshape(b, length, dilation, h, hd).transpose(0, 2, 3, 1, 4)
        return t.reshape(b, dilation, h, nb, band, hd)

    def with_prev(t):
        prev = jnp.pad(t, [(0, 0)] * 3 + [(1, 0), (0, 0), (0, 0)])[:, :, :, :-1]
        return jnp.concatenate([prev, t], axis=-2)

    qs = to_strided(qp)
    kb = with_prev(to_strided(kp))
    vb = with_prev(to_strided(vp))
    scores = jnp.einsum('brhnqd,brhnkd->brhnqk', qs, kb).astype(jnp.float32) * (hd ** -0.5)
    qi = jnp.arange(band)[:, None]
    kj = jnp.arange(2 * band)[None, :]
    dist = qi + band - kj
    blk = jnp.arange(nb)[:, None, None]
    valid = (dist >= 0) & (dist <= band) & (blk * band + kj - band >= 0)
    scores = jnp.where(valid, scores, NEG_INF)
    lse = jax.nn.logsumexp(scores, axis=-1)
    p = jnp.exp(scores - lse[..., None])
    o = jnp.einsum('brhnqk,brhnkd->brhnqd', p, vb.astype(jnp.float32))
    o = o.reshape(b, dilation, h, length, hd).transpose(0, 3, 1, 2, 4).reshape(b, sp, h, hd)[:, :s]
    lse = lse.reshape(b, dilation, h, length).transpose(0, 3, 1, 2).reshape(b, sp, h)[:, :s]
    return o, lse


def dilated_mixture(q, k, v):
    results = [dilated_window_attention(q, k, v, w, d) for w, d in DILATED_PATTERNS]
    outs = jnp.stack([r[0] for r in results], axis=0)
    lses = jnp.stack([r[1] for r in results], axis=0)
    wts = jax.nn.softmax(lses, axis=0)
    return jnp.einsum('pbsh,pbshd->bshd', wts, outs)


def chunked_spatial_gating(u, v, w_s, b_s, ln_g, ln_b):
    b, s, _ = v.shape
    vn = pad_seq(layer_norm(v, ln_g, ln_b), SGU_CHUNK, 1)
    sp = vn.shape[1]
    vc = vn.reshape(b, sp // SGU_CHUNK, SGU_CHUNK, GROUP_HEADS, HEAD_DIM)
    causal = jnp.tril(jnp.ones((SGU_CHUNK, SGU_CHUNK), dtype=w_s.dtype))
    mixed = jnp.einsum('gij,bnjgc->bnigc', w_s * causal, vc) + b_s.T[None, None, :, :, None]
    return u * mixed.reshape(b, sp, GROUP_WIDTH)[:, :s]


def moba_attention(q, k, v):
    b, s, h, hd = q.shape
    qp, kp, vp = (pad_seq(t, MOBA_BLOCK, 1).transpose(0, 2, 1, 3) for t in (q, k, v))
    sp = qp.shape[2]
    nb = sp // MOBA_BLOCK
    topk = min(MOBA_TOPK, nb)
    kblocks = kp.reshape(b, h, nb, MOBA_BLOCK, hd)
    vblocks = vp.reshape(b, h, nb, MOBA_BLOCK, hd)
    kmean = jnp.mean(kblocks.astype(jnp.float32), axis=3)
    gate = jnp.einsum('bhsd,bhnd->bhsn', qp.astype(jnp.float32), kmean)
    qblk = jnp.arange(sp) // MOBA_BLOCK
    past = jnp.arange(nb)[None, :] < qblk[:, None]
    gate = jnp.where(past, gate, NEG_INF)
    _, idx = lax.top_k(gate, topk)
    sel_valid = jnp.arange(topk)[None, :] < qblk[:, None]

    nc = sp // MOBA_QCHUNK
    q_ch = qp.reshape(b, h, nc, MOBA_QCHUNK, hd).transpose(2, 0, 1, 3, 4)
    idx_ch = idx.reshape(b, h, nc, MOBA_QCHUNK, topk).transpose(2, 0, 1, 3, 4)
    valid_ch = sel_valid.reshape(nc, MOBA_QCHUNK, topk)
    starts = jnp.arange(nc, dtype=jnp.int32) * MOBA_QCHUNK
    gather = jax.vmap(jax.vmap(lambda t, i: t[i]))
    scale = hd ** -0.5
    n_sel = topk * MOBA_BLOCK

    def one_chunk(args):
        qc, ic, vc, start = args
        kg = gather(kblocks, ic).reshape(b, h, MOBA_QCHUNK, n_sel, hd)
        vg = gather(vblocks, ic).reshape(b, h, MOBA_QCHUNK, n_sel, hd)
        own = (start // MOBA_BLOCK) * MOBA_BLOCK
        ko = lax.dynamic_slice_in_dim(kp, own, MOBA_BLOCK, axis=2)
        vo = lax.dynamic_slice_in_dim(vp, own, MOBA_BLOCK, axis=2)
        s_sel = jnp.einsum('bhqd,bhqkd->bhqk', qc, kg).astype(jnp.float32) * scale
        s_sel = jnp.where(jnp.repeat(vc, MOBA_BLOCK, axis=-1), s_sel, NEG_INF)
        s_own = jnp.einsum('bhqd,bhkd->bhqk', qc, ko).astype(jnp.float32) * scale
        qpos = start + jnp.arange(MOBA_QCHUNK)
        kpos = own + jnp.arange(MOBA_BLOCK)
        s_own = jnp.where(kpos[None, :] <= qpos[:, None], s_own, NEG_INF)
        p = jax.nn.softmax(jnp.concatenate([s_sel, s_own], axis=-1), axis=-1)
        return (jnp.einsum('bhqk,bhqkd->bhqd', p[..., :n_sel], vg.astype(jnp.float32))
                + jnp.einsum('bhqk,bhkd->bhqd', p[..., n_sel:], vo.astype(jnp.float32)))

    o = lax.map(one_chunk, (q_ch, idx_ch, valid_ch, starts))
    return o.transpose(1, 0, 3, 2, 4).reshape(b, sp, h, hd)[:, :s]


def short_conv_mixer(gate_b, gate_c, hval, conv_w):
    z = gate_c * hval
    y = lax.conv_general_dilated(z, conv_w[:, None, :].astype(z.dtype), window_strides=(1,),
                                 padding=[(CONV_WIDTH - 1, 0)],
                                 dimension_numbers=('NWC', 'WIO', 'NWC'),
                                 feature_group_count=GROUP_WIDTH)
    return gate_b * y


def memory_cross_attention(q, mem_n, w_mem_kv):
    b, m, _ = mem_n.shape
    kv = (mem_n @ w_mem_kv).reshape(b, m, 2, MEM_HEADS, HEAD_DIM)
    s = jnp.einsum('bshd,bmhd->bhsm', q, kv[:, :, 0]).astype(jnp.float32) * (HEAD_DIM ** -0.5)
    p = jax.nn.softmax(s, axis=-1)
    return jnp.einsum('bhsm,bmhd->bshd', p, kv[:, :, 1].astype(jnp.float32))


def hybrid_layer(x, mem, pos, norm_g, w_in, sgu_w, sgu_b, sgu_ln_g, sgu_ln_b, conv_w,
                 mem_norm_g, w_mem_kv, out_norm_g, w_out):
    b, s, _ = x.shape
    h = rms_norm(x, norm_g)
    proj = h @ w_in
    (a_q, a_k, a_v, a_g, b_u, b_v, b_g, c_q, c_k, c_v, c_g,
     d_b, d_c, d_h, d_g, x_q, x_g) = jnp.split(proj, IN_SLICES, axis=-1)
    heads = lambda t: t.reshape(b, s, GROUP_HEADS, HEAD_DIM)
    flat = lambda t: t.reshape(b, s, GROUP_WIDTH)

    o_a = flat(dilated_mixture(rotary(heads(a_q), pos), rotary(heads(a_k), pos), heads(a_v)))
    o_b = chunked_spatial_gating(jax.nn.gelu(b_u), jax.nn.gelu(b_v), sgu_w, sgu_b, sgu_ln_g, sgu_ln_b)
    o_c = flat(moba_attention(rotary(heads(c_q), pos), rotary(heads(c_k), pos), heads(c_v)))
    o_d = short_conv_mixer(d_b, d_c, d_h, conv_w)
    o_x = flat(memory_cross_attention(heads(x_q), rms_norm(mem, mem_norm_g), w_mem_kv))

    branches = [(o_a, a_g), (o_b, b_g), (o_c, c_g), (o_d, d_g), (o_x, x_g)]
    y = jnp.concatenate([(o * jax.nn.silu(g)).astype(x.dtype) for o, g in branches], axis=-1)
    y = rms_norm(y.reshape(b, s, N_GROUPS, GROUP_WIDTH),
                 out_norm_g.reshape(N_GROUPS, GROUP_WIDTH)).reshape(b, s, MIX_WIDTH)
    return x + y @ w_out


def setup_inputs(seed: int = 0) -> dict:
    key = jax.random.key(seed)
    ks = jax.random.split(key, 16)
    f32 = jnp.float32
    nrm = lambda k, shape, scale: jax.random.normal(k, shape, f32) * scale
    return {
        'x': nrm(ks[0], (BATCH, SEQ, D_MODEL), 1.0),
        'mem': nrm(ks[1], (BATCH, N_MEM, D_MODEL), 1.0),
        'norm_g': 1.0 + nrm(ks[2], (DEPTH, D_MODEL), 0.01),
        'w_in': nrm(ks[3], (DEPTH, D_MODEL, IN_COLS), D_MODEL ** -0.5),
        'sgu_w': nrm(ks[4], (DEPTH, GROUP_HEADS, SGU_CHUNK, SGU_CHUNK), SGU_CHUNK ** -0.5),
        'sgu_b': 1.0 + nrm(ks[5], (DEPTH, GROUP_HEADS, SGU_CHUNK), 0.01),
        'sgu_ln_g': 1.0 + nrm(ks[6], (DEPTH, GROUP_WIDTH), 0.01),
        'sgu_ln_b': nrm(ks[7], (DEPTH, GROUP_WIDTH), 0.01),
        'conv_w': nrm(ks[8], (DEPTH, CONV_WIDTH, GROUP_WIDTH), CONV_WIDTH ** -0.5),
        'mem_norm_g': 1.0 + nrm(ks[9], (DEPTH, D_MODEL), 0.01),
        'w_mem_kv': nrm(ks[10], (DEPTH, D_MODEL, 2 * GROUP_WIDTH), D_MODEL ** -0.5),
        'out_norm_g': 1.0 + nrm(ks[11], (DEPTH, MIX_WIDTH), 0.01),
        'w_out': nrm(ks[12], (DEPTH, MIX_WIDTH, D_MODEL), MIX_WIDTH ** -0.5),
        'final_norm_g': 1.0 + nrm(ks[13], (D_MODEL,), 0.01),
    }


def reference(x, mem, norm_g, w_in, sgu_w, sgu_b, sgu_ln_g, sgu_ln_b, conv_w,
              mem_norm_g, w_mem_kv, out_norm_g, w_out, final_norm_g):
    pos = jnp.arange(x.shape[1], dtype=jnp.int32)
    for i in range(DEPTH):
        x = hybrid_layer(x, mem, pos, norm_g[i], w_in[i], sgu_w[i], sgu_b[i], sgu_ln_g[i],
                         sgu_ln_b[i], conv_w[i], mem_norm_g[i], w_mem_kv[i], out_norm_g[i], w_out[i])
    return rms_norm(x, final_norm_g)
```

```python
import functools

import jax
import jax.numpy as jnp
from jax import lax
from jax.experimental import pallas as pl
from jax.experimental.pallas import tpu as pltpu

F32 = jnp.float32
BF16 = jnp.bfloat16

HEAD_DIM = 128
GROUP_HEADS = 4
GROUP_WIDTH = GROUP_HEADS * HEAD_DIM
N_GROUPS = 5
MIX_WIDTH = N_GROUPS * GROUP_WIDTH
IN_SLICES = 17
ROPE_THETA = 10000.0
NORM_EPS = 1e-6
NEG_INF = -1e30
ATTN_SCALE = HEAD_DIM ** -0.5

DIL_BAND = 128
DIL_STRIDES = (1, 4, 16)
DIL_SUPER = DIL_BAND * DIL_STRIDES[-1]
SGU_CHUNK = 128
MOBA_BLOCK = 256
MOBA_TOPK = 3
CONV_WIDTH = 3
CONV_HALO = 8

SL_A_Q, SL_A_K, SL_A_V, SL_A_G = 0, 1, 2, 3
SL_B_U, SL_B_V, SL_B_G = 4, 5, 6
SL_C_Q, SL_C_K, SL_C_V, SL_C_G = 7, 8, 9, 10
SL_D_B, SL_D_C, SL_D_H, SL_D_G = 11, 12, 13, 14
SL_X_Q, SL_X_G = 15, 16
ROTARY_SLICES = (SL_A_Q, SL_A_K, SL_C_Q, SL_C_K)

VMEM_LIMIT_BYTES = 56 * 1024 * 1024


def _params(*semantics):
    return pltpu.CompilerParams(dimension_semantics=semantics, vmem_limit_bytes=VMEM_LIMIT_BYTES)


def _dot_nt(a, b):
    return lax.dot_general(a, b, (((1,), (1,)), ((), ())), preferred_element_type=F32)


INPROJ_TM = 1024
NORM_ROWS = 128


def _inproj_kernel(x_ref, g_ref, w_ref, cos_ref, sin_ref, o_ref, h_scr):
    n = pl.program_id(1)

    @pl.when(n == 0)
    def _():
        def body(i, c):
            rows = pl.ds(pl.multiple_of(i * NORM_ROWS, NORM_ROWS), NORM_ROWS)
            x = x_ref[rows, :]
            y = x * lax.rsqrt(jnp.mean(x * x, axis=-1, keepdims=True) + NORM_EPS) * g_ref[...]
            h_scr[rows, :] = y.astype(BF16)
            return c
        lax.fori_loop(0, INPROJ_TM // NORM_ROWS, body, 0)

    o_ref[...] = jnp.dot(h_scr[...], w_ref[...], preferred_element_type=F32)

    is_rot = functools.reduce(jnp.logical_or, [n == s for s in ROTARY_SLICES])

    @pl.when(is_rot)
    def _():
        cos = cos_ref[...]
        sin = sin_ref[...]
        for hh in range(GROUP_HEADS):
            sl = slice(hh * HEAD_DIM, (hh + 1) * HEAD_DIM)
            a = o_ref[:, sl]
            o_ref[:, sl] = a * cos + pltpu.roll(a, HEAD_DIM // 2, 1) * sin


def _inproj(x2d, norm_g, w_bf16, cos_t, sin_t, seq):
    m, d = x2d.shape
    n_cols = w_bf16.shape[1]
    tm = INPROJ_TM
    seq_tiles = seq // tm
    return pl.pallas_call(
        _inproj_kernel,
        out_shape=jax.ShapeDtypeStruct((m, n_cols), F32),
        grid=(m // tm, n_cols // GROUP_WIDTH),
        in_specs=[
            pl.BlockSpec((tm, d), lambda i, n: (i, 0)),
            pl.BlockSpec((1, d), lambda i, n: (0, 0)),
            pl.BlockSpec((d, GROUP_WIDTH), lambda i, n: (0, n)),
            pl.BlockSpec((tm, HEAD_DIM), lambda i, n: (i % seq_tiles, 0)),
            pl.BlockSpec((tm, HEAD_DIM), lambda i, n: (i % seq_tiles, 0)),
        ],
        out_specs=pl.BlockSpec((tm, GROUP_WIDTH), lambda i, n: (i, n)),
        scratch_shapes=[pltpu.VMEM((tm, d), BF16)],
        compiler_params=_params("parallel", "arbitrary"),
        name="inproj",
    )(x2d, norm_g.reshape(1, d), w_bf16, cos_t, sin_t)


def _dilated_kernel(q_ref, kp_ref, ko_ref, vp_ref, vo_ref, o_ref,
                    kn, vn, o1, o2, o3, l1, l2, l3):
    sup = DIL_SUPER
    band = DIL_BAND
    has_prev = pl.program_id(1) > 0

    kn[0:sup, :] = kp_ref[...]
    kn[sup:2 * sup, :] = ko_ref[...]
    vn[0:sup, :] = vp_ref[...]
    vn[sup:2 * sup, :] = vo_ref[...]

    qi = lax.broadcasted_iota(jnp.int32, (band, 2 * band), 0)
    kj = lax.broadcasted_iota(jnp.int32, (band, 2 * band), 1)
    in_band = (kj >= qi) & (kj <= qi + band)
    bias_any = jnp.where(in_band, 0.0, NEG_INF).astype(F32)
    bias_start = jnp.where(in_band & (kj >= band), 0.0, NEG_INF).astype(F32)

    def attend(qb, kb, vb, at_start):
        bias = jnp.where(at_start, bias_start, bias_any)
        s = _dot_nt(qb.astype(BF16), kb.astype(BF16)) * ATTN_SCALE + bias
        m = jnp.max(s, axis=-1, keepdims=True)
        p = jnp.exp(s - m)
        l = jnp.sum(p, axis=-1, keepdims=True)
        o = jnp.dot(p.astype(BF16), vb.astype(BF16), preferred_element_type=F32)
        return o / l, jnp.broadcast_to(m + jnp.log(l), (band, HEAD_DIM))

    def pattern(stride, o_out, l_out):
        blocks_per_group = band * stride

        def body(t, c):
            grp = t // stride
            res = t % stride
            q0 = grp * blocks_per_group + res
            k0 = sup + q0 - blocks_per_group
            if stride == 1:
                q0 = pl.multiple_of(q0, band)
                k0 = pl.multiple_of(k0, band)
                qsl = pl.ds(q0, band)
                ksl = pl.ds(k0, 2 * band)
            else:
                qsl = pl.ds(q0, band, stride=stride)
                ksl = pl.ds(k0, 2 * band, stride=stride)
            at_start = jnp.logical_and(grp == 0, jnp.logical_not(has_prev))
            o, lse = attend(q_ref[qsl, :], kn[ksl, :], vn[ksl, :], at_start)
            o_out[qsl, :] = o
            l_out[qsl, :] = lse
            return c
        lax.fori_loop(0, sup // band, body, 0)

    pattern(DIL_STRIDES[0], o1, l1)
    pattern(DIL_STRIDES[1], o2, l2)
    pattern(DIL_STRIDES[2], o3, l3)

    rows_per_step = 256

    def merge(i, c):
        rows = pl.ds(pl.multiple_of(i * rows_per_step, rows_per_step), rows_per_step)
        a1, a2, a3 = l1[rows, :], l2[rows, :], l3[rows, :]
        mx = jnp.maximum(jnp.maximum(a1, a2), a3)
        w1, w2, w3 = jnp.exp(a1 - mx), jnp.exp(a2 - mx), jnp.exp(a3 - mx)
        num = w1 * o1[rows, :] + w2 * o2[rows, :] + w3 * o3[rows, :]
        o_ref[rows, :] = (num / (w1 + w2 + w3)).astype(o_ref.dtype)
        return c
    lax.fori_loop(0, sup // rows_per_step, merge, 0)


def _dilated(proj, batch, seq):
    sup = DIL_SUPER
    blk = (None, sup, HEAD_DIM)
    prev = lambda sb: jnp.maximum(sb - 1, 0)
    scr = pltpu.VMEM((sup, HEAD_DIM), F32)
    win = pltpu.VMEM((2 * sup, HEAD_DIM), F32)
    return pl.pallas_call(
        _dilated_kernel,
        out_shape=jax.ShapeDtypeStruct((batch, seq, GROUP_WIDTH), BF16),
        grid=(batch, seq // sup, GROUP_HEADS),
        in_specs=[
            pl.BlockSpec(blk, lambda b, sb, h: (b, sb, SL_A_Q * GROUP_HEADS + h)),
            pl.BlockSpec(blk, lambda b, sb, h: (b, prev(sb), SL_A_K * GROUP_HEADS + h)),
            pl.BlockSpec(blk, lambda b, sb, h: (b, sb, SL_A_K * GROUP_HEADS + h)),
            pl.BlockSpec(blk, lambda b, sb, h: (b, prev(sb), SL_A_V * GROUP_HEADS + h)),
            pl.BlockSpec(blk, lambda b, sb, h: (b, sb, SL_A_V * GROUP_HEADS + h)),
        ],
        out_specs=pl.BlockSpec(blk, lambda b, sb, h: (b, sb, h)),
        scratch_shapes=[win, win, scr, scr, scr, scr, scr, scr],
        compiler_params=_params("parallel", "parallel", "parallel"),
        name="dilated",
    )(proj, proj, proj, proj, proj)


def _moba_kernel(q_ref, k_ref, v_ref, o_ref, kmean, selneg, m_scr, l_scr, acc_scr):
    blk = MOBA_BLOCK
    n_blocks = k_ref.shape[0] // blk
    qi = pl.program_id(2)

    @pl.when(qi == 0)
    def _():
        for n in range(n_blocks):
            kmean[n:n + 1, :] = jnp.mean(k_ref[n * blk:(n + 1) * blk, :], axis=0, keepdims=True)

    q = q_ref[...]
    qb = q.astype(BF16)

    gate = lax.dot_general(q, kmean[...], (((1,), (1,)), ((), ())),
                           precision=lax.Precision.HIGHEST, preferred_element_type=F32)
    bidx = lax.broadcasted_iota(jnp.int32, (blk, n_blocks), 1)
    bidx_f = bidx.astype(F32)
    g = jnp.where(bidx < qi, gate, NEG_INF)
    sel_bias = jnp.full((blk, n_blocks), NEG_INF, F32)
    for t in range(MOBA_TOPK):
        mx = jnp.max(g, axis=-1, keepdims=True)
        first = jnp.min(jnp.where(g == mx, bidx_f, float(n_blocks)), axis=-1, keepdims=True)
        hit = bidx_f == first
        sel_bias = jnp.where(hit, jnp.where(t < qi, 0.0, NEG_INF), sel_bias)
        g = jnp.where(hit, -jnp.inf, g)
    selneg[...] = sel_bias

    own = pl.ds(pl.multiple_of(qi * blk, blk), blk)
    row = lax.broadcasted_iota(jnp.int32, (blk, blk), 0)
    col = lax.broadcasted_iota(jnp.int32, (blk, blk), 1)
    s = _dot_nt(qb, k_ref[own, :].astype(BF16)) * ATTN_SCALE
    s = jnp.where(col <= row, s, NEG_INF)
    m = jnp.max(s, axis=-1, keepdims=True)
    p = jnp.exp(s - m)
    m_scr[...] = m
    l_scr[...] = jnp.sum(p, axis=-1, keepdims=True)
    acc_scr[...] = jnp.dot(p.astype(BF16), v_ref[own, :].astype(BF16), preferred_element_type=F32)

    def body(j, c):
        rows = pl.ds(pl.multiple_of(j * blk, blk), blk)
        selcol = jnp.sum(jnp.where(bidx == j, selneg[...], 0.0), axis=-1, keepdims=True)
        s = _dot_nt(qb, k_ref[rows, :].astype(BF16)) * ATTN_SCALE + selcol
        m_old = m_scr[...]
        m_new = jnp.maximum(m_old, jnp.max(s, axis=-1, keepdims=True))
        alpha = jnp.exp(m_old - m_new)
        p = jnp.exp(s - m_new)
        m_scr[...] = m_new
        l_scr[...] = alpha * l_scr[...] + jnp.sum(p, axis=-1, keepdims=True)
        acc_scr[...] = alpha * acc_scr[...] + jnp.dot(
            p.astype(BF16), v_ref[rows, :].astype(BF16), preferred_element_type=F32)
        return c
    lax.fori_loop(0, qi, body, 0)

    o_ref[...] = (acc_scr[...] / l_scr[...]).astype(o_ref.dtype)


def _moba(proj, batch, seq):
    blk = MOBA_BLOCK
    n_blocks = seq // blk
    full = (None, seq, HEAD_DIM)
    return pl.pallas_call(
        _moba_kernel,
        out_shape=jax.ShapeDtypeStruct((batch, seq, GROUP_WIDTH), BF16),
        grid=(batch, GROUP_HEADS, n_blocks),
        in_specs=[
            pl.BlockSpec((None, blk, HEAD_DIM), lambda b, h, i: (b, i, SL_C_Q * GROUP_HEADS + h)),
            pl.BlockSpec(full, lambda b, h, i: (b, 0, SL_C_K * GROUP_HEADS + h)),
            pl.BlockSpec(full, lambda b, h, i: (b, 0, SL_C_V * GROUP_HEADS + h)),
        ],
        out_specs=pl.BlockSpec((None, blk, HEAD_DIM), lambda b, h, i: (b, i, h)),
        scratch_shapes=[
            pltpu.VMEM((n_blocks, HEAD_DIM), F32),
            pltpu.VMEM((blk, n_blocks), F32),
            pltpu.VMEM((blk, 1), F32),
            pltpu.VMEM((blk, 1), F32),
            pltpu.VMEM((blk, HEAD_DIM), F32),
        ],
        compiler_params=_params("parallel", "parallel", "arbitrary"),
        name="moba",
    )(proj, proj, proj)


def _memkv_kernel(mem_ref, g_ref, w_ref, o_ref):
    x = mem_ref[...]
    y = x * lax.rsqrt(jnp.mean(x * x, axis=-1, keepdims=True) + NORM_EPS) * g_ref[...]
    o_ref[...] = jnp.dot(y.astype(BF16), w_ref[...], preferred_element_type=F32).astype(o_ref.dtype)


def _memkv(mem, mem_norm_g, w_bf16):
    batch, n_mem, d = mem.shape
    n_cols = w_bf16.shape[1]
    return pl.pallas_call(
        _memkv_kernel,
        out_shape=jax.ShapeDtypeStruct((batch, n_mem, n_cols), BF16),
        grid=(batch,),
        in_specs=[
            pl.BlockSpec((None, n_mem, d), lambda b: (b, 0, 0)),
            pl.BlockSpec((1, d), lambda b: (0, 0)),
            pl.BlockSpec((d, n_cols), lambda b: (0, 0)),
        ],
        out_specs=pl.BlockSpec((None, n_mem, n_cols), lambda b: (b, 0, 0)),
        compiler_params=_params("parallel"),
        name="memkv",
    )(mem, mem_norm_g.reshape(1, d), w_bf16)


LOCAL_TM = 512


def _local_kernel(bu_ref, bv_ref, db_ref, dc_ref, dh_ref, hc_ref, hh_ref, xq_ref,
                  ws_ref, bs_ref, lng_ref, lnb_ref, cw_ref, kv_ref, o_ref, zbuf):
    tm = LOCAL_TM
    gw = GROUP_WIDTH

    u = jax.nn.gelu(bu_ref[...])
    v = jax.nn.gelu(bv_ref[...])
    mu = jnp.mean(v, axis=-1, keepdims=True)
    vc = v - mu
    var = jnp.mean(vc * vc, axis=-1, keepdims=True)
    vnorm = (vc * lax.rsqrt(var + NORM_EPS) * lng_ref[...] + lnb_ref[...]).astype(BF16)
    r = lax.broadcasted_iota(jnp.int32, (SGU_CHUNK, SGU_CHUNK), 0)
    c = lax.broadcasted_iota(jnp.int32, (SGU_CHUNK, SGU_CHUNK), 1)
    for hd in range(GROUP_HEADS):
        cols = slice(hd * HEAD_DIM, (hd + 1) * HEAD_DIM)
        w_causal = jnp.where(c <= r, ws_ref[hd], 0.0).astype(BF16)
        bias = bs_ref[:, hd:hd + 1]
        for ch in range(tm // SGU_CHUNK):
            rows = slice(ch * SGU_CHUNK, (ch + 1) * SGU_CHUNK)
            mixed = jnp.dot(w_causal, vnorm[rows, cols], preferred_element_type=F32) + bias
            o_ref[rows, cols] = (u[rows, cols] * mixed).astype(o_ref.dtype)

    first_tile = pl.program_id(1) == 0
    zbuf[0:CONV_HALO, :] = jnp.where(first_tile, 0.0, hc_ref[...] * hh_ref[...])
    zbuf[CONV_HALO:CONV_HALO + tm, :] = dc_ref[...] * dh_ref[...]
    y = cw_ref[CONV_WIDTH - 1:CONV_WIDTH, :] * zbuf[CONV_HALO:CONV_HALO + tm, :]
    for tap in range(CONV_WIDTH - 1):
        back = CONV_WIDTH - 1 - tap
        y = y + cw_ref[tap:tap + 1, :] * zbuf[CONV_HALO - back:CONV_HALO - back + tm, :]
    o_ref[:, gw:2 * gw] = (db_ref[...] * y).astype(o_ref.dtype)

    for hd in range(GROUP_HEADS):
        cols = slice(hd * HEAD_DIM, (hd + 1) * HEAD_DIM)
        kh = kv_ref[:, hd * HEAD_DIM:(hd + 1) * HEAD_DIM]
        vh = kv_ref[:, gw + hd * HEAD_DIM:gw + (hd + 1) * HEAD_DIM]
        s = _dot_nt(xq_ref[:, cols].astype(BF16), kh) * ATTN_SCALE
        m = jnp.max(s, axis=-1, keepdims=True)
        p = jnp.exp(s - m)
        l = jnp.sum(p, axis=-1, keepdims=True)
        o = jnp.dot(p.astype(BF16), vh, preferred_element_type=F32) / l
        o_ref[:, 2 * gw + hd * HEAD_DIM:2 * gw + (hd + 1) * HEAD_DIM] = o.astype(o_ref.dtype)


def _local(proj, memkv, sgu_w, sgu_b_t, ln_g, ln_b, conv_w, batch, seq):
    tm = LOCAL_TM
    gw = GROUP_WIDTH
    halo_per_tile = tm // CONV_HALO
    tile = lambda sl: pl.BlockSpec((None, tm, gw), lambda b, i: (b, i, sl))
    halo = lambda sl: pl.BlockSpec(
        (None, CONV_HALO, gw), lambda b, i: (b, jnp.maximum(i * halo_per_tile - 1, 0), sl))
    whole = lambda a: pl.BlockSpec(a.shape, lambda b, i: (0,) * a.ndim)
    n_mem = memkv.shape[1]
    return pl.pallas_call(
        _local_kernel,
        out_shape=jax.ShapeDtypeStruct((batch, seq, 3 * gw), BF16),
        grid=(batch, seq // tm),
        in_specs=[
            tile(SL_B_U), tile(SL_B_V), tile(SL_D_B), tile(SL_D_C), tile(SL_D_H),
            halo(SL_D_C), halo(SL_D_H), tile(SL_X_Q),
            whole(sgu_w), whole(sgu_b_t), whole(ln_g), whole(ln_b), whole(conv_w),
            pl.BlockSpec((None, n_mem, 2 * gw), lambda b, i: (b, 0, 0)),
        ],
        out_specs=pl.BlockSpec((None, tm, 3 * gw), lambda b, i: (b, i, 0)),
        scratch_shapes=[pltpu.VMEM((CONV_HALO + tm, gw), F32)],
        compiler_params=_params("parallel", "parallel"),
        name="local",
    )(proj, proj, proj, proj, proj, proj, proj, proj,
      sgu_w, sgu_b_t, ln_g, ln_b, conv_w, memkv)


OUTPROJ_TM = 256


def _outproj_kernel(ya_ref, yl_ref, yc_ref, ga_ref, gb_ref, gc_ref, gd_ref, gx_ref,
                    ng_ref, w_ref, x_ref, fg_ref, o_ref, y_scr, *, final):
    gw = GROUP_WIDTH
    branches = (
        (ya_ref[...], ga_ref),
        (yl_ref[:, 0:gw], gb_ref),
        (yc_ref[...], gc_ref),
        (yl_ref[:, gw:2 * gw], gd_ref),
        (yl_ref[:, 2 * gw:3 * gw], gx_ref),
    )
    for grp, (o, gate_ref) in enumerate(branches):
        cols = slice(grp * gw, (grp + 1) * gw)
        y = o.astype(F32) * jax.nn.silu(gate_ref[...])
        y = y * lax.rsqrt(jnp.mean(y * y, axis=-1, keepdims=True) + NORM_EPS)
        y_scr[:, cols] = (y * ng_ref[:, cols]).astype(BF16)
    acc = x_ref[...] + jnp.dot(y_scr[...], w_ref[...], preferred_element_type=F32)
    if final:
        acc = acc * lax.rsqrt(jnp.mean(acc * acc, axis=-1, keepdims=True) + NORM_EPS) * fg_ref[...]
    o_ref[...] = acc


def _outproj(ya, yl, yc, proj2d, out_norm_g, w_bf16, x2d, final_g, final):
    m, d = x2d.shape
    tm = OUTPROJ_TM
    gw = GROUP_WIDTH
    gate = lambda sl: pl.BlockSpec((tm, gw), lambda i: (i, sl))
    return pl.pallas_call(
        functools.partial(_outproj_kernel, final=final),
        out_shape=jax.ShapeDtypeStruct((m, d), F32),
        grid=(m // tm,),
        in_specs=[
            pl.BlockSpec((tm, gw), lambda i: (i, 0)),
            pl.BlockSpec((tm, 3 * gw), lambda i: (i, 0)),
            pl.BlockSpec((tm, gw), lambda i: (i, 0)),
            gate(SL_A_G), gate(SL_B_G), gate(SL_C_G), gate(SL_D_G), gate(SL_X_G),
            pl.BlockSpec((1, MIX_WIDTH), lambda i: (0, 0)),
            pl.BlockSpec((MIX_WIDTH, d), lambda i: (0, 0)),
            pl.BlockSpec((tm, d), lambda i: (i, 0)),
            pl.BlockSpec((1, d), lambda i: (0, 0)),
        ],
        out_specs=pl.BlockSpec((tm, d), lambda i: (i, 0)),
        scratch_shapes=[pltpu.VMEM((tm, MIX_WIDTH), BF16)],
        compiler_params=_params("parallel"),
        name="outproj_final" if final else "outproj",
    )(ya, yl, yc, proj2d, proj2d, proj2d, proj2d, proj2d,
      out_norm_g.reshape(1, MIX_WIDTH), w_bf16, x2d, final_g.reshape(1, d))


def _rotary_tables(seq):
    half = HEAD_DIM // 2
    inv_freq = ROPE_THETA ** (-jnp.arange(half, dtype=F32) / half)
    ang = jnp.arange(seq, dtype=jnp.int32).astype(F32)[:, None] * inv_freq[None, :]
    cos, sin = jnp.cos(ang), jnp.sin(ang)
    return jnp.concatenate([cos, cos], axis=-1), jnp.concatenate([-sin, sin], axis=-1)


def kernel(x, mem, norm_g, w_in, sgu_w, sgu_b, sgu_ln_g, sgu_ln_b, conv_w, mem_norm_g, w_mem_kv,
           out_norm_g, w_out, final_norm_g):
    batch, seq, d = x.shape
    depth = w_in.shape[0]
    cos_t, sin_t = _rotary_tables(seq)
    x2d = x.reshape(batch * seq, d)
    for i in range(depth):
        proj2d = _inproj(x2d, norm_g[i], w_in[i].astype(BF16), cos_t, sin_t, seq)
        proj = proj2d.reshape(batch, seq, IN_SLICES * GROUP_WIDTH)
        memkv = _memkv(mem, mem_norm_g[i], w_mem_kv[i].astype(BF16))
        ya = _dilated(proj, batch, seq)
        yc = _moba(proj, batch, seq)
        yl = _local(proj, memkv, sgu_w[i], sgu_b[i].T, sgu_ln_g[i].reshape(1, GROUP_WIDTH),
                    sgu_ln_b[i].reshape(1, GROUP_WIDTH), conv_w[i], batch, seq)
        flat = lambda t: t.reshape(batch * seq, t.shape[-1])
        x2d = _outproj(flat(ya), flat(yl), flat(yc), proj2d, out_norm_g[i], w_out[i].astype(BF16),
                       x2d, final_norm_g, final=(i == depth - 1))
    return x2d.reshape(batch, seq, d)
```

```python
import functools

import jax
import jax.numpy as jnp
from jax import lax
from jax.experimental import pallas as pl
from jax.experimental.pallas import tpu as pltpu

F32 = jnp.float32
BF16 = jnp.bfloat16

HEAD_DIM = 128
GROUP_HEADS = 4
GROUP_WIDTH = GROUP_HEADS * HEAD_DIM
N_GROUPS = 5
MIX_WIDTH = N_GROUPS * GROUP_WIDTH
IN_SLICES = 17
ROPE_THETA = 10000.0
NORM_EPS = 1e-6
NEG_INF = -1e30
ATTN_SCALE = HEAD_DIM ** -0.5
LOG2E = 1.4426950408889634
EXP2_SCALE = ATTN_SCALE * LOG2E

DIL_BAND = 128
DIL_STRIDES = (1, 4, 16)
DIL_SUPER = DIL_BAND * DIL_STRIDES[-1]
SGU_CHUNK = 128
MOBA_BLOCK = 256
MOBA_TOPK = 3
CONV_WIDTH = 3
CONV_HALO = 8

SL_A_Q, SL_A_K, SL_A_V, SL_A_G = 0, 1, 2, 3
SL_B_U, SL_B_V, SL_B_G = 4, 5, 6
SL_C_Q, SL_C_K, SL_C_V, SL_C_G = 7, 8, 9, 10
SL_D_B, SL_D_C, SL_D_H, SL_D_G = 11, 12, 13, 14
SL_X_Q, SL_X_G = 15, 16
ROTARY_SLICES = (SL_A_Q, SL_A_K, SL_C_Q, SL_C_K)

VMEM_LIMIT_BYTES = 56 * 1024 * 1024


def _params(*semantics):
    return pltpu.CompilerParams(dimension_semantics=semantics, vmem_limit_bytes=VMEM_LIMIT_BYTES)


def _dot_nt(a, b):
    return lax.dot_general(a, b, (((1,), (1,)), ((), ())), preferred_element_type=F32)


INPROJ_TM = 1024
NORM_ROWS = 128


def _inproj_kernel(x_ref, g_ref, w_ref, cos_ref, sin_ref, o_ref, h_scr):
    n = pl.program_id(1)

    @pl.when(n == 0)
    def _():
        def body(i, c):
            rows = pl.ds(pl.multiple_of(i * NORM_ROWS, NORM_ROWS), NORM_ROWS)
            x = x_ref[rows, :]
            y = x * lax.rsqrt(jnp.mean(x * x, axis=-1, keepdims=True) + NORM_EPS) * g_ref[...]
            h_scr[rows, :] = y.astype(BF16)
            return c
        lax.fori_loop(0, INPROJ_TM // NORM_ROWS, body, 0)

    o_ref[...] = jnp.dot(h_scr[...], w_ref[...], preferred_element_type=F32)

    is_rot = functools.reduce(jnp.logical_or, [n == s for s in ROTARY_SLICES])

    @pl.when(is_rot)
    def _():
        cos = cos_ref[...]
        sin = sin_ref[...]
        for hh in range(GROUP_HEADS):
            sl = slice(hh * HEAD_DIM, (hh + 1) * HEAD_DIM)
            a = o_ref[:, sl]
            o_ref[:, sl] = a * cos + pltpu.roll(a, HEAD_DIM // 2, 1) * sin


def _inproj(x2d, norm_g, w_bf16, cos_t, sin_t, seq):
    m, d = x2d.shape
    n_cols = w_bf16.shape[1]
    tm = INPROJ_TM
    seq_tiles = seq // tm
    return pl.pallas_call(
        _inproj_kernel,
        out_shape=jax.ShapeDtypeStruct((m, n_cols), F32),
        grid=(m // tm, n_cols // GROUP_WIDTH),
        in_specs=[
            pl.BlockSpec((tm, d), lambda i, n: (i, 0)),
            pl.BlockSpec((1, d), lambda i, n: (0, 0)),
            pl.BlockSpec((d, GROUP_WIDTH), lambda i, n: (0, n)),
            pl.BlockSpec((tm, HEAD_DIM), lambda i, n: (i % seq_tiles, 0)),
            pl.BlockSpec((tm, HEAD_DIM), lambda i, n: (i % seq_tiles, 0)),
        ],
        out_specs=pl.BlockSpec((tm, GROUP_WIDTH), lambda i, n: (i, n)),
        scratch_shapes=[pltpu.VMEM((tm, d), BF16)],
        compiler_params=_params("parallel", "arbitrary"),
        name="inproj",
    )(x2d, norm_g.reshape(1, d), w_bf16, cos_t, sin_t)


DIL_UNROLL = 4


def _dilated_kernel(q_ref, kp_ref, ko_ref, vp_ref, vo_ref, o_ref,
                    kn, vn, o1, o2, o3, l1, l2, l3):
    sup = DIL_SUPER
    band = DIL_BAND
    has_prev = pl.program_id(1) > 0

    kn[0:sup, :] = kp_ref[...]
    kn[sup:2 * sup, :] = ko_ref[...]
    vn[0:sup, :] = vp_ref[...]
    vn[sup:2 * sup, :] = vo_ref[...]

    qi = lax.broadcasted_iota(jnp.int32, (band, 2 * band), 0)
    kj = lax.broadcasted_iota(jnp.int32, (band, 2 * band), 1)
    in_band = (kj >= qi) & (kj <= qi + band)
    bias_any = jnp.where(in_band, 0.0, NEG_INF).astype(F32)
    bias_start = jnp.where(in_band & (kj >= band), 0.0, NEG_INF).astype(F32)

    ones = jnp.ones((2 * band, HEAD_DIM), BF16)

    def attend(qb, kb, vb, at_start):
        bias = jnp.where(at_start, bias_start, bias_any)
        t = _dot_nt(qb.astype(BF16), kb.astype(BF16)) * EXP2_SCALE + bias
        m = jnp.max(t, axis=-1, keepdims=True)
        p = jnp.exp2(t - m)
        va = jnp.concatenate([vb.astype(BF16), ones], axis=1)
        r = jnp.dot(p.astype(BF16), va, preferred_element_type=F32)
        l = r[:, HEAD_DIM:]
        return r[:, :HEAD_DIM] / l, m + jnp.log(l) * LOG2E

    def pattern(stride, o_out, l_out):
        span = band * stride

        def one_block(t):
            grp = t // stride
            res = t % stride
            q0 = grp * span + res
            k0 = sup + q0 - span
            if stride == 1:
                qsl = pl.ds(pl.multiple_of(q0, band), band)
                ksl = pl.ds(pl.multiple_of(k0, band), 2 * band)
            else:
                qsl = pl.ds(q0, band, stride=stride)
                ksl = pl.ds(k0, 2 * band, stride=stride)
            at_start = jnp.logical_and(grp == 0, jnp.logical_not(has_prev))
            o, lse = attend(q_ref[qsl, :], kn[ksl, :], vn[ksl, :], at_start)
            o_out[qsl, :] = o
            l_out[qsl, :] = lse

        def body(g, c):
            for u in range(DIL_UNROLL):
                one_block(g * DIL_UNROLL + u)
            return c
        lax.fori_loop(0, sup // band // DIL_UNROLL, body, 0)

    pattern(DIL_STRIDES[0], o1, l1)
    pattern(DIL_STRIDES[1], o2, l2)
    pattern(DIL_STRIDES[2], o3, l3)

    rows_per_step = 256

    def merge(i, c):
        rows = pl.ds(pl.multiple_of(i * rows_per_step, rows_per_step), rows_per_step)
        a1, a2, a3 = l1[rows, :], l2[rows, :], l3[rows, :]
        mx = jnp.maximum(jnp.maximum(a1, a2), a3)
        w1, w2, w3 = jnp.exp2(a1 - mx), jnp.exp2(a2 - mx), jnp.exp2(a3 - mx)
        num = w1 * o1[rows, :] + w2 * o2[rows, :] + w3 * o3[rows, :]
        o_ref[rows, :] = (num / (w1 + w2 + w3)).astype(o_ref.dtype)
        return c
    lax.fori_loop(0, sup // rows_per_step, merge, 0)


def _dilated(proj, batch, seq):
    sup = DIL_SUPER
    blk = (None, sup, HEAD_DIM)
    prev = lambda sb: jnp.maximum(sb - 1, 0)
    scr = pltpu.VMEM((sup, HEAD_DIM), F32)
    win = pltpu.VMEM((2 * sup, HEAD_DIM), F32)
    return pl.pallas_call(
        _dilated_kernel,
        out_shape=jax.ShapeDtypeStruct((batch, seq, GROUP_WIDTH), BF16),
        grid=(batch, seq // sup, GROUP_HEADS),
        in_specs=[
            pl.BlockSpec(blk, lambda b, sb, h: (b, sb, SL_A_Q * GROUP_HEADS + h)),
            pl.BlockSpec(blk, lambda b, sb, h: (b, prev(sb), SL_A_K * GROUP_HEADS + h)),
            pl.BlockSpec(blk, lambda b, sb, h: (b, sb, SL_A_K * GROUP_HEADS + h)),
            pl.BlockSpec(blk, lambda b, sb, h: (b, prev(sb), SL_A_V * GROUP_HEADS + h)),
            pl.BlockSpec(blk, lambda b, sb, h: (b, sb, SL_A_V * GROUP_HEADS + h)),
        ],
        out_specs=pl.BlockSpec(blk, lambda b, sb, h: (b, sb, h)),
        scratch_shapes=[win, win, scr, scr, scr, scr, scr, scr],
        compiler_params=_params("parallel", "parallel", "parallel"),
        name="dilated",
    )(proj, proj, proj, proj, proj)


MOBA_TQ = 2 * MOBA_BLOCK
MOBA_BLOCK_SHIFT = 8
assert 1 << MOBA_BLOCK_SHIFT == MOBA_BLOCK


def _moba_kernel(q_ref, k_ref, v_ref, o_ref, kaug, vaug, kmean, qaug, m_scr, acc_scr):
    blk = MOBA_BLOCK
    tq = MOBA_TQ
    hd = HEAD_DIM
    n_blocks = k_ref.shape[0] // blk
    ti = pl.program_id(2)

    @pl.when(ti == 0)
    def _():
        kmean[...] = jnp.zeros_like(kmean)
        lane = lax.broadcasted_iota(jnp.int32, (blk, hd), 1)
        ones = jnp.ones((blk, hd), BF16)
        for n in range(n_blocks):
            rows = slice(n * blk, (n + 1) * blk)
            kb = k_ref[rows, :]
            kmean[n:n + 1, :] = jnp.mean(kb, axis=0, keepdims=True)
            kaug[rows, 0:hd] = kb.astype(BF16)
            kaug[rows, hd:2 * hd] = jnp.where(lane == n, 1.0, 0.0).astype(BF16)
            vaug[rows, 0:hd] = v_ref[rows, :].astype(BF16)
            vaug[rows, hd:2 * hd] = ones

    q = q_ref[...]
    gate = lax.dot_general(q, kmean[...], (((1,), (1,)), ((), ())),
                           precision=lax.Precision.HIGHEST, preferred_element_type=F32)
    bidx = lax.broadcasted_iota(jnp.int32, (tq, hd), 1)
    bidx_f = bidx.astype(F32)
    row = lax.broadcasted_iota(jnp.int32, (tq, hd), 0)
    qblk = ti * (tq // blk) + lax.shift_right_logical(row, MOBA_BLOCK_SHIFT)
    g = jnp.where(bidx < qblk, gate, NEG_INF)
    sel_bias = jnp.full((tq, hd), NEG_INF, F32)
    for t in range(MOBA_TOPK):
        mx = jnp.max(g, axis=-1, keepdims=True)
        first = jnp.min(jnp.where(g == mx, bidx_f, float(hd)), axis=-1, keepdims=True)
        hit = bidx_f == first
        sel_bias = jnp.where(hit, jnp.where(t < qblk, 0.0, NEG_INF), sel_bias)
        g = jnp.where(hit, -jnp.inf, g)
    sel_bias = jnp.where(bidx == qblk, 0.0, sel_bias)
    qaug[:, 0:hd] = q.astype(BF16)
    qaug[:, hd:2 * hd] = sel_bias.astype(BF16)

    def chunk(c):
        return pl.ds(pl.multiple_of(c * tq, tq), tq)

    def scores(c):
        return _dot_nt(qaug[...], kaug[chunk(c), :])

    s = scores(ti)
    r_i = lax.broadcasted_iota(jnp.int32, (tq, tq), 0)
    c_i = lax.broadcasted_iota(jnp.int32, (tq, tq), 1)
    s = jnp.where(c_i <= r_i, s, NEG_INF)
    m = jnp.max(s, axis=-1, keepdims=True)
    p = jnp.exp2(s * EXP2_SCALE - m * EXP2_SCALE)
    m_scr[...] = m
    acc_scr[...] = jnp.dot(p.astype(BF16), vaug[chunk(ti), :], preferred_element_type=F32)

    def body(c, s):
        s_next = scores(jnp.minimum(c + 1, ti - 1))
        m_old = m_scr[...]
        m_new = jnp.maximum(m_old, jnp.max(s, axis=-1, keepdims=True))
        alpha = jnp.exp2((m_old - m_new) * EXP2_SCALE)
        p = jnp.exp2(s * EXP2_SCALE - m_new * EXP2_SCALE)
        m_scr[...] = m_new
        acc_scr[...] = alpha * acc_scr[...] + jnp.dot(
            p.astype(BF16), vaug[chunk(c), :], preferred_element_type=F32)
        return s_next
    lax.fori_loop(0, ti, body, scores(0))

    acc = acc_scr[...]
    o_ref[...] = (acc[:, 0:hd] / acc[:, hd:2 * hd]).astype(o_ref.dtype)


def _moba(proj, batch, seq):
    tq = MOBA_TQ
    hd = HEAD_DIM
    full = (None, seq, hd)
    return pl.pallas_call(
        _moba_kernel,
        out_shape=jax.ShapeDtypeStruct((batch, seq, GROUP_WIDTH), BF16),
        grid=(batch, GROUP_HEADS, seq // tq),
        in_specs=[
            pl.BlockSpec((None, tq, hd), lambda b, h, i: (b, i, SL_C_Q * GROUP_HEADS + h)),
            pl.BlockSpec(full, lambda b, h, i: (b, 0, SL_C_K * GROUP_HEADS + h)),
            pl.BlockSpec(full, lambda b, h, i: (b, 0, SL_C_V * GROUP_HEADS + h)),
        ],
        out_specs=pl.BlockSpec((None, tq, hd), lambda b, h, i: (b, i, h)),
        scratch_shapes=[
            pltpu.VMEM((seq, 2 * hd), BF16),
            pltpu.VMEM((seq, 2 * hd), BF16),
            pltpu.VMEM((hd, hd), F32),
            pltpu.VMEM((tq, 2 * hd), BF16),
            pltpu.VMEM((tq, 1), F32),
            pltpu.VMEM((tq, 2 * hd), F32),
        ],
        compiler_params=_params("parallel", "parallel", "arbitrary"),
        name="moba",
    )(proj, proj, proj)


def _memkv_kernel(mem_ref, g_ref, w_ref, o_ref):
    x = mem_ref[...]
    y = x * lax.rsqrt(jnp.mean(x * x, axis=-1, keepdims=True) + NORM_EPS) * g_ref[...]
    o_ref[...] = jnp.dot(y.astype(BF16), w_ref[...], preferred_element_type=F32).astype(o_ref.dtype)


def _memkv(mem, mem_norm_g, w_bf16):
    batch, n_mem, d = mem.shape
    n_cols = w_bf16.shape[1]
    return pl.pallas_call(
        _memkv_kernel,
        out_shape=jax.ShapeDtypeStruct((batch, n_mem, n_cols), BF16),
        grid=(batch,),
        in_specs=[
            pl.BlockSpec((None, n_mem, d), lambda b: (b, 0, 0)),
            pl.BlockSpec((1, d), lambda b: (0, 0)),
            pl.BlockSpec((d, n_cols), lambda b: (0, 0)),
        ],
        out_specs=pl.BlockSpec((None, n_mem, n_cols), lambda b: (b, 0, 0)),
        compiler_params=_params("parallel"),
        name="memkv",
    )(mem, mem_norm_g.reshape(1, d), w_bf16)


LOCAL_TM = 512


def _local_kernel(bu_ref, bv_ref, db_ref, dc_ref, dh_ref, hc_ref, hh_ref, xq_ref,
                  ws_ref, bs_ref, lng_ref, lnb_ref, cw_ref, kv_ref, o_ref, zbuf):
    tm = LOCAL_TM
    gw = GROUP_WIDTH

    u = jax.nn.gelu(bu_ref[...])
    v = jax.nn.gelu(bv_ref[...])
    mu = jnp.mean(v, axis=-1, keepdims=True)
    vc = v - mu
    var = jnp.mean(vc * vc, axis=-1, keepdims=True)
    vnorm = (vc * lax.rsqrt(var + NORM_EPS) * lng_ref[...] + lnb_ref[...]).astype(BF16)
    r = lax.broadcasted_iota(jnp.int32, (SGU_CHUNK, SGU_CHUNK), 0)
    c = lax.broadcasted_iota(jnp.int32, (SGU_CHUNK, SGU_CHUNK), 1)
    for hd in range(GROUP_HEADS):
        cols = slice(hd * HEAD_DIM, (hd + 1) * HEAD_DIM)
        w_causal = jnp.where(c <= r, ws_ref[hd], 0.0).astype(BF16)
        bias = bs_ref[:, hd:hd + 1]
        for ch in range(tm // SGU_CHUNK):
            rows = slice(ch * SGU_CHUNK, (ch + 1) * SGU_CHUNK)
            mixed = jnp.dot(w_causal, vnorm[rows, cols], preferred_element_type=F32) + bias
            o_ref[rows, cols] = (u[rows, cols] * mixed).astype(o_ref.dtype)

    first_tile = pl.program_id(1) == 0
    zbuf[0:CONV_HALO, :] = jnp.where(first_tile, 0.0, hc_ref[...] * hh_ref[...])
    zbuf[CONV_HALO:CONV_HALO + tm, :] = dc_ref[...] * dh_ref[...]
    y = cw_ref[CONV_WIDTH - 1:CONV_WIDTH, :] * zbuf[CONV_HALO:CONV_HALO + tm, :]
    for tap in range(CONV_WIDTH - 1):
        back = CONV_WIDTH - 1 - tap
        y = y + cw_ref[tap:tap + 1, :] * zbuf[CONV_HALO - back:CONV_HALO - back + tm, :]
    o_ref[:, gw:2 * gw] = (db_ref[...] * y).astype(o_ref.dtype)

    for hd in range(GROUP_HEADS):
        cols = slice(hd * HEAD_DIM, (hd + 1) * HEAD_DIM)
        kh = kv_ref[:, hd * HEAD_DIM:(hd + 1) * HEAD_DIM]
        vh = kv_ref[:, gw + hd * HEAD_DIM:gw + (hd + 1) * HEAD_DIM]
        s = _dot_nt(xq_ref[:, cols].astype(BF16), kh) * ATTN_SCALE
        m = jnp.max(s, axis=-1, keepdims=True)
        p = jnp.exp(s - m)
        l = jnp.sum(p, axis=-1, keepdims=True)
        o = jnp.dot(p.astype(BF16), vh, preferred_element_type=F32) / l
        o_ref[:, 2 * gw + hd * HEAD_DIM:2 * gw + (hd + 1) * HEAD_DIM] = o.astype(o_ref.dtype)


def _local(proj, memkv, sgu_w, sgu_b_t, ln_g, ln_b, conv_w, batch, seq):
    tm = LOCAL_TM
    gw = GROUP_WIDTH
    halo_per_tile = tm // CONV_HALO
    tile = lambda sl: pl.BlockSpec((None, tm, gw), lambda b, i: (b, i, sl))
    halo = lambda sl: pl.BlockSpec(
        (None, CONV_HALO, gw), lambda b, i: (b, jnp.maximum(i * halo_per_tile - 1, 0), sl))
    whole = lambda a: pl.BlockSpec(a.shape, lambda b, i: (0,) * a.ndim)
    n_mem = memkv.shape[1]
    return pl.pallas_call(
        _local_kernel,
        out_shape=jax.ShapeDtypeStruct((batch, seq, 3 * gw), BF16),
        grid=(batch, seq // tm),
        in_specs=[
            tile(SL_B_U), tile(SL_B_V), tile(SL_D_B), tile(SL_D_C), tile(SL_D_H),
            halo(SL_D_C), halo(SL_D_H), tile(SL_X_Q),
            whole(sgu_w), whole(sgu_b_t), whole(ln_g), whole(ln_b), whole(conv_w),
            pl.BlockSpec((None, n_mem, 2 * gw), lambda b, i: (b, 0, 0)),
        ],
        out_specs=pl.BlockSpec((None, tm, 3 * gw), lambda b, i: (b, i, 0)),
        scratch_shapes=[pltpu.VMEM((CONV_HALO + tm, gw), F32)],
        compiler_params=_params("parallel", "parallel"),
        name="local",
    )(proj, proj, proj, proj, proj, proj, proj, proj,
      sgu_w, sgu_b_t, ln_g, ln_b, conv_w, memkv)


OUTPROJ_TM = 256


def _outproj_kernel(ya_ref, yl_ref, yc_ref, ga_ref, gb_ref, gc_ref, gd_ref, gx_ref,
                    ng_ref, w_ref, x_ref, fg_ref, o_ref, y_scr, *, final):
    gw = GROUP_WIDTH
    branches = (
        (ya_ref[...], ga_ref),
        (yl_ref[:, 0:gw], gb_ref),
        (yc_ref[...], gc_ref),
        (yl_ref[:, gw:2 * gw], gd_ref),
        (yl_ref[:, 2 * gw:3 * gw], gx_ref),
    )
    for grp, (o, gate_ref) in enumerate(branches):
        cols = slice(grp * gw, (grp + 1) * gw)
        y = o.astype(F32) * jax.nn.silu(gate_ref[...])
        y = y * lax.rsqrt(jnp.mean(y * y, axis=-1, keepdims=True) + NORM_EPS)
        y_scr[:, cols] = (y * ng_ref[:, cols]).astype(BF16)
    acc = x_ref[...] + jnp.dot(y_scr[...], w_ref[...], preferred_element_type=F32)
    if final:
        acc = acc * lax.rsqrt(jnp.mean(acc * acc, axis=-1, keepdims=True) + NORM_EPS) * fg_ref[...]
    o_ref[...] = acc


def _outproj(ya, yl, yc, proj2d, out_norm_g, w_bf16, x2d, final_g, final):
    m, d = x2d.shape
    tm = OUTPROJ_TM
    gw = GROUP_WIDTH
    gate = lambda sl: pl.BlockSpec((tm, gw), lambda i: (i, sl))
    return pl.pallas_call(
        functools.partial(_outproj_kernel, final=final),
        out_shape=jax.ShapeDtypeStruct((m, d), F32),
        grid=(m // tm,),
        in_specs=[
            pl.BlockSpec((tm, gw), lambda i: (i, 0)),
            pl.BlockSpec((tm, 3 * gw), lambda i: (i, 0)),
            pl.BlockSpec((tm, gw), lambda i: (i, 0)),
            gate(SL_A_G), gate(SL_B_G), gate(SL_C_G), gate(SL_D_G), gate(SL_X_G),
            pl.BlockSpec((1, MIX_WIDTH), lambda i: (0, 0)),
            pl.BlockSpec((MIX_WIDTH, d), lambda i: (0, 0)),
            pl.BlockSpec((tm, d), lambda i: (i, 0)),
            pl.BlockSpec((1, d), lambda i: (0, 0)),
        ],
        out_specs=pl.BlockSpec((tm, d), lambda i: (i, 0)),
        scratch_shapes=[pltpu.VMEM((tm, MIX_WIDTH), BF16)],
        compiler_params=_params("parallel"),
        name="outproj_final" if final else "outproj",
    )(ya, yl, yc, proj2d, proj2d, proj2d, proj2d, proj2d,
      out_norm_g.reshape(1, MIX_WIDTH), w_bf16, x2d, final_g.reshape(1, d))


def _rotary_tables(seq):
    half = HEAD_DIM // 2
    inv_freq = ROPE_THETA ** (-jnp.arange(half, dtype=F32) / half)
    ang = jnp.arange(seq, dtype=jnp.int32).astype(F32)[:, None] * inv_freq[None, :]
    cos, sin = jnp.cos(ang), jnp.sin(ang)
    return jnp.concatenate([cos, cos], axis=-1), jnp.concatenate([-sin, sin], axis=-1)


def kernel(x, mem, norm_g, w_in, sgu_w, sgu_b, sgu_ln_g, sgu_ln_b, conv_w, mem_norm_g, w_mem_kv,
           out_norm_g, w_out, final_norm_g):
    batch, seq, d = x.shape
    depth = w_in.shape[0]
    cos_t, sin_t = _rotary_tables(seq)
    x2d = x.reshape(batch * seq, d)
    for i in range(depth):
        proj2d = _inproj(x2d, norm_g[i], w_in[i].astype(BF16), cos_t, sin_t, seq)
        proj = proj2d.reshape(batch, seq, IN_SLICES * GROUP_WIDTH)
        memkv = _memkv(mem, mem_norm_g[i], w_mem_kv[i].astype(BF16))
        ya = _dilated(proj, batch, seq)
        yc = _moba(proj, batch, seq)
        yl = _local(proj, memkv, sgu_w[i], sgu_b[i].T, sgu_ln_g[i].reshape(1, GROUP_WIDTH),
                    sgu_ln_b[i].reshape(1, GROUP_WIDTH), conv_w[i], batch, seq)
        flat = lambda t: t.reshape(batch * seq, t.shape[-1])
        x2d = _outproj(flat(ya), flat(yl), flat(yc), proj2d, out_norm_g[i], w_out[i].astype(BF16),
                       x2d, final_norm_g, final=(i == depth - 1))
    return x2d.reshape(batch, seq, d)
```

```python
import functools

import jax
import jax.numpy as jnp
from jax import lax
from jax.experimental import pallas as pl
from jax.experimental.pallas import tpu as pltpu

F32 = jnp.float32
BF16 = jnp.bfloat16

HEAD_DIM = 128
GROUP_HEADS = 4
GROUP_WIDTH = GROUP_HEADS * HEAD_DIM
N_GROUPS = 5
MIX_WIDTH = N_GROUPS * GROUP_WIDTH
IN_SLICES = 17
ROPE_THETA = 10000.0
NORM_EPS = 1e-6
NEG_INF = -1e30
ATTN_SCALE = HEAD_DIM ** -0.5
LOG2E = 1.4426950408889634
EXP2_SCALE = ATTN_SCALE * LOG2E

DIL_BAND = 128
DIL_STRIDES = (1, 4, 16)
DIL_SUPER = DIL_BAND * DIL_STRIDES[-1]
SGU_CHUNK = 128
MOBA_BLOCK = 256
MOBA_TOPK = 3
CONV_WIDTH = 3
CONV_HALO = 16

SL_A_Q, SL_A_K, SL_A_V, SL_A_G = 0, 1, 2, 3
SL_B_U, SL_B_V, SL_B_G = 4, 5, 6
SL_C_Q, SL_C_K, SL_C_V, SL_C_G = 7, 8, 9, 10
SL_D_B, SL_D_C, SL_D_H, SL_D_G = 11, 12, 13, 14
SL_X_Q, SL_X_G = 15, 16
ROTARY_SLICES = (SL_A_Q, SL_A_K, SL_C_Q, SL_C_K)
N_F32_SLICES = 3


def _bf16_slice(sl):
    assert sl >= N_F32_SLICES
    return sl - N_F32_SLICES


VMEM_LIMIT_BYTES = 56 * 1024 * 1024


def _params(*semantics):
    return pltpu.CompilerParams(dimension_semantics=semantics, vmem_limit_bytes=VMEM_LIMIT_BYTES)


def _dot_nt(a, b):
    return lax.dot_general(a, b, (((1,), (1,)), ((), ())), preferred_element_type=F32)


NORM_TM = 512


def _rmsnorm_kernel(x_ref, g_ref, o_ref):
    x = x_ref[...]
    y = x * lax.rsqrt(jnp.mean(x * x, axis=-1, keepdims=True) + NORM_EPS) * g_ref[...]
    o_ref[...] = y.astype(o_ref.dtype)


def _rmsnorm(x2d, g):
    m, d = x2d.shape
    return pl.pallas_call(
        _rmsnorm_kernel,
        out_shape=jax.ShapeDtypeStruct((m, d), BF16),
        grid=(m // NORM_TM,),
        in_specs=[pl.BlockSpec((NORM_TM, d), lambda i: (i, 0)), pl.BlockSpec((1, d), lambda i: (0, 0))],
        out_specs=pl.BlockSpec((NORM_TM, d), lambda i: (i, 0)),
        compiler_params=_params("parallel"),
        name="rmsnorm",
    )(x2d, g.reshape(1, d))


INPROJ_TM = 2048


def _inproj_kernel(h_ref, w_ref, cos_ref, sin_ref, of_ref, ob_ref, acc):
    n = pl.program_id(1)
    acc[...] = jnp.dot(h_ref[...], w_ref[...].astype(BF16), preferred_element_type=F32)

    def emit(o_ref, rotary):
        if not rotary:
            o_ref[...] = acc[...].astype(o_ref.dtype)
            return
        cos = cos_ref[...]
        sin = sin_ref[...]
        for hh in range(GROUP_HEADS):
            sl = slice(hh * HEAD_DIM, (hh + 1) * HEAD_DIM)
            a = acc[:, sl]
            o_ref[:, sl] = (a * cos + pltpu.roll(a, HEAD_DIM // 2, 1) * sin).astype(o_ref.dtype)

    is_f32 = n < N_F32_SLICES
    is_rot = functools.reduce(jnp.logical_or, [n == s for s in ROTARY_SLICES])
    for f32_out in (True, False):
        for rotary in (True, False):
            cond = jnp.logical_and(is_f32 == f32_out, is_rot == rotary)
            pl.when(cond)(functools.partial(emit, of_ref if f32_out else ob_ref, rotary))


def _inproj(h2d, w_f32, cos_t, sin_t, seq):
    m, d = h2d.shape
    gw = GROUP_WIDTH
    n_slices = w_f32.shape[1] // gw
    tm = INPROJ_TM
    seq_tiles = seq // tm
    return pl.pallas_call(
        _inproj_kernel,
        out_shape=(jax.ShapeDtypeStruct((m, N_F32_SLICES * gw), F32),
                   jax.ShapeDtypeStruct((m, (n_slices - N_F32_SLICES) * gw), BF16)),
        grid=(m // tm, n_slices),
        in_specs=[
            pl.BlockSpec((tm, d), lambda i, n: (i, 0)),
            pl.BlockSpec((d, gw), lambda i, n: (0, n)),
            pl.BlockSpec((tm, HEAD_DIM), lambda i, n: (i % seq_tiles, 0)),
            pl.BlockSpec((tm, HEAD_DIM), lambda i, n: (i % seq_tiles, 0)),
        ],
        out_specs=(pl.BlockSpec((tm, gw), lambda i, n: (i, jnp.minimum(n, N_F32_SLICES - 1))),
                   pl.BlockSpec((tm, gw), lambda i, n: (i, jnp.maximum(n - N_F32_SLICES, 0)))),
        scratch_shapes=[pltpu.VMEM((tm, gw), F32)],
        compiler_params=_params("parallel", "arbitrary"),
        name="inproj",
    )(h2d, w_f32, cos_t, sin_t)


DIL_UNROLL = 4


def _dilated_kernel(q_ref, kp_ref, ko_ref, vp_ref, vo_ref, o_ref,
                    kn, vn, o1, o2, o3, l1, l2, l3):
    sup = DIL_SUPER
    band = DIL_BAND
    has_prev = pl.program_id(1) > 0

    kn[0:sup, :] = kp_ref[...]
    kn[sup:2 * sup, :] = ko_ref[...]
    vn[0:sup, :] = vp_ref[...]
    vn[sup:2 * sup, :] = vo_ref[...]

    qi = lax.broadcasted_iota(jnp.int32, (band, 2 * band), 0)
    kj = lax.broadcasted_iota(jnp.int32, (band, 2 * band), 1)
    in_band = (kj >= qi) & (kj <= qi + band)
    bias_any = jnp.where(in_band, 0.0, NEG_INF).astype(F32)
    bias_start = jnp.where(in_band & (kj >= band), 0.0, NEG_INF).astype(F32)

    ones = jnp.ones((2 * band, HEAD_DIM), BF16)

    def attend(qb, kb, vb, at_start):
        bias = jnp.where(at_start, bias_start, bias_any)
        t = _dot_nt(qb.astype(BF16), kb.astype(BF16)) * EXP2_SCALE + bias
        m = jnp.max(t, axis=-1, keepdims=True)
        p = jnp.exp2(t - m)
        va = jnp.concatenate([vb.astype(BF16), ones], axis=1)
        r = jnp.dot(p.astype(BF16), va, preferred_element_type=F32)
        l = r[:, HEAD_DIM:]
        return r[:, :HEAD_DIM] / l, m + jnp.log(l) * LOG2E

    def pattern(stride, o_out, l_out):
        span = band * stride

        def one_block(t):
            grp = t // stride
            res = t % stride
            q0 = grp * span + res
            k0 = sup + q0 - span
            if stride == 1:
                qsl = pl.ds(pl.multiple_of(q0, band), band)
                ksl = pl.ds(pl.multiple_of(k0, band), 2 * band)
            else:
                qsl = pl.ds(q0, band, stride=stride)
                ksl = pl.ds(k0, 2 * band, stride=stride)
            at_start = jnp.logical_and(grp == 0, jnp.logical_not(has_prev))
            o, lse = attend(q_ref[qsl, :], kn[ksl, :], vn[ksl, :], at_start)
            o_out[qsl, :] = o
            l_out[qsl, :] = lse

        def body(g, c):
            for u in range(DIL_UNROLL):
                one_block(g * DIL_UNROLL + u)
            return c
        lax.fori_loop(0, sup // band // DIL_UNROLL, body, 0)

    pattern(DIL_STRIDES[0], o1, l1)
    pattern(DIL_STRIDES[1], o2, l2)
    pattern(DIL_STRIDES[2], o3, l3)

    rows_per_step = 256

    def merge(i, c):
        rows = pl.ds(pl.multiple_of(i * rows_per_step, rows_per_step), rows_per_step)
        a1, a2, a3 = l1[rows, :], l2[rows, :], l3[rows, :]
        mx = jnp.maximum(jnp.maximum(a1, a2), a3)
        w1, w2, w3 = jnp.exp2(a1 - mx), jnp.exp2(a2 - mx), jnp.exp2(a3 - mx)
        num = w1 * o1[rows, :] + w2 * o2[rows, :] + w3 * o3[rows, :]
        o_ref[rows, :] = (num / (w1 + w2 + w3)).astype(o_ref.dtype)
        return c
    lax.fori_loop(0, sup // rows_per_step, merge, 0)


def _dilated(proj, batch, seq):
    sup = DIL_SUPER
    blk = (None, sup, HEAD_DIM)
    prev = lambda sb: jnp.maximum(sb - 1, 0)
    scr = pltpu.VMEM((sup, HEAD_DIM), F32)
    win = pltpu.VMEM((2 * sup, HEAD_DIM), F32)
    return pl.pallas_call(
        _dilated_kernel,
        out_shape=jax.ShapeDtypeStruct((batch, seq, GROUP_WIDTH), BF16),
        grid=(batch, seq // sup, GROUP_HEADS),
        in_specs=[
            pl.BlockSpec(blk, lambda b, sb, h: (b, sb, SL_A_Q * GROUP_HEADS + h)),
            pl.BlockSpec(blk, lambda b, sb, h: (b, prev(sb), SL_A_K * GROUP_HEADS + h)),
            pl.BlockSpec(blk, lambda b, sb, h: (b, sb, SL_A_K * GROUP_HEADS + h)),
            pl.BlockSpec(blk, lambda b, sb, h: (b, prev(sb), SL_A_V * GROUP_HEADS + h)),
            pl.BlockSpec(blk, lambda b, sb, h: (b, sb, SL_A_V * GROUP_HEADS + h)),
        ],
        out_specs=pl.BlockSpec(blk, lambda b, sb, h: (b, sb, h)),
        scratch_shapes=[win, win, scr, scr, scr, scr, scr, scr],
        compiler_params=_params("parallel", "parallel", "parallel"),
        name="dilated",
    )(proj, proj, proj, proj, proj)


MOBA_TQ = 2 * MOBA_BLOCK
MOBA_BLOCK_SHIFT = 8
assert 1 << MOBA_BLOCK_SHIFT == MOBA_BLOCK


def _moba_kernel(q_ref, k_ref, v_ref, o_ref, kaug, vaug, kmean, qaug, m_scr, acc_scr):
    blk = MOBA_BLOCK
    tq = MOBA_TQ
    hd = HEAD_DIM
    n_blocks = k_ref.shape[0] // blk
    ti = pl.program_id(2)

    @pl.when(ti == 0)
    def _():
        kmean[...] = jnp.zeros_like(kmean)
        lane = lax.broadcasted_iota(jnp.int32, (blk, hd), 1)
        ones = jnp.ones((blk, hd), BF16)
        for n in range(n_blocks):
            rows = slice(n * blk, (n + 1) * blk)
            kb = k_ref[rows, :]
            kmean[n:n + 1, :] = jnp.mean(kb.astype(F32), axis=0, keepdims=True)
            kaug[rows, 0:hd] = kb
            kaug[rows, hd:2 * hd] = jnp.where(lane == n, 1.0, 0.0).astype(BF16)
            vaug[rows, 0:hd] = v_ref[rows, :]
            vaug[rows, hd:2 * hd] = ones

    q = q_ref[...]
    gate = lax.dot_general(q.astype(F32), kmean[...], (((1,), (1,)), ((), ())),
                           precision=lax.Precision.HIGHEST, preferred_element_type=F32)
    bidx = lax.broadcasted_iota(jnp.int32, (tq, hd), 1)
    bidx_f = bidx.astype(F32)
    row = lax.broadcasted_iota(jnp.int32, (tq, hd), 0)
    qblk = ti * (tq // blk) + lax.shift_right_logical(row, MOBA_BLOCK_SHIFT)
    g = jnp.where(bidx < qblk, gate, NEG_INF)
    sel_bias = jnp.full((tq, hd), NEG_INF, F32)
    for t in range(MOBA_TOPK):
        mx = jnp.max(g, axis=-1, keepdims=True)
        first = jnp.min(jnp.where(g == mx, bidx_f, float(hd)), axis=-1, keepdims=True)
        hit = bidx_f == first
        sel_bias = jnp.where(hit, jnp.where(t < qblk, 0.0, NEG_INF), sel_bias)
        g = jnp.where(hit, -jnp.inf, g)
    sel_bias = jnp.where(bidx == qblk, 0.0, sel_bias)
    qaug[:, 0:hd] = q
    qaug[:, hd:2 * hd] = sel_bias.astype(BF16)

    def chunk(c):
        return pl.ds(pl.multiple_of(c * tq, tq), tq)

    def scores(c):
        return _dot_nt(qaug[...], kaug[chunk(c), :])

    s = scores(ti)
    r_i = lax.broadcasted_iota(jnp.int32, (tq, tq), 0)
    c_i = lax.broadcasted_iota(jnp.int32, (tq, tq), 1)
    s = jnp.where(c_i <= r_i, s, NEG_INF)
    m = jnp.max(s, axis=-1, keepdims=True)
    p = jnp.exp2(s * EXP2_SCALE - m * EXP2_SCALE)
    m_scr[...] = m
    acc_scr[...] = jnp.dot(p.astype(BF16), vaug[chunk(ti), :], preferred_element_type=F32)

    def body(c, s):
        s_next = scores(jnp.minimum(c + 1, ti - 1))
        m_old = m_scr[...]
        m_new = jnp.maximum(m_old, jnp.max(s, axis=-1, keepdims=True))
        alpha = jnp.exp2((m_old - m_new) * EXP2_SCALE)
        p = jnp.exp2(s * EXP2_SCALE - m_new * EXP2_SCALE)
        m_scr[...] = m_new
        acc_scr[...] = alpha * acc_scr[...] + jnp.dot(
            p.astype(BF16), vaug[chunk(c), :], preferred_element_type=F32)
        return s_next
    lax.fori_loop(0, ti, body, scores(0))

    acc = acc_scr[...]
    o_ref[...] = (acc[:, 0:hd] / acc[:, hd:2 * hd]).astype(o_ref.dtype)


def _moba(proj, batch, seq):
    tq = MOBA_TQ
    hd = HEAD_DIM
    full = (None, seq, hd)
    return pl.pallas_call(
        _moba_kernel,
        out_shape=jax.ShapeDtypeStruct((batch, seq, GROUP_WIDTH), BF16),
        grid=(batch, GROUP_HEADS, seq // tq),
        in_specs=[
            pl.BlockSpec((None, tq, hd), lambda b, h, i: (b, i, _bf16_slice(SL_C_Q) * GROUP_HEADS + h)),
            pl.BlockSpec(full, lambda b, h, i: (b, 0, _bf16_slice(SL_C_K) * GROUP_HEADS + h)),
            pl.BlockSpec(full, lambda b, h, i: (b, 0, _bf16_slice(SL_C_V) * GROUP_HEADS + h)),
        ],
        out_specs=pl.BlockSpec((None, tq, hd), lambda b, h, i: (b, i, h)),
        scratch_shapes=[
            pltpu.VMEM((seq, 2 * hd), BF16),
            pltpu.VMEM((seq, 2 * hd), BF16),
            pltpu.VMEM((hd, hd), F32),
            pltpu.VMEM((tq, 2 * hd), BF16),
            pltpu.VMEM((tq, 1), F32),
            pltpu.VMEM((tq, 2 * hd), F32),
        ],
        compiler_params=_params("parallel", "parallel", "arbitrary"),
        name="moba",
    )(proj, proj, proj)


def _memkv_kernel(mem_ref, g_ref, w_ref, o_ref):
    x = mem_ref[...]
    y = x * lax.rsqrt(jnp.mean(x * x, axis=-1, keepdims=True) + NORM_EPS) * g_ref[...]
    o_ref[...] = jnp.dot(y.astype(BF16), w_ref[...], preferred_element_type=F32).astype(o_ref.dtype)


def _memkv(mem, mem_norm_g, w_bf16):
    batch, n_mem, d = mem.shape
    n_cols = w_bf16.shape[1]
    return pl.pallas_call(
        _memkv_kernel,
        out_shape=jax.ShapeDtypeStruct((batch, n_mem, n_cols), BF16),
        grid=(batch,),
        in_specs=[
            pl.BlockSpec((None, n_mem, d), lambda b: (b, 0, 0)),
            pl.BlockSpec((1, d), lambda b: (0, 0)),
            pl.BlockSpec((d, n_cols), lambda b: (0, 0)),
        ],
        out_specs=pl.BlockSpec((None, n_mem, n_cols), lambda b: (b, 0, 0)),
        compiler_params=_params("parallel"),
        name="memkv",
    )(mem, mem_norm_g.reshape(1, d), w_bf16)


LOCAL_TM = 512


def _local_kernel(bu_ref, bv_ref, db_ref, dc_ref, dh_ref, hc_ref, hh_ref, xq_ref,
                  ws_ref, bs_ref, lng_ref, lnb_ref, cw_ref, kv_ref, o_ref, zbuf):
    tm = LOCAL_TM
    gw = GROUP_WIDTH

    u = jax.nn.gelu(bu_ref[...].astype(F32))
    v = jax.nn.gelu(bv_ref[...].astype(F32))
    mu = jnp.mean(v, axis=-1, keepdims=True)
    vc = v - mu
    var = jnp.mean(vc * vc, axis=-1, keepdims=True)
    vnorm = (vc * lax.rsqrt(var + NORM_EPS) * lng_ref[...] + lnb_ref[...]).astype(BF16)
    r = lax.broadcasted_iota(jnp.int32, (SGU_CHUNK, SGU_CHUNK), 0)
    c = lax.broadcasted_iota(jnp.int32, (SGU_CHUNK, SGU_CHUNK), 1)
    for hd in range(GROUP_HEADS):
        cols = slice(hd * HEAD_DIM, (hd + 1) * HEAD_DIM)
        w_causal = jnp.where(c <= r, ws_ref[hd], 0.0).astype(BF16)
        bias = bs_ref[:, hd:hd + 1]
        for ch in range(tm // SGU_CHUNK):
            rows = slice(ch * SGU_CHUNK, (ch + 1) * SGU_CHUNK)
            mixed = jnp.dot(w_causal, vnorm[rows, cols], preferred_element_type=F32) + bias
            o_ref[rows, cols] = (u[rows, cols] * mixed).astype(o_ref.dtype)

    first_tile = pl.program_id(1) == 0
    zbuf[0:CONV_HALO, :] = jnp.where(first_tile, 0.0, hc_ref[...].astype(F32) * hh_ref[...].astype(F32))
    zbuf[CONV_HALO:CONV_HALO + tm, :] = dc_ref[...].astype(F32) * dh_ref[...].astype(F32)
    y = cw_ref[CONV_WIDTH - 1:CONV_WIDTH, :] * zbuf[CONV_HALO:CONV_HALO + tm, :]
    for tap in range(CONV_WIDTH - 1):
        back = CONV_WIDTH - 1 - tap
        y = y + cw_ref[tap:tap + 1, :] * zbuf[CONV_HALO - back:CONV_HALO - back + tm, :]
    o_ref[:, gw:2 * gw] = (db_ref[...].astype(F32) * y).astype(o_ref.dtype)

    for hd in range(GROUP_HEADS):
        cols = slice(hd * HEAD_DIM, (hd + 1) * HEAD_DIM)
        kh = kv_ref[:, hd * HEAD_DIM:(hd + 1) * HEAD_DIM]
        vh = kv_ref[:, gw + hd * HEAD_DIM:gw + (hd + 1) * HEAD_DIM]
        s = _dot_nt(xq_ref[:, cols], kh) * ATTN_SCALE
        m = jnp.max(s, axis=-1, keepdims=True)
        p = jnp.exp(s - m)
        l = jnp.sum(p, axis=-1, keepdims=True)
        o = jnp.dot(p.astype(BF16), vh, preferred_element_type=F32) / l
        o_ref[:, 2 * gw + hd * HEAD_DIM:2 * gw + (hd + 1) * HEAD_DIM] = o.astype(o_ref.dtype)


def _local(proj, memkv, sgu_w, sgu_b_t, ln_g, ln_b, conv_w, batch, seq):
    tm = LOCAL_TM
    gw = GROUP_WIDTH
    halo_per_tile = tm // CONV_HALO
    tile = lambda sl: pl.BlockSpec((None, tm, gw), lambda b, i: (b, i, _bf16_slice(sl)))
    halo = lambda sl: pl.BlockSpec(
        (None, CONV_HALO, gw), lambda b, i: (b, jnp.maximum(i * halo_per_tile - 1, 0), _bf16_slice(sl)))
    whole = lambda a: pl.BlockSpec(a.shape, lambda b, i: (0,) * a.ndim)
    n_mem = memkv.shape[1]
    return pl.pallas_call(
        _local_kernel,
        out_shape=jax.ShapeDtypeStruct((batch, seq, 3 * gw), BF16),
        grid=(batch, seq // tm),
        in_specs=[
            tile(SL_B_U), tile(SL_B_V), tile(SL_D_B), tile(SL_D_C), tile(SL_D_H),
            halo(SL_D_C), halo(SL_D_H), tile(SL_X_Q),
            whole(sgu_w), whole(sgu_b_t), whole(ln_g), whole(ln_b), whole(conv_w),
            pl.BlockSpec((None, n_mem, 2 * gw), lambda b, i: (b, 0, 0)),
        ],
        out_specs=pl.BlockSpec((None, tm, 3 * gw), lambda b, i: (b, i, 0)),
        scratch_shapes=[pltpu.VMEM((CONV_HALO + tm, gw), F32)],
        compiler_params=_params("parallel", "parallel"),
        name="local",
    )(proj, proj, proj, proj, proj, proj, proj, proj,
      sgu_w, sgu_b_t, ln_g, ln_b, conv_w, memkv)


OUTPROJ_TM = 256


def _outproj_kernel(ya_ref, yl_ref, yc_ref, ga_ref, gb_ref, gc_ref, gd_ref, gx_ref,
                    ng_ref, w_ref, x_ref, pg_ref, *rest, final):
    if final:
        o_ref, y_scr = rest
    else:
        o_ref, h_ref, y_scr = rest
    gw = GROUP_WIDTH
    branches = (
        (ya_ref[...], ga_ref),
        (yl_ref[:, 0:gw], gb_ref),
        (yc_ref[...], gc_ref),
        (yl_ref[:, gw:2 * gw], gd_ref),
        (yl_ref[:, 2 * gw:3 * gw], gx_ref),
    )
    for grp, (o, gate_ref) in enumerate(branches):
        cols = slice(grp * gw, (grp + 1) * gw)
        y = o.astype(F32) * jax.nn.silu(gate_ref[...].astype(F32))
        y = y * lax.rsqrt(jnp.mean(y * y, axis=-1, keepdims=True) + NORM_EPS)
        y_scr[:, cols] = (y * ng_ref[:, cols]).astype(BF16)
    acc = x_ref[...] + jnp.dot(y_scr[...], w_ref[...], preferred_element_type=F32)
    normed = acc * lax.rsqrt(jnp.mean(acc * acc, axis=-1, keepdims=True) + NORM_EPS) * pg_ref[...]
    if final:
        o_ref[...] = normed
    else:
        o_ref[...] = acc
        h_ref[...] = normed.astype(h_ref.dtype)


def _outproj(ya, yl, yc, proj_b, out_norm_g, w_bf16, x2d, post_g, final):
    m, d = x2d.shape
    tm = OUTPROJ_TM
    gw = GROUP_WIDTH
    gate = lambda sl: pl.BlockSpec((tm, gw), lambda i: (i, _bf16_slice(sl)))
    row_tile = pl.BlockSpec((tm, d), lambda i: (i, 0))
    x_out = jax.ShapeDtypeStruct((m, d), F32)
    return pl.pallas_call(
        functools.partial(_outproj_kernel, final=final),
        out_shape=x_out if final else (x_out, jax.ShapeDtypeStruct((m, d), BF16)),
        grid=(m // tm,),
        in_specs=[
            pl.BlockSpec((tm, gw), lambda i: (i, 0)),
            pl.BlockSpec((tm, 3 * gw), lambda i: (i, 0)),
            pl.BlockSpec((tm, gw), lambda i: (i, 0)),
            gate(SL_A_G), gate(SL_B_G), gate(SL_C_G), gate(SL_D_G), gate(SL_X_G),
            pl.BlockSpec((1, MIX_WIDTH), lambda i: (0, 0)),
            pl.BlockSpec((MIX_WIDTH, d), lambda i: (0, 0)),
            pl.BlockSpec((tm, d), lambda i: (i, 0)),
            pl.BlockSpec((1, d), lambda i: (0, 0)),
        ],
        out_specs=row_tile if final else (row_tile, row_tile),
        scratch_shapes=[pltpu.VMEM((tm, MIX_WIDTH), BF16)],
        compiler_params=_params("parallel"),
        name="outproj_final" if final else "outproj",
    )(ya, yl, yc, proj_b, proj_b, proj_b, proj_b, proj_b,
      out_norm_g.reshape(1, MIX_WIDTH), w_bf16, x2d, post_g.reshape(1, d))


def _rotary_tables(seq):
    half = HEAD_DIM // 2
    inv_freq = ROPE_THETA ** (-jnp.arange(half, dtype=F32) / half)
    ang = jnp.arange(seq, dtype=jnp.int32).astype(F32)[:, None] * inv_freq[None, :]
    cos, sin = jnp.cos(ang), jnp.sin(ang)
    return jnp.concatenate([cos, cos], axis=-1), jnp.concatenate([-sin, sin], axis=-1)


def kernel(x, mem, norm_g, w_in, sgu_w, sgu_b, sgu_ln_g, sgu_ln_b, conv_w, mem_norm_g, w_mem_kv,
           out_norm_g, w_out, final_norm_g):
    batch, seq, d = x.shape
    depth = w_in.shape[0]
    cos_t, sin_t = _rotary_tables(seq)
    x2d = x.reshape(batch * seq, d)
    h2d = _rmsnorm(x2d, norm_g[0])
    flat = lambda t: t.reshape(batch * seq, t.shape[-1])
    for i in range(depth):
        final = i == depth - 1
        proj_f, proj_b = _inproj(h2d, w_in[i], cos_t, sin_t, seq)
        proj_a = proj_f.reshape(batch, seq, proj_f.shape[-1])
        proj_r = proj_b.reshape(batch, seq, proj_b.shape[-1])
        memkv = _memkv(mem, mem_norm_g[i], w_mem_kv[i].astype(BF16))
        ya = _dilated(proj_a, batch, seq)
        yc = _moba(proj_r, batch, seq)
        yl = _local(proj_r, memkv, sgu_w[i], sgu_b[i].T, sgu_ln_g[i].reshape(1, GROUP_WIDTH),
                    sgu_ln_b[i].reshape(1, GROUP_WIDTH), conv_w[i], batch, seq)
        out = _outproj(flat(ya), flat(yl), flat(yc), proj_b, out_norm_g[i], w_out[i].astype(BF16),
                       x2d, final_norm_g if final else norm_g[i + 1], final)
        if final:
            x2d = out
        else:
            x2d, h2d = out
    return x2d.reshape(batch, seq, d)
```

```python
import functools

import jax
import jax.numpy as jnp
from jax import lax
from jax.experimental import pallas as pl
from jax.experimental.pallas import tpu as pltpu

F32 = jnp.float32
BF16 = jnp.bfloat16

HEAD_DIM = 128
GROUP_HEADS = 4
GROUP_WIDTH = GROUP_HEADS * HEAD_DIM
N_GROUPS = 5
MIX_WIDTH = N_GROUPS * GROUP_WIDTH
IN_SLICES = 17
ROPE_THETA = 10000.0
NORM_EPS = 1e-6
NEG_INF = -1e30
ATTN_SCALE = HEAD_DIM ** -0.5
LOG2E = 1.4426950408889634
EXP2_SCALE = ATTN_SCALE * LOG2E

DIL_BAND = 128
DIL_STRIDES = (1, 4, 16)
DIL_SUPER = DIL_BAND * DIL_STRIDES[-1]
SGU_CHUNK = 128
MOBA_BLOCK = 256
MOBA_TOPK = 3
CONV_WIDTH = 3
CONV_HALO = 16

SL_A_Q, SL_A_K, SL_A_V, SL_A_G = 0, 1, 2, 3
SL_B_U, SL_B_V, SL_B_G = 4, 5, 6
SL_C_Q, SL_C_K, SL_C_V, SL_C_G = 7, 8, 9, 10
SL_D_B, SL_D_C, SL_D_H, SL_D_G = 11, 12, 13, 14
SL_X_Q, SL_X_G = 15, 16
ROTARY_SLICES = (SL_A_Q, SL_A_K, SL_C_Q, SL_C_K)
N_F32_SLICES = 3


def _bf16_slice(sl):
    assert sl >= N_F32_SLICES
    return sl - N_F32_SLICES


VMEM_LIMIT_BYTES = 56 * 1024 * 1024


def _params(*semantics):
    return pltpu.CompilerParams(dimension_semantics=semantics, vmem_limit_bytes=VMEM_LIMIT_BYTES)


def _dot_nt(a, b):
    return lax.dot_general(a, b, (((1,), (1,)), ((), ())), preferred_element_type=F32)


NORM_TM = 512


def _rmsnorm_kernel(x_ref, g_ref, o_ref):
    x = x_ref[...]
    y = x * lax.rsqrt(jnp.mean(x * x, axis=-1, keepdims=True) + NORM_EPS) * g_ref[...]
    o_ref[...] = y.astype(o_ref.dtype)


def _rmsnorm(x2d, g):
    m, d = x2d.shape
    return pl.pallas_call(
        _rmsnorm_kernel,
        out_shape=jax.ShapeDtypeStruct((m, d), BF16),
        grid=(m // NORM_TM,),
        in_specs=[pl.BlockSpec((NORM_TM, d), lambda i: (i, 0)), pl.BlockSpec((1, d), lambda i: (0, 0))],
        out_specs=pl.BlockSpec((NORM_TM, d), lambda i: (i, 0)),
        compiler_params=_params("parallel"),
        name="rmsnorm",
    )(x2d, g.reshape(1, d))


INPROJ_TM = 2048


def _inproj_kernel(h_ref, w_ref, cos_ref, sin_ref, of_ref, ob_ref, acc):
    n = pl.program_id(1)
    acc[...] = jnp.dot(h_ref[...], w_ref[...].astype(BF16), preferred_element_type=F32)

    def emit(o_ref, rotary):
        if not rotary:
            o_ref[...] = acc[...].astype(o_ref.dtype)
            return
        cos = cos_ref[...]
        sin = sin_ref[...]
        for hh in range(GROUP_HEADS):
            sl = slice(hh * HEAD_DIM, (hh + 1) * HEAD_DIM)
            a = acc[:, sl]
            o_ref[:, sl] = (a * cos + pltpu.roll(a, HEAD_DIM // 2, 1) * sin).astype(o_ref.dtype)

    is_f32 = n < N_F32_SLICES
    is_rot = functools.reduce(jnp.logical_or, [n == s for s in ROTARY_SLICES])
    for f32_out in (True, False):
        for rotary in (True, False):
            cond = jnp.logical_and(is_f32 == f32_out, is_rot == rotary)
            pl.when(cond)(functools.partial(emit, of_ref if f32_out else ob_ref, rotary))


def _inproj(h2d, w_stack, layer, cos_t, sin_t, seq):
    m, d = h2d.shape
    gw = GROUP_WIDTH
    n_slices = w_stack.shape[2] // gw
    tm = INPROJ_TM
    seq_tiles = seq // tm
    return pl.pallas_call(
        _inproj_kernel,
        out_shape=(jax.ShapeDtypeStruct((m, N_F32_SLICES * gw), F32),
                   jax.ShapeDtypeStruct((m, (n_slices - N_F32_SLICES) * gw), BF16)),
        grid=(m // tm, n_slices),
        in_specs=[
            pl.BlockSpec((tm, d), lambda i, n: (i, 0)),
            pl.BlockSpec((None, d, gw), lambda i, n: (layer, 0, n)),
            pl.BlockSpec((tm, HEAD_DIM), lambda i, n: (i % seq_tiles, 0)),
            pl.BlockSpec((tm, HEAD_DIM), lambda i, n: (i % seq_tiles, 0)),
        ],
        out_specs=(pl.BlockSpec((tm, gw), lambda i, n: (i, jnp.minimum(n, N_F32_SLICES - 1))),
                   pl.BlockSpec((tm, gw), lambda i, n: (i, jnp.maximum(n - N_F32_SLICES, 0)))),
        scratch_shapes=[pltpu.VMEM((tm, gw), F32)],
        compiler_params=_params("parallel", "arbitrary"),
        name="inproj",
    )(h2d, w_stack, cos_t, sin_t)


DIL_UNROLL = 4


def _dilated_kernel(q_ref, kp_ref, ko_ref, vp_ref, vo_ref, o_ref,
                    kn, vn, o1, o2, o3, l1, l2, l3):
    sup = DIL_SUPER
    band = DIL_BAND
    has_prev = pl.program_id(1) > 0

    kn[0:sup, :] = kp_ref[...]
    kn[sup:2 * sup, :] = ko_ref[...]
    vn[0:sup, :] = vp_ref[...]
    vn[sup:2 * sup, :] = vo_ref[...]

    qi = lax.broadcasted_iota(jnp.int32, (band, 2 * band), 0)
    kj = lax.broadcasted_iota(jnp.int32, (band, 2 * band), 1)
    in_band = (kj >= qi) & (kj <= qi + band)
    bias_any = jnp.where(in_band, 0.0, NEG_INF).astype(F32)
    bias_start = jnp.where(in_band & (kj >= band), 0.0, NEG_INF).astype(F32)

    ones = jnp.ones((2 * band, HEAD_DIM), BF16)

    def attend(qb, kb, vb, at_start):
        bias = jnp.where(at_start, bias_start, bias_any)
        t = _dot_nt(qb.astype(BF16), kb.astype(BF16)) * EXP2_SCALE + bias
        m = jnp.max(t, axis=-1, keepdims=True)
        p = jnp.exp2(t - m)
        va = jnp.concatenate([vb.astype(BF16), ones], axis=1)
        r = jnp.dot(p.astype(BF16), va, preferred_element_type=F32)
        l = r[:, HEAD_DIM:]
        return r[:, :HEAD_DIM] / l, m + jnp.log(l) * LOG2E

    def pattern(stride, o_out, l_out):
        span = band * stride

        def one_block(t):
            grp = t // stride
            res = t % stride
            q0 = grp * span + res
            k0 = sup + q0 - span
            if stride == 1:
                qsl = pl.ds(pl.multiple_of(q0, band), band)
                ksl = pl.ds(pl.multiple_of(k0, band), 2 * band)
            else:
                qsl = pl.ds(q0, band, stride=stride)
                ksl = pl.ds(k0, 2 * band, stride=stride)
            at_start = jnp.logical_and(grp == 0, jnp.logical_not(has_prev))
            o, lse = attend(q_ref[qsl, :], kn[ksl, :], vn[ksl, :], at_start)
            o_out[qsl, :] = o
            l_out[qsl, :] = lse

        def body(g, c):
            for u in range(DIL_UNROLL):
                one_block(g * DIL_UNROLL + u)
            return c
        lax.fori_loop(0, sup // band // DIL_UNROLL, body, 0)

    pattern(DIL_STRIDES[0], o1, l1)
    pattern(DIL_STRIDES[1], o2, l2)
    pattern(DIL_STRIDES[2], o3, l3)

    rows_per_step = 256

    def merge(i, c):
        rows = pl.ds(pl.multiple_of(i * rows_per_step, rows_per_step), rows_per_step)
        a1, a2, a3 = l1[rows, :], l2[rows, :], l3[rows, :]
        mx = jnp.maximum(jnp.maximum(a1, a2), a3)
        w1, w2, w3 = jnp.exp2(a1 - mx), jnp.exp2(a2 - mx), jnp.exp2(a3 - mx)
        num = w1 * o1[rows, :] + w2 * o2[rows, :] + w3 * o3[rows, :]
        o_ref[rows, :] = (num / (w1 + w2 + w3)).astype(o_ref.dtype)
        return c
    lax.fori_loop(0, sup // rows_per_step, merge, 0)


def _dilated(proj, batch, seq):
    sup = DIL_SUPER
    blk = (None, sup, HEAD_DIM)
    prev = lambda sb: jnp.maximum(sb - 1, 0)
    scr = pltpu.VMEM((sup, HEAD_DIM), F32)
    win = pltpu.VMEM((2 * sup, HEAD_DIM), F32)
    return pl.pallas_call(
        _dilated_kernel,
        out_shape=jax.ShapeDtypeStruct((batch, seq, GROUP_WIDTH), BF16),
        grid=(batch, seq // sup, GROUP_HEADS),
        in_specs=[
            pl.BlockSpec(blk, lambda b, sb, h: (b, sb, SL_A_Q * GROUP_HEADS + h)),
            pl.BlockSpec(blk, lambda b, sb, h: (b, prev(sb), SL_A_K * GROUP_HEADS + h)),
            pl.BlockSpec(blk, lambda b, sb, h: (b, sb, SL_A_K * GROUP_HEADS + h)),
            pl.BlockSpec(blk, lambda b, sb, h: (b, prev(sb), SL_A_V * GROUP_HEADS + h)),
            pl.BlockSpec(blk, lambda b, sb, h: (b, sb, SL_A_V * GROUP_HEADS + h)),
        ],
        out_specs=pl.BlockSpec(blk, lambda b, sb, h: (b, sb, h)),
        scratch_shapes=[win, win, scr, scr, scr, scr, scr, scr],
        compiler_params=_params("parallel", "parallel", "parallel"),
        name="dilated",
    )(proj, proj, proj, proj, proj)


MOBA_TQ = 2 * MOBA_BLOCK
MOBA_BLOCK_SHIFT = 8
assert 1 << MOBA_BLOCK_SHIFT == MOBA_BLOCK


def _moba_kernel(q_ref, k_ref, v_ref, o_ref, kaug, vaug, kmean, qaug, m_scr, acc_scr, s_a, s_b):
    blk = MOBA_BLOCK
    tq = MOBA_TQ
    hd = HEAD_DIM
    seq = k_ref.shape[0]
    n_blocks = seq // blk
    n_tiles = seq // tq

    kmean[...] = jnp.zeros_like(kmean)
    lane = lax.broadcasted_iota(jnp.int32, (blk, hd), 1)
    ones = jnp.ones((blk, hd), BF16)
    for n in range(n_blocks):
        rows = slice(n * blk, (n + 1) * blk)
        kb = k_ref[rows, :]
        kmean[n:n + 1, :] = jnp.mean(kb.astype(F32), axis=0, keepdims=True)
        kaug[rows, 0:hd] = kb
        kaug[rows, hd:2 * hd] = jnp.where(lane == n, 1.0, 0.0).astype(BF16)
        vaug[rows, 0:hd] = v_ref[rows, :]
        vaug[rows, hd:2 * hd] = ones

    def tile_rows(t):
        return pl.ds(pl.multiple_of(t * tq, tq), tq)

    bidx = lax.broadcasted_iota(jnp.int32, (tq, hd), 1)
    bidx_f = bidx.astype(F32)
    row = lax.broadcasted_iota(jnp.int32, (tq, hd), 0)

    def route(t, carry):
        q = q_ref[tile_rows(t), :]
        gate = lax.dot_general(q.astype(F32), kmean[...], (((1,), (1,)), ((), ())),
                               precision=lax.Precision.HIGHEST, preferred_element_type=F32)
        qblk = t * (tq // blk) + lax.shift_right_logical(row, MOBA_BLOCK_SHIFT)
        g = jnp.where(bidx < qblk, gate, NEG_INF)
        sel_bias = jnp.full((tq, hd), NEG_INF, F32)
        for k in range(MOBA_TOPK):
            mx = jnp.max(g, axis=-1, keepdims=True)
            first = jnp.min(jnp.where(g == mx, bidx_f, float(hd)), axis=-1, keepdims=True)
            hit = bidx_f == first
            sel_bias = jnp.where(hit, jnp.where(k < qblk, 0.0, NEG_INF), sel_bias)
            g = jnp.where(hit, -jnp.inf, g)
        sel_bias = jnp.where(bidx == qblk, 0.0, sel_bias)
        qaug[tile_rows(t), 0:hd] = q
        qaug[tile_rows(t), hd:2 * hd] = sel_bias.astype(BF16)
        return carry
    lax.fori_loop(0, n_tiles, route, 0)

    def scores(t, c):
        return _dot_nt(qaug[tile_rows(t), :], kaug[tile_rows(c), :])

    def advance(t, c):
        wrap = c == t
        return jnp.where(wrap, t + 1, t), jnp.where(wrap, 0, c + 1)

    r_col = lax.broadcasted_iota(jnp.int32, (tq, 1), 0)
    c_i = lax.broadcasted_iota(jnp.int32, (tq, tq), 1)

    def step(s_ref, t, c):
        row_lim = jnp.where(c == t, r_col, tq)
        s = jnp.where(c_i <= row_lim, s_ref[...], NEG_INF)
        m_old = jnp.where(c == 0, 2 * NEG_INF, m_scr[...])
        m_new = jnp.maximum(m_old, jnp.max(s, axis=-1, keepdims=True))
        alpha = jnp.exp2((m_old - m_new) * EXP2_SCALE)
        p = jnp.exp2((s - m_new) * EXP2_SCALE)
        m_scr[...] = m_new
        acc = alpha * acc_scr[...] + jnp.dot(
            p.astype(BF16), vaug[tile_rows(c), :], preferred_element_type=F32)
        acc_scr[...] = acc
        o_ref[tile_rows(t), :] = (acc[:, 0:hd] / acc[:, hd:2 * hd]).astype(o_ref.dtype)

    acc_scr[...] = jnp.zeros_like(acc_scr)
    s_a[...] = scores(0, 0)

    def pair(j, carry):
        t0, c0 = carry
        t1, c1 = advance(t0, c0)
        t2, c2 = advance(t1, c1)
        past_end = t2 == n_tiles
        s_b[...] = scores(t1, c1)
        step(s_a, t0, c0)
        s_a[...] = scores(jnp.where(past_end, 0, t2), jnp.where(past_end, 0, c2))
        step(s_b, t1, c1)
        return t2, c2
    n_steps = n_tiles * (n_tiles + 1) // 2
    assert n_steps % 2 == 0
    lax.fori_loop(0, n_steps // 2, pair, (jnp.int32(0), jnp.int32(0)))


def _moba(proj, batch, seq):
    tq = MOBA_TQ
    hd = HEAD_DIM
    full = lambda sl: pl.BlockSpec((None, seq, hd), lambda b, h: (b, 0, _bf16_slice(sl) * GROUP_HEADS + h))
    return pl.pallas_call(
        _moba_kernel,
        out_shape=jax.ShapeDtypeStruct((batch, seq, GROUP_WIDTH), BF16),
        grid=(batch, GROUP_HEADS),
        in_specs=[full(SL_C_Q), full(SL_C_K), full(SL_C_V)],
        out_specs=pl.BlockSpec((None, seq, hd), lambda b, h: (b, 0, h)),
        scratch_shapes=[
            pltpu.VMEM((seq, 2 * hd), BF16),
            pltpu.VMEM((seq, 2 * hd), BF16),
            pltpu.VMEM((hd, hd), F32),
            pltpu.VMEM((seq, 2 * hd), BF16),
            pltpu.VMEM((tq, 1), F32),
            pltpu.VMEM((tq, 2 * hd), F32),
            pltpu.VMEM((tq, tq), F32),
            pltpu.VMEM((tq, tq), F32),
        ],
        compiler_params=_params("parallel", "parallel"),
        name="moba",
    )(proj, proj, proj)


def _memkv_kernel(mem_ref, g_ref, w_ref, o_ref):
    x = mem_ref[...]
    y = x * lax.rsqrt(jnp.mean(x * x, axis=-1, keepdims=True) + NORM_EPS) * g_ref[...]
    o_ref[...] = jnp.dot(y.astype(BF16), w_ref[...], preferred_element_type=F32).astype(o_ref.dtype)


def _memkv(mem, mem_norm_g, w_bf16):
    batch, n_mem, d = mem.shape
    n_cols = w_bf16.shape[1]
    return pl.pallas_call(
        _memkv_kernel,
        out_shape=jax.ShapeDtypeStruct((batch, n_mem, n_cols), BF16),
        grid=(batch,),
        in_specs=[
            pl.BlockSpec((None, n_mem, d), lambda b: (b, 0, 0)),
            pl.BlockSpec((1, d), lambda b: (0, 0)),
            pl.BlockSpec((d, n_cols), lambda b: (0, 0)),
        ],
        out_specs=pl.BlockSpec((None, n_mem, n_cols), lambda b: (b, 0, 0)),
        compiler_params=_params("parallel"),
        name="memkv",
    )(mem, mem_norm_g.reshape(1, d), w_bf16)


LOCAL_TM = 512


def _local_kernel(bu_ref, bv_ref, db_ref, dc_ref, dh_ref, hc_ref, hh_ref, xq_ref,
                  ws_ref, bs_ref, lng_ref, lnb_ref, cw_ref, kv_ref, o_ref, zbuf):
    tm = LOCAL_TM
    gw = GROUP_WIDTH

    u = jax.nn.gelu(bu_ref[...].astype(F32))
    v = jax.nn.gelu(bv_ref[...].astype(F32))
    mu = jnp.mean(v, axis=-1, keepdims=True)
    vc = v - mu
    var = jnp.mean(vc * vc, axis=-1, keepdims=True)
    vnorm = (vc * lax.rsqrt(var + NORM_EPS) * lng_ref[...] + lnb_ref[...]).astype(BF16)
    r = lax.broadcasted_iota(jnp.int32, (SGU_CHUNK, SGU_CHUNK), 0)
    c = lax.broadcasted_iota(jnp.int32, (SGU_CHUNK, SGU_CHUNK), 1)
    for hd in range(GROUP_HEADS):
        cols = slice(hd * HEAD_DIM, (hd + 1) * HEAD_DIM)
        w_causal = jnp.where(c <= r, ws_ref[hd], 0.0).astype(BF16)
        bias = bs_ref[:, hd:hd + 1]
        for ch in range(tm // SGU_CHUNK):
            rows = slice(ch * SGU_CHUNK, (ch + 1) * SGU_CHUNK)
            mixed = jnp.dot(w_causal, vnorm[rows, cols], preferred_element_type=F32) + bias
            o_ref[rows, cols] = (u[rows, cols] * mixed).astype(o_ref.dtype)

    first_tile = pl.program_id(1) == 0
    zbuf[0:CONV_HALO, :] = jnp.where(first_tile, 0.0, hc_ref[...].astype(F32) * hh_ref[...].astype(F32))
    zbuf[CONV_HALO:CONV_HALO + tm, :] = dc_ref[...].astype(F32) * dh_ref[...].astype(F32)
    y = cw_ref[CONV_WIDTH - 1:CONV_WIDTH, :] * zbuf[CONV_HALO:CONV_HALO + tm, :]
    for tap in range(CONV_WIDTH - 1):
        back = CONV_WIDTH - 1 - tap
        y = y + cw_ref[tap:tap + 1, :] * zbuf[CONV_HALO - back:CONV_HALO - back + tm, :]
    o_ref[:, gw:2 * gw] = (db_ref[...].astype(F32) * y).astype(o_ref.dtype)

    for hd in range(GROUP_HEADS):
        cols = slice(hd * HEAD_DIM, (hd + 1) * HEAD_DIM)
        kh = kv_ref[:, hd * HEAD_DIM:(hd + 1) * HEAD_DIM]
        vh = kv_ref[:, gw + hd * HEAD_DIM:gw + (hd + 1) * HEAD_DIM]
        s = _dot_nt(xq_ref[:, cols], kh) * ATTN_SCALE
        m = jnp.max(s, axis=-1, keepdims=True)
        p = jnp.exp(s - m)
        l = jnp.sum(p, axis=-1, keepdims=True)
        o = jnp.dot(p.astype(BF16), vh, preferred_element_type=F32) / l
        o_ref[:, 2 * gw + hd * HEAD_DIM:2 * gw + (hd + 1) * HEAD_DIM] = o.astype(o_ref.dtype)


def _local(proj, memkv, sgu_w, sgu_b_t, ln_g, ln_b, conv_w, batch, seq):
    tm = LOCAL_TM
    gw = GROUP_WIDTH
    halo_per_tile = tm // CONV_HALO
    tile = lambda sl: pl.BlockSpec((None, tm, gw), lambda b, i: (b, i, _bf16_slice(sl)))
    halo = lambda sl: pl.BlockSpec(
        (None, CONV_HALO, gw), lambda b, i: (b, jnp.maximum(i * halo_per_tile - 1, 0), _bf16_slice(sl)))
    whole = lambda a: pl.BlockSpec(a.shape, lambda b, i: (0,) * a.ndim)
    n_mem = memkv.shape[1]
    return pl.pallas_call(
        _local_kernel,
        out_shape=jax.ShapeDtypeStruct((batch, seq, 3 * gw), BF16),
        grid=(batch, seq // tm),
        in_specs=[
            tile(SL_B_U), tile(SL_B_V), tile(SL_D_B), tile(SL_D_C), tile(SL_D_H),
            halo(SL_D_C), halo(SL_D_H), tile(SL_X_Q),
            whole(sgu_w), whole(sgu_b_t), whole(ln_g), whole(ln_b), whole(conv_w),
            pl.BlockSpec((None, n_mem, 2 * gw), lambda b, i: (b, 0, 0)),
        ],
        out_specs=pl.BlockSpec((None, tm, 3 * gw), lambda b, i: (b, i, 0)),
        scratch_shapes=[pltpu.VMEM((CONV_HALO + tm, gw), F32)],
        compiler_params=_params("parallel", "parallel"),
        name="local",
    )(proj, proj, proj, proj, proj, proj, proj, proj,
      sgu_w, sgu_b_t, ln_g, ln_b, conv_w, memkv)


OUTPROJ_TM = 256


def _outproj_kernel(ya_ref, yl_ref, yc_ref, ga_ref, gb_ref, gc_ref, gd_ref, gx_ref,
                    ng_ref, w_ref, x_ref, pg_ref, *rest, final):
    if final:
        o_ref, y_scr = rest
    else:
        o_ref, h_ref, y_scr = rest
    gw = GROUP_WIDTH
    branches = (
        (ya_ref[...], ga_ref),
        (yl_ref[:, 0:gw], gb_ref),
        (yc_ref[...], gc_ref),
        (yl_ref[:, gw:2 * gw], gd_ref),
        (yl_ref[:, 2 * gw:3 * gw], gx_ref),
    )
    for grp, (o, gate_ref) in enumerate(branches):
        cols = slice(grp * gw, (grp + 1) * gw)
        y = o.astype(F32) * jax.nn.silu(gate_ref[...].astype(F32))
        y = y * lax.rsqrt(jnp.mean(y * y, axis=-1, keepdims=True) + NORM_EPS)
        y_scr[:, cols] = (y * ng_ref[:, cols]).astype(BF16)
    acc = x_ref[...] + jnp.dot(y_scr[...], w_ref[...], preferred_element_type=F32)
    normed = acc * lax.rsqrt(jnp.mean(acc * acc, axis=-1, keepdims=True) + NORM_EPS) * pg_ref[...]
    if final:
        o_ref[...] = normed
    else:
        o_ref[...] = acc
        h_ref[...] = normed.astype(h_ref.dtype)


def _outproj(ya, yl, yc, proj_b, out_norm_g, w_bf16, x2d, post_g, final):
    m, d = x2d.shape
    tm = OUTPROJ_TM
    gw = GROUP_WIDTH
    gate = lambda sl: pl.BlockSpec((tm, gw), lambda i: (i, _bf16_slice(sl)))
    row_tile = pl.BlockSpec((tm, d), lambda i: (i, 0))
    x_out = jax.ShapeDtypeStruct((m, d), F32)
    return pl.pallas_call(
        functools.partial(_outproj_kernel, final=final),
        out_shape=x_out if final else (x_out, jax.ShapeDtypeStruct((m, d), BF16)),
        grid=(m // tm,),
        in_specs=[
            pl.BlockSpec((tm, gw), lambda i: (i, 0)),
            pl.BlockSpec((tm, 3 * gw), lambda i: (i, 0)),
            pl.BlockSpec((tm, gw), lambda i: (i, 0)),
            gate(SL_A_G), gate(SL_B_G), gate(SL_C_G), gate(SL_D_G), gate(SL_X_G),
            pl.BlockSpec((1, MIX_WIDTH), lambda i: (0, 0)),
            pl.BlockSpec((MIX_WIDTH, d), lambda i: (0, 0)),
            pl.BlockSpec((tm, d), lambda i: (i, 0)),
            pl.BlockSpec((1, d), lambda i: (0, 0)),
        ],
        out_specs=row_tile if final else (row_tile, row_tile),
        scratch_shapes=[pltpu.VMEM((tm, MIX_WIDTH), BF16)],
        compiler_params=_params("parallel"),
        name="outproj_final" if final else "outproj",
    )(ya, yl, yc, proj_b, proj_b, proj_b, proj_b, proj_b,
      out_norm_g.reshape(1, MIX_WIDTH), w_bf16, x2d, post_g.reshape(1, d))


def _rotary_tables(seq):
    half = HEAD_DIM // 2
    inv_freq = ROPE_THETA ** (-jnp.arange(half, dtype=F32) / half)
    ang = jnp.arange(seq, dtype=jnp.int32).astype(F32)[:, None] * inv_freq[None, :]
    cos, sin = jnp.cos(ang), jnp.sin(ang)
    return jnp.concatenate([cos, cos], axis=-1), jnp.concatenate([-sin, sin], axis=-1)


def kernel(x, mem, norm_g, w_in, sgu_w, sgu_b, sgu_ln_g, sgu_ln_b, conv_w, mem_norm_g, w_mem_kv,
           out_norm_g, w_out, final_norm_g):
    batch, seq, d = x.shape
    depth = w_in.shape[0]
    cos_t, sin_t = _rotary_tables(seq)
    x2d = x.reshape(batch * seq, d)
    h2d = _rmsnorm(x2d, norm_g[0])
    flat = lambda t: t.reshape(batch * seq, t.shape[-1])
    for i in range(depth):
        final = i == depth - 1
        proj_f, proj_b = _inproj(h2d, w_in, i, cos_t, sin_t, seq)
        proj_a = proj_f.reshape(batch, seq, proj_f.shape[-1])
        proj_r = proj_b.reshape(batch, seq, proj_b.shape[-1])
        memkv = _memkv(mem, mem_norm_g[i], w_mem_kv[i].astype(BF16))
        ya = _dilated(proj_a, batch, seq)
        yc = _moba(proj_r, batch, seq)
        yl = _local(proj_r, memkv, sgu_w[i], sgu_b[i].T, sgu_ln_g[i].reshape(1, GROUP_WIDTH),
                    sgu_ln_b[i].reshape(1, GROUP_WIDTH), conv_w[i], batch, seq)
        out = _outproj(flat(ya), flat(yl), flat(yc), proj_b, out_norm_g[i], w_out[i].astype(BF16),
                       x2d, final_norm_g if final else norm_g[i + 1], final)
        if final:
            x2d = out
        else:
            x2d, h2d = out
    return x2d.reshape(batch, seq, d)
```

```python
import functools

import jax
import jax.numpy as jnp
from jax import lax
from jax.experimental import pallas as pl
from jax.experimental.pallas import tpu as pltpu

F32 = jnp.float32
BF16 = jnp.bfloat16

HEAD_DIM = 128
GROUP_HEADS = 4
GROUP_WIDTH = GROUP_HEADS * HEAD_DIM
N_GROUPS = 5
MIX_WIDTH = N_GROUPS * GROUP_WIDTH
IN_SLICES = 17
ROPE_THETA = 10000.0
NORM_EPS = 1e-6
NEG_INF = -1e30
ATTN_SCALE = HEAD_DIM ** -0.5
LOG2E = 1.4426950408889634
EXP2_SCALE = ATTN_SCALE * LOG2E

DIL_BAND = 128
DIL_STRIDES = (1, 4, 16)
DIL_SUPER = DIL_BAND * DIL_STRIDES[-1]
SGU_CHUNK = 128
MOBA_BLOCK = 256
MOBA_TOPK = 3
CONV_WIDTH = 3
CONV_HALO = 16

SL_A_Q, SL_A_K, SL_A_V, SL_A_G = 0, 1, 2, 3
SL_B_U, SL_B_V, SL_B_G = 4, 5, 6
SL_C_Q, SL_C_K, SL_C_V, SL_C_G = 7, 8, 9, 10
SL_D_B, SL_D_C, SL_D_H, SL_D_G = 11, 12, 13, 14
SL_X_Q, SL_X_G = 15, 16
ROTARY_SLICES = (SL_A_Q, SL_A_K, SL_C_Q, SL_C_K)
N_F32_SLICES = 3


def _bf16_slice(sl):
    assert sl >= N_F32_SLICES
    return sl - N_F32_SLICES


VMEM_LIMIT_BYTES = 56 * 1024 * 1024


def _params(*semantics):
    return pltpu.CompilerParams(dimension_semantics=semantics, vmem_limit_bytes=VMEM_LIMIT_BYTES)


def _dot_nt(a, b):
    return lax.dot_general(a, b, (((1,), (1,)), ((), ())), preferred_element_type=F32)


NORM_TM = 512


def _rmsnorm_kernel(x_ref, g_ref, o_ref):
    x = x_ref[...]
    y = x * lax.rsqrt(jnp.mean(x * x, axis=-1, keepdims=True) + NORM_EPS) * g_ref[...]
    o_ref[...] = y.astype(o_ref.dtype)


def _rmsnorm(x2d, g):
    m, d = x2d.shape
    return pl.pallas_call(
        _rmsnorm_kernel,
        out_shape=jax.ShapeDtypeStruct((m, d), BF16),
        grid=(m // NORM_TM,),
        in_specs=[pl.BlockSpec((NORM_TM, d), lambda i: (i, 0)), pl.BlockSpec((1, d), lambda i: (0, 0))],
        out_specs=pl.BlockSpec((NORM_TM, d), lambda i: (i, 0)),
        compiler_params=_params("parallel"),
        name="rmsnorm",
    )(x2d, g.reshape(1, d))


INPROJ_TM = 2048


def _inproj_kernel(h_ref, w_ref, cos_ref, sin_ref, of_ref, ob_ref, acc):
    n = pl.program_id(1)
    acc[...] = jnp.dot(h_ref[...], w_ref[...].astype(BF16), preferred_element_type=F32)

    def emit(o_ref, rotary):
        if not rotary:
            o_ref[...] = acc[...].astype(o_ref.dtype)
            return
        cos = cos_ref[...]
        sin = sin_ref[...]
        for hh in range(GROUP_HEADS):
            sl = slice(hh * HEAD_DIM, (hh + 1) * HEAD_DIM)
            a = acc[:, sl]
            o_ref[:, sl] = (a * cos + pltpu.roll(a, HEAD_DIM // 2, 1) * sin).astype(o_ref.dtype)

    is_f32 = n < N_F32_SLICES
    is_rot = functools.reduce(jnp.logical_or, [n == s for s in ROTARY_SLICES])
    for f32_out in (True, False):
        for rotary in (True, False):
            cond = jnp.logical_and(is_f32 == f32_out, is_rot == rotary)
            pl.when(cond)(functools.partial(emit, of_ref if f32_out else ob_ref, rotary))


def _inproj(h2d, w_stack, layer, cos_t, sin_t, seq):
    m, d = h2d.shape
    gw = GROUP_WIDTH
    n_slices = w_stack.shape[2] // gw
    tm = INPROJ_TM
    seq_tiles = seq // tm
    return pl.pallas_call(
        _inproj_kernel,
        out_shape=(jax.ShapeDtypeStruct((m, N_F32_SLICES * gw), F32),
                   jax.ShapeDtypeStruct((m, (n_slices - N_F32_SLICES) * gw), BF16)),
        grid=(m // tm, n_slices),
        in_specs=[
            pl.BlockSpec((tm, d), lambda i, n: (i, 0)),
            pl.BlockSpec((None, d, gw), lambda i, n: (layer, 0, n)),
            pl.BlockSpec((tm, HEAD_DIM), lambda i, n: (i % seq_tiles, 0)),
            pl.BlockSpec((tm, HEAD_DIM), lambda i, n: (i % seq_tiles, 0)),
        ],
        out_specs=(pl.BlockSpec((tm, gw), lambda i, n: (i, jnp.minimum(n, N_F32_SLICES - 1))),
                   pl.BlockSpec((tm, gw), lambda i, n: (i, jnp.maximum(n - N_F32_SLICES, 0)))),
        scratch_shapes=[pltpu.VMEM((tm, gw), F32)],
        compiler_params=_params("parallel", "arbitrary"),
        name="inproj",
    )(h2d, w_stack, cos_t, sin_t)


DIL_UNROLL = 4


def _dilated_kernel(q_ref, kp_ref, ko_ref, vp_ref, vo_ref, o_ref,
                    kn, vn, o1, o2, o3, l1, l2, l3):
    sup = DIL_SUPER
    band = DIL_BAND
    has_prev = pl.program_id(1) > 0

    kn[0:sup, :] = kp_ref[...]
    kn[sup:2 * sup, :] = ko_ref[...]
    vn[0:sup, :] = vp_ref[...]
    vn[sup:2 * sup, :] = vo_ref[...]

    qi = lax.broadcasted_iota(jnp.int32, (band, 2 * band), 0)
    kj = lax.broadcasted_iota(jnp.int32, (band, 2 * band), 1)
    in_band = (kj >= qi) & (kj <= qi + band)
    bias_any = jnp.where(in_band, 0.0, NEG_INF).astype(F32)
    bias_start = jnp.where(in_band & (kj >= band), 0.0, NEG_INF).astype(F32)

    ones = jnp.ones((2 * band, HEAD_DIM), BF16)

    def attend(qb, kb, vb, at_start):
        bias = jnp.where(at_start, bias_start, bias_any)
        t = _dot_nt(qb.astype(BF16), kb.astype(BF16)) * EXP2_SCALE + bias
        m = jnp.max(t, axis=-1, keepdims=True)
        p = jnp.exp2(t - m)
        va = jnp.concatenate([vb.astype(BF16), ones], axis=1)
        r = jnp.dot(p.astype(BF16), va, preferred_element_type=F32)
        l = r[:, HEAD_DIM:]
        return r[:, :HEAD_DIM] / l, m + jnp.log(l) * LOG2E

    def pattern(stride, o_out, l_out):
        span = band * stride

        def one_block(t):
            grp = t // stride
            res = t % stride
            q0 = grp * span + res
            k0 = sup + q0 - span
            if stride == 1:
                qsl = pl.ds(pl.multiple_of(q0, band), band)
                ksl = pl.ds(pl.multiple_of(k0, band), 2 * band)
            else:
                qsl = pl.ds(q0, band, stride=stride)
                ksl = pl.ds(k0, 2 * band, stride=stride)
            at_start = jnp.logical_and(grp == 0, jnp.logical_not(has_prev))
            o, lse = attend(q_ref[qsl, :], kn[ksl, :], vn[ksl, :], at_start)
            o_out[qsl, :] = o
            l_out[qsl, :] = lse

        def body(g, c):
            for u in range(DIL_UNROLL):
                one_block(g * DIL_UNROLL + u)
            return c
        lax.fori_loop(0, sup // band // DIL_UNROLL, body, 0)

    pattern(DIL_STRIDES[0], o1, l1)
    pattern(DIL_STRIDES[1], o2, l2)
    pattern(DIL_STRIDES[2], o3, l3)

    rows_per_step = 256

    def merge(i, c):
        rows = pl.ds(pl.multiple_of(i * rows_per_step, rows_per_step), rows_per_step)
        a1, a2, a3 = l1[rows, :], l2[rows, :], l3[rows, :]
        mx = jnp.maximum(jnp.maximum(a1, a2), a3)
        w1, w2, w3 = jnp.exp2(a1 - mx), jnp.exp2(a2 - mx), jnp.exp2(a3 - mx)
        num = w1 * o1[rows, :] + w2 * o2[rows, :] + w3 * o3[rows, :]
        o_ref[rows, :] = (num / (w1 + w2 + w3)).astype(o_ref.dtype)
        return c
    lax.fori_loop(0, sup // rows_per_step, merge, 0)


def _dilated(proj, batch, seq):
    sup = DIL_SUPER
    blk = (None, sup, HEAD_DIM)
    prev = lambda sb: jnp.maximum(sb - 1, 0)
    scr = pltpu.VMEM((sup, HEAD_DIM), F32)
    win = pltpu.VMEM((2 * sup, HEAD_DIM), F32)
    return pl.pallas_call(
        _dilated_kernel,
        out_shape=jax.ShapeDtypeStruct((batch, seq, GROUP_WIDTH), BF16),
        grid=(batch, seq // sup, GROUP_HEADS),
        in_specs=[
            pl.BlockSpec(blk, lambda b, sb, h: (b, sb, SL_A_Q * GROUP_HEADS + h)),
            pl.BlockSpec(blk, lambda b, sb, h: (b, prev(sb), SL_A_K * GROUP_HEADS + h)),
            pl.BlockSpec(blk, lambda b, sb, h: (b, sb, SL_A_K * GROUP_HEADS + h)),
            pl.BlockSpec(blk, lambda b, sb, h: (b, prev(sb), SL_A_V * GROUP_HEADS + h)),
            pl.BlockSpec(blk, lambda b, sb, h: (b, sb, SL_A_V * GROUP_HEADS + h)),
        ],
        out_specs=pl.BlockSpec(blk, lambda b, sb, h: (b, sb, h)),
        scratch_shapes=[win, win, scr, scr, scr, scr, scr, scr],
        compiler_params=_params("parallel", "parallel", "parallel"),
        name="dilated",
    )(proj, proj, proj, proj, proj)


MOBA_TQ = 2 * MOBA_BLOCK
MOBA_BLOCK_SHIFT = 8
assert 1 << MOBA_BLOCK_SHIFT == MOBA_BLOCK


def _moba_kernel(q_ref, k_ref, v_ref, o_ref, kaug, vaug, kmean, qaug, m_all, acc_all, s_a, s_b):
    blk = MOBA_BLOCK
    tq = MOBA_TQ
    hd = HEAD_DIM
    seq = k_ref.shape[0]
    n_blocks = seq // blk
    n_tiles = seq // tq

    lane = lax.broadcasted_iota(jnp.int32, (blk, hd), 1)
    ones = jnp.ones((blk, hd), BF16)
    for n in range(n_blocks):
        rows = slice(n * blk, (n + 1) * blk)
        kb = k_ref[rows, :]
        kmean[n:n + 1, :] = jnp.mean(kb.astype(F32), axis=0, keepdims=True)
        kaug[rows, 0:hd] = kb
        kaug[rows, hd:2 * hd] = jnp.where(lane == n, 1.0, 0.0).astype(BF16)
        vaug[rows, 0:hd] = v_ref[rows, :]
        vaug[rows, hd:2 * hd] = ones

    def tile_rows(t):
        return pl.ds(pl.multiple_of(t * tq, tq), tq)

    km = kmean[...]
    km_hi = km.astype(BF16)
    rest = km - km_hi.astype(F32)
    km_mid = rest.astype(BF16)
    km_lo = (rest - km_mid.astype(F32)).astype(BF16)
    parts = _dot_nt(jnp.concatenate([km_hi, km_mid, km_lo], axis=0), q_ref[...])
    gate = parts[0:n_blocks] + parts[n_blocks:2 * n_blocks] + parts[2 * n_blocks:3 * n_blocks]
    bidx = lax.broadcasted_iota(jnp.int32, (n_blocks, seq), 0)
    bidx_f = bidx.astype(F32)
    qblk = lax.shift_right_logical(lax.broadcasted_iota(jnp.int32, (n_blocks, seq), 1), MOBA_BLOCK_SHIFT)
    g = jnp.where(bidx < qblk, gate, NEG_INF)
    sel_bias = jnp.full((n_blocks, seq), NEG_INF, F32)
    for k in range(MOBA_TOPK):
        mx = jnp.max(g, axis=0, keepdims=True)
        first = jnp.min(jnp.where(g == mx, bidx_f, float(n_blocks)), axis=0, keepdims=True)
        hit = bidx_f == first
        sel_bias = jnp.where(hit, jnp.where(k < qblk, 0.0, NEG_INF), sel_bias)
        g = jnp.where(hit, -jnp.inf, g)
    sel_bias = jnp.where(bidx == qblk, 0.0, sel_bias)
    sel_bias = jnp.concatenate([sel_bias, jnp.zeros((hd - n_blocks, seq), F32)], axis=0)
    qaug[:, 0:hd] = q_ref[...]
    for t in range(n_tiles):
        rows = slice(t * tq, (t + 1) * tq)
        qaug[rows, hd:2 * hd] = sel_bias[:, rows].T.astype(BF16)

    def scores(t, c):
        return _dot_nt(qaug[tile_rows(t), :], kaug[tile_rows(c), :])

    def two_steps(first, advance, n_steps, step):
        assert n_steps % 2 == 0
        s_a[...] = scores(*first)

        def pair(j, tc0):
            tc1 = advance(*tc0)
            tc2 = advance(*tc1)
            s_b[...] = scores(*tc1)
            step(s_a, *tc0)
            s_a[...] = scores(*(jnp.minimum(v, n_tiles - 1) for v in tc2))
            step(s_b, *tc1)
            return tc2
        lax.fori_loop(0, n_steps // 2, pair, tuple(jnp.int32(v) for v in first))

    r_i = lax.broadcasted_iota(jnp.int32, (tq, tq), 0)
    c_i = lax.broadcasted_iota(jnp.int32, (tq, tq), 1)

    def diagonal_step(s_ref, t, c):
        s = jnp.where(c_i <= r_i, s_ref[...], NEG_INF)
        m = jnp.max(s, axis=-1, keepdims=True)
        p = jnp.exp2((s - m) * EXP2_SCALE)
        m_all[tile_rows(t), :] = m
        acc_all[tile_rows(t), :] = jnp.dot(p.astype(BF16), vaug[tile_rows(c), :], preferred_element_type=F32)

    def past_step(s_ref, t, c):
        s = s_ref[...]
        m_old = m_all[tile_rows(t), :]
        m_new = jnp.maximum(m_old, jnp.max(s, axis=-1, keepdims=True))
        alpha = jnp.exp2((m_old - m_new) * EXP2_SCALE)
        p = jnp.exp2((s - m_new) * EXP2_SCALE)
        m_all[tile_rows(t), :] = m_new
        acc_all[tile_rows(t), :] = alpha * acc_all[tile_rows(t), :] + jnp.dot(
            p.astype(BF16), vaug[tile_rows(c), :], preferred_element_type=F32)

    def next_past(t, c):
        wrap = c + 1 == t
        return jnp.where(wrap, t + 1, t), jnp.where(wrap, 0, c + 1)

    two_steps((0, 0), lambda t, c: (t + 1, c + 1), n_tiles, diagonal_step)
    two_steps((1, 0), next_past, n_tiles * (n_tiles - 1) // 2, past_step)

    def finish(t, carry):
        acc = acc_all[tile_rows(t), :]
        o_ref[tile_rows(t), :] = (acc[:, 0:hd] / acc[:, hd:2 * hd]).astype(o_ref.dtype)
        return carry
    lax.fori_loop(0, n_tiles, finish, 0)


def _moba(proj, batch, seq):
    tq = MOBA_TQ
    hd = HEAD_DIM
    full = lambda sl: pl.BlockSpec((None, seq, hd), lambda b, h: (b, 0, _bf16_slice(sl) * GROUP_HEADS + h))
    return pl.pallas_call(
        _moba_kernel,
        out_shape=jax.ShapeDtypeStruct((batch, seq, GROUP_WIDTH), BF16),
        grid=(batch, GROUP_HEADS),
        in_specs=[full(SL_C_Q), full(SL_C_K), full(SL_C_V)],
        out_specs=pl.BlockSpec((None, seq, hd), lambda b, h: (b, 0, h)),
        scratch_shapes=[
            pltpu.VMEM((seq, 2 * hd), BF16),
            pltpu.VMEM((seq, 2 * hd), BF16),
            pltpu.VMEM((seq // MOBA_BLOCK, hd), F32),
            pltpu.VMEM((seq, 2 * hd), BF16),
            pltpu.VMEM((seq, 1), F32),
            pltpu.VMEM((seq, 2 * hd), F32),
            pltpu.VMEM((tq, tq), F32),
            pltpu.VMEM((tq, tq), F32),
        ],
        compiler_params=_params("parallel", "parallel"),
        name="moba",
    )(proj, proj, proj)


def _memkv_kernel(mem_ref, g_ref, w_ref, o_ref):
    x = mem_ref[...]
    y = x * lax.rsqrt(jnp.mean(x * x, axis=-1, keepdims=True) + NORM_EPS) * g_ref[...]
    o_ref[...] = jnp.dot(y.astype(BF16), w_ref[...], preferred_element_type=F32).astype(o_ref.dtype)


def _memkv(mem, mem_norm_g, w_bf16):
    batch, n_mem, d = mem.shape
    n_cols = w_bf16.shape[1]
    return pl.pallas_call(
        _memkv_kernel,
        out_shape=jax.ShapeDtypeStruct((batch, n_mem, n_cols), BF16),
        grid=(batch,),
        in_specs=[
            pl.BlockSpec((None, n_mem, d), lambda b: (b, 0, 0)),
            pl.BlockSpec((1, d), lambda b: (0, 0)),
            pl.BlockSpec((d, n_cols), lambda b: (0, 0)),
        ],
        out_specs=pl.BlockSpec((None, n_mem, n_cols), lambda b: (b, 0, 0)),
        compiler_params=_params("parallel"),
        name="memkv",
    )(mem, mem_norm_g.reshape(1, d), w_bf16)


LOCAL_TM = 512


def _local_kernel(bu_ref, bv_ref, db_ref, dc_ref, dh_ref, hc_ref, hh_ref, xq_ref,
                  ws_ref, bs_ref, lng_ref, lnb_ref, cw_ref, kv_ref, o_ref, zbuf):
    tm = LOCAL_TM
    gw = GROUP_WIDTH

    u = jax.nn.gelu(bu_ref[...].astype(F32))
    v = jax.nn.gelu(bv_ref[...].astype(F32))
    mu = jnp.mean(v, axis=-1, keepdims=True)
    vc = v - mu
    var = jnp.mean(vc * vc, axis=-1, keepdims=True)
    vnorm = (vc * lax.rsqrt(var + NORM_EPS) * lng_ref[...] + lnb_ref[...]).astype(BF16)
    r = lax.broadcasted_iota(jnp.int32, (SGU_CHUNK, SGU_CHUNK), 0)
    c = lax.broadcasted_iota(jnp.int32, (SGU_CHUNK, SGU_CHUNK), 1)
    for hd in range(GROUP_HEADS):
        cols = slice(hd * HEAD_DIM, (hd + 1) * HEAD_DIM)
        w_causal = jnp.where(c <= r, ws_ref[hd], 0.0).astype(BF16)
        bias = bs_ref[:, hd:hd + 1]
        for ch in range(tm // SGU_CHUNK):
            rows = slice(ch * SGU_CHUNK, (ch + 1) * SGU_CHUNK)
            mixed = jnp.dot(w_causal, vnorm[rows, cols], preferred_element_type=F32) + bias
            o_ref[rows, cols] = (u[rows, cols] * mixed).astype(o_ref.dtype)

    first_tile = pl.program_id(1) == 0
    zbuf[0:CONV_HALO, :] = jnp.where(first_tile, 0.0, hc_ref[...].astype(F32) * hh_ref[...].astype(F32))
    zbuf[CONV_HALO:CONV_HALO + tm, :] = dc_ref[...].astype(F32) * dh_ref[...].astype(F32)
    y = cw_ref[CONV_WIDTH - 1:CONV_WIDTH, :] * zbuf[CONV_HALO:CONV_HALO + tm, :]
    for tap in range(CONV_WIDTH - 1):
        back = CONV_WIDTH - 1 - tap
        y = y + cw_ref[tap:tap + 1, :] * zbuf[CONV_HALO - back:CONV_HALO - back + tm, :]
    o_ref[:, gw:2 * gw] = (db_ref[...].astype(F32) * y).astype(o_ref.dtype)

    for hd in range(GROUP_HEADS):
        cols = slice(hd * HEAD_DIM, (hd + 1) * HEAD_DIM)
        kh = kv_ref[:, hd * HEAD_DIM:(hd + 1) * HEAD_DIM]
        vh = kv_ref[:, gw + hd * HEAD_DIM:gw + (hd + 1) * HEAD_DIM]
        s = _dot_nt(xq_ref[:, cols], kh) * ATTN_SCALE
        m = jnp.max(s, axis=-1, keepdims=True)
        p = jnp.exp(s - m)
        l = jnp.sum(p, axis=-1, keepdims=True)
        o = jnp.dot(p.astype(BF16), vh, preferred_element_type=F32) / l
        o_ref[:, 2 * gw + hd * HEAD_DIM:2 * gw + (hd + 1) * HEAD_DIM] = o.astype(o_ref.dtype)


def _local(proj, memkv, sgu_w, sgu_b_t, ln_g, ln_b, conv_w, batch, seq):
    tm = LOCAL_TM
    gw = GROUP_WIDTH
    halo_per_tile = tm // CONV_HALO
    tile = lambda sl: pl.BlockSpec((None, tm, gw), lambda b, i: (b, i, _bf16_slice(sl)))
    halo = lambda sl: pl.BlockSpec(
        (None, CONV_HALO, gw), lambda b, i: (b, jnp.maximum(i * halo_per_tile - 1, 0), _bf16_slice(sl)))
    whole = lambda a: pl.BlockSpec(a.shape, lambda b, i: (0,) * a.ndim)
    n_mem = memkv.shape[1]
    return pl.pallas_call(
        _local_kernel,
        out_shape=jax.ShapeDtypeStruct((batch, seq, 3 * gw), BF16),
        grid=(batch, seq // tm),
        in_specs=[
            tile(SL_B_U), tile(SL_B_V), tile(SL_D_B), tile(SL_D_C), tile(SL_D_H),
            halo(SL_D_C), halo(SL_D_H), tile(SL_X_Q),
            whole(sgu_w), whole(sgu_b_t), whole(ln_g), whole(ln_b), whole(conv_w),
            pl.BlockSpec((None, n_mem, 2 * gw), lambda b, i: (b, 0, 0)),
        ],
        out_specs=pl.BlockSpec((None, tm, 3 * gw), lambda b, i: (b, i, 0)),
        scratch_shapes=[pltpu.VMEM((CONV_HALO + tm, gw), F32)],
        compiler_params=_params("parallel", "parallel"),
        name="local",
    )(proj, proj, proj, proj, proj, proj, proj, proj,
      sgu_w, sgu_b_t, ln_g, ln_b, conv_w, memkv)


OUTPROJ_TM = 256


def _outproj_kernel(ya_ref, yl_ref, yc_ref, ga_ref, gb_ref, gc_ref, gd_ref, gx_ref,
                    ng_ref, w_ref, x_ref, pg_ref, *outs, final):
    gw = GROUP_WIDTH
    branches = (
        (lambda: ya_ref[...], ga_ref),
        (lambda: yl_ref[:, 0:gw], gb_ref),
        (lambda: yc_ref[...], gc_ref),
        (lambda: yl_ref[:, gw:2 * gw], gd_ref),
        (lambda: yl_ref[:, 2 * gw:3 * gw], gx_ref),
    )
    acc = x_ref[...]
    for grp, (o, gate_ref) in enumerate(branches):
        rows = slice(grp * gw, (grp + 1) * gw)
        y = o().astype(F32) * jax.nn.silu(gate_ref[...].astype(F32))
        y = y * lax.rsqrt(jnp.mean(y * y, axis=-1, keepdims=True) + NORM_EPS)
        y = (y * ng_ref[:, rows]).astype(BF16)
        acc = acc + jnp.dot(y, w_ref[rows, :], preferred_element_type=F32)
    normed = acc * lax.rsqrt(jnp.mean(acc * acc, axis=-1, keepdims=True) + NORM_EPS) * pg_ref[...]
    if final:
        (o_ref,) = outs
        o_ref[...] = normed
    else:
        o_ref, h_ref = outs
        o_ref[...] = acc
        h_ref[...] = normed.astype(h_ref.dtype)


def _outproj(ya, yl, yc, proj_b, out_norm_g, w_bf16, x2d, post_g, final):
    m, d = x2d.shape
    tm = OUTPROJ_TM
    gw = GROUP_WIDTH
    gate = lambda sl: pl.BlockSpec((tm, gw), lambda i: (i, _bf16_slice(sl)))
    row_tile = pl.BlockSpec((tm, d), lambda i: (i, 0))
    x_out = jax.ShapeDtypeStruct((m, d), F32)
    return pl.pallas_call(
        functools.partial(_outproj_kernel, final=final),
        out_shape=x_out if final else (x_out, jax.ShapeDtypeStruct((m, d), BF16)),
        grid=(m // tm,),
        in_specs=[
            pl.BlockSpec((tm, gw), lambda i: (i, 0)),
            pl.BlockSpec((tm, 3 * gw), lambda i: (i, 0)),
            pl.BlockSpec((tm, gw), lambda i: (i, 0)),
            gate(SL_A_G), gate(SL_B_G), gate(SL_C_G), gate(SL_D_G), gate(SL_X_G),
            pl.BlockSpec((1, MIX_WIDTH), lambda i: (0, 0)),
            pl.BlockSpec((MIX_WIDTH, d), lambda i: (0, 0)),
            row_tile,
            pl.BlockSpec((1, d), lambda i: (0, 0)),
        ],
        out_specs=row_tile if final else (row_tile, row_tile),
        compiler_params=_params("parallel"),
        name="outproj_final" if final else "outproj",
    )(ya, yl, yc, proj_b, proj_b, proj_b, proj_b, proj_b,
      out_norm_g.reshape(1, MIX_WIDTH), w_bf16, x2d, post_g.reshape(1, d))


def _rotary_tables(seq):
    half = HEAD_DIM // 2
    inv_freq = ROPE_THETA ** (-jnp.arange(half, dtype=F32) / half)
    ang = jnp.arange(seq, dtype=jnp.int32).astype(F32)[:, None] * inv_freq[None, :]
    cos, sin = jnp.cos(ang), jnp.sin(ang)
    return jnp.concatenate([cos, cos], axis=-1), jnp.concatenate([-sin, sin], axis=-1)


def kernel(x, mem, norm_g, w_in, sgu_w, sgu_b, sgu_ln_g, sgu_ln_b, conv_w, mem_norm_g, w_mem_kv,
           out_norm_g, w_out, final_norm_g):
    batch, seq, d = x.shape
    depth = w_in.shape[0]
    cos_t, sin_t = _rotary_tables(seq)
    x2d = x.reshape(batch * seq, d)
    h2d = _rmsnorm(x2d, norm_g[0])
    flat = lambda t: t.reshape(batch * seq, t.shape[-1])
    for i in range(depth):
        final = i == depth - 1
        proj_f, proj_b = _inproj(h2d, w_in, i, cos_t, sin_t, seq)
        proj_a = proj_f.reshape(batch, seq, proj_f.shape[-1])
        proj_r = proj_b.reshape(batch, seq, proj_b.shape[-1])
        memkv = _memkv(mem, mem_norm_g[i], w_mem_kv[i].astype(BF16))
        ya = _dilated(proj_a, batch, seq)
        yc = _moba(proj_r, batch, seq)
        yl = _local(proj_r, memkv, sgu_w[i], sgu_b[i].T, sgu_ln_g[i].reshape(1, GROUP_WIDTH),
                    sgu_ln_b[i].reshape(1, GROUP_WIDTH), conv_w[i], batch, seq)
        out = _outproj(flat(ya), flat(yl), flat(yc), proj_b, out_norm_g[i], w_out[i].astype(BF16),
                       x2d, final_norm_g if final else norm_g[i + 1], final)
        if final:
            x2d = out
        else:
            x2d, h2d = out
    return x2d.reshape(batch, seq, d)
```

```python
import functools

import jax
import jax.numpy as jnp
from jax import lax
from jax.experimental import pallas as pl
from jax.experimental.pallas import tpu as pltpu

F32 = jnp.float32
BF16 = jnp.bfloat16

HEAD_DIM = 128
GROUP_HEADS = 4
GROUP_WIDTH = GROUP_HEADS * HEAD_DIM
N_GROUPS = 5
MIX_WIDTH = N_GROUPS * GROUP_WIDTH
IN_SLICES = 17
ROPE_THETA = 10000.0
NORM_EPS = 1e-6
NEG_INF = -1e30
ATTN_SCALE = HEAD_DIM ** -0.5
LOG2E = 1.4426950408889634
EXP2_SCALE = ATTN_SCALE * LOG2E

DIL_BAND = 128
DIL_STRIDES = (1, 4, 16)
DIL_SUPER = DIL_BAND * DIL_STRIDES[-1]
SGU_CHUNK = 128
MOBA_BLOCK = 256
MOBA_TOPK = 3
CONV_WIDTH = 3
CONV_HALO = 16

SL_A_Q, SL_A_K, SL_A_V, SL_A_G = 0, 1, 2, 3
SL_B_U, SL_B_V, SL_B_G = 4, 5, 6
SL_C_Q, SL_C_K, SL_C_V, SL_C_G = 7, 8, 9, 10
SL_D_B, SL_D_C, SL_D_H, SL_D_G = 11, 12, 13, 14
SL_X_Q, SL_X_G = 15, 16
ROTARY_SLICES = (SL_A_Q, SL_A_K, SL_C_Q, SL_C_K)
N_F32_SLICES = 3


def _bf16_slice(sl):
    assert sl >= N_F32_SLICES
    return sl - N_F32_SLICES


VMEM_LIMIT_BYTES = 56 * 1024 * 1024


def _params(*semantics):
    return pltpu.CompilerParams(dimension_semantics=semantics, vmem_limit_bytes=VMEM_LIMIT_BYTES)


def _dot_nt(a, b):
    return lax.dot_general(a, b, (((1,), (1,)), ((), ())), preferred_element_type=F32)


NORM_TM = 512


def _rmsnorm_kernel(x_ref, g_ref, o_ref):
    x = x_ref[...]
    y = x * lax.rsqrt(jnp.mean(x * x, axis=-1, keepdims=True) + NORM_EPS) * g_ref[...]
    o_ref[...] = y.astype(o_ref.dtype)


def _rmsnorm(x2d, g):
    m, d = x2d.shape
    return pl.pallas_call(
        _rmsnorm_kernel,
        out_shape=jax.ShapeDtypeStruct((m, d), BF16),
        grid=(m // NORM_TM,),
        in_specs=[pl.BlockSpec((NORM_TM, d), lambda i: (i, 0)), pl.BlockSpec((1, d), lambda i: (0, 0))],
        out_specs=pl.BlockSpec((NORM_TM, d), lambda i: (i, 0)),
        compiler_params=_params("parallel"),
        name="rmsnorm",
    )(x2d, g.reshape(1, d))


INPROJ_TM = 2048


INPROJ_ROW_CHUNKS = 4


def _inproj_kernel(h_ref, w_ref, cos_ref, sin_ref, of_ref, ob_ref):
    n = pl.program_id(1)
    chunk_rows = INPROJ_TM // INPROJ_ROW_CHUNKS

    def project(o_ref, rotary):
        w = w_ref[...].astype(BF16)
        for r in range(INPROJ_ROW_CHUNKS):
            rows = slice(r * chunk_rows, (r + 1) * chunk_rows)
            acc = jnp.dot(h_ref[rows, :], w, preferred_element_type=F32)
            if not rotary:
                o_ref[rows, :] = acc.astype(o_ref.dtype)
                continue
            cos = cos_ref[rows, :]
            sin = sin_ref[rows, :]
            for hh in range(GROUP_HEADS):
                sl = slice(hh * HEAD_DIM, (hh + 1) * HEAD_DIM)
                a = acc[:, sl]
                o_ref[rows, sl] = (a * cos + pltpu.roll(a, HEAD_DIM // 2, 1) * sin).astype(o_ref.dtype)

    is_f32 = n < N_F32_SLICES
    is_rot = functools.reduce(jnp.logical_or, [n == s for s in ROTARY_SLICES])
    for f32_out in (True, False):
        for rotary in (True, False):
            cond = jnp.logical_and(is_f32 == f32_out, is_rot == rotary)
            pl.when(cond)(functools.partial(project, of_ref if f32_out else ob_ref, rotary))


def _inproj(h2d, w_stack, layer, cos_t, sin_t, seq):
    m, d = h2d.shape
    gw = GROUP_WIDTH
    n_slices = w_stack.shape[2] // gw
    tm = INPROJ_TM
    seq_tiles = seq // tm
    return pl.pallas_call(
        _inproj_kernel,
        out_shape=(jax.ShapeDtypeStruct((m, N_F32_SLICES * gw), F32),
                   jax.ShapeDtypeStruct((m, (n_slices - N_F32_SLICES) * gw), BF16)),
        grid=(m // tm, n_slices),
        in_specs=[
            pl.BlockSpec((tm, d), lambda i, n: (i, 0)),
            pl.BlockSpec((None, d, gw), lambda i, n: (layer, 0, n)),
            pl.BlockSpec((tm, HEAD_DIM), lambda i, n: (i % seq_tiles, 0)),
            pl.BlockSpec((tm, HEAD_DIM), lambda i, n: (i % seq_tiles, 0)),
        ],
        out_specs=(pl.BlockSpec((tm, gw), lambda i, n: (i, jnp.minimum(n, N_F32_SLICES - 1))),
                   pl.BlockSpec((tm, gw), lambda i, n: (i, jnp.maximum(n - N_F32_SLICES, 0)))),
        compiler_params=_params("parallel", "arbitrary"),
        name="inproj",
    )(h2d, w_stack, cos_t, sin_t)


DIL_UNROLL = 16


def _dilated_kernel(q_ref, kp_ref, ko_ref, vp_ref, vo_ref, o_ref,
                    kn, vn, o1, o2, o3, l1, l2, l3):
    sup = DIL_SUPER
    band = DIL_BAND
    has_prev = pl.program_id(1) > 0

    kn[0:sup, :] = kp_ref[...]
    kn[sup:2 * sup, :] = ko_ref[...]
    vn[0:sup, :] = vp_ref[...]
    vn[sup:2 * sup, :] = vo_ref[...]

    qi = lax.broadcasted_iota(jnp.int32, (band, 2 * band), 0)
    kj = lax.broadcasted_iota(jnp.int32, (band, 2 * band), 1)
    in_band = (kj >= qi) & (kj <= qi + band)
    bias_any = jnp.where(in_band, 0.0, NEG_INF).astype(F32)
    bias_start = jnp.where(in_band & (kj >= band), 0.0, NEG_INF).astype(F32)

    ones = jnp.ones((2 * band, HEAD_DIM), BF16)

    def attend(qb, kb, vb, at_start):
        bias = jnp.where(at_start, bias_start, bias_any)
        t = _dot_nt(qb.astype(BF16), kb.astype(BF16)) * EXP2_SCALE + bias
        m = jnp.max(t, axis=-1, keepdims=True)
        p = jnp.exp2(t - m)
        va = jnp.concatenate([vb.astype(BF16), ones], axis=1)
        r = jnp.dot(p.astype(BF16), va, preferred_element_type=F32)
        l = r[:, HEAD_DIM:]
        return r[:, :HEAD_DIM] / l, m + jnp.log(l) * LOG2E

    def pattern(stride, o_out, l_out):
        span = band * stride

        def one_block(t):
            grp = t // stride
            res = t % stride
            q0 = grp * span + res
            k0 = sup + q0 - span
            if stride == 1:
                qsl = pl.ds(pl.multiple_of(q0, band), band)
                ksl = pl.ds(pl.multiple_of(k0, band), 2 * band)
            else:
                qsl = pl.ds(q0, band, stride=stride)
                ksl = pl.ds(k0, 2 * band, stride=stride)
            at_start = jnp.logical_and(grp == 0, jnp.logical_not(has_prev))
            o, lse = attend(q_ref[qsl, :], kn[ksl, :], vn[ksl, :], at_start)
            o_out[qsl, :] = o
            l_out[qsl, :] = lse

        def body(g, c):
            for u in range(DIL_UNROLL):
                one_block(g * DIL_UNROLL + u)
            return c
        lax.fori_loop(0, sup // band // DIL_UNROLL, body, 0)

    pattern(DIL_STRIDES[0], o1, l1)
    pattern(DIL_STRIDES[1], o2, l2)
    pattern(DIL_STRIDES[2], o3, l3)

    rows_per_step = 256

    def merge(i, c):
        rows = pl.ds(pl.multiple_of(i * rows_per_step, rows_per_step), rows_per_step)
        a1, a2, a3 = l1[rows, :], l2[rows, :], l3[rows, :]
        mx = jnp.maximum(jnp.maximum(a1, a2), a3)
        w1, w2, w3 = jnp.exp2(a1 - mx), jnp.exp2(a2 - mx), jnp.exp2(a3 - mx)
        num = w1 * o1[rows, :] + w2 * o2[rows, :] + w3 * o3[rows, :]
        o_ref[rows, :] = (num / (w1 + w2 + w3)).astype(o_ref.dtype)
        return c
    lax.fori_loop(0, sup // rows_per_step, merge, 0)


def _dilated(proj, batch, seq):
    sup = DIL_SUPER
    blk = (None, sup, HEAD_DIM)
    prev = lambda sb: jnp.maximum(sb - 1, 0)
    scr = pltpu.VMEM((sup, HEAD_DIM), F32)
    win = pltpu.VMEM((2 * sup, HEAD_DIM), F32)
    return pl.pallas_call(
        _dilated_kernel,
        out_shape=jax.ShapeDtypeStruct((batch, seq, GROUP_WIDTH), BF16),
        grid=(batch, seq // sup, GROUP_HEADS),
        in_specs=[
            pl.BlockSpec(blk, lambda b, sb, h: (b, sb, SL_A_Q * GROUP_HEADS + h)),
            pl.BlockSpec(blk, lambda b, sb, h: (b, prev(sb), SL_A_K * GROUP_HEADS + h)),
            pl.BlockSpec(blk, lambda b, sb, h: (b, sb, SL_A_K * GROUP_HEADS + h)),
            pl.BlockSpec(blk, lambda b, sb, h: (b, prev(sb), SL_A_V * GROUP_HEADS + h)),
            pl.BlockSpec(blk, lambda b, sb, h: (b, sb, SL_A_V * GROUP_HEADS + h)),
        ],
        out_specs=pl.BlockSpec(blk, lambda b, sb, h: (b, sb, h)),
        scratch_shapes=[win, win, scr, scr, scr, scr, scr, scr],
        compiler_params=_params("parallel", "parallel", "parallel"),
        name="dilated",
    )(proj, proj, proj, proj, proj)


MOBA_TQ = 2 * MOBA_BLOCK
MOBA_BLOCK_SHIFT = 8
assert 1 << MOBA_BLOCK_SHIFT == MOBA_BLOCK


def _moba_kernel(q_ref, k_ref, v_ref, o_ref, kaug, vaug, kmean, qaug, m_all, acc_all, s_a, s_b):
    blk = MOBA_BLOCK
    tq = MOBA_TQ
    hd = HEAD_DIM
    seq = k_ref.shape[0]
    n_blocks = seq // blk
    n_tiles = seq // tq

    lane = lax.broadcasted_iota(jnp.int32, (blk, hd), 1)
    ones = jnp.ones((blk, hd), BF16)
    for n in range(n_blocks):
        rows = slice(n * blk, (n + 1) * blk)
        kb = k_ref[rows, :]
        kmean[n:n + 1, :] = jnp.mean(kb.astype(F32), axis=0, keepdims=True)
        kaug[rows, 0:hd] = kb
        kaug[rows, hd:2 * hd] = jnp.where(lane == n, 1.0, 0.0).astype(BF16)
        vaug[rows, 0:hd] = v_ref[rows, :]
        vaug[rows, hd:2 * hd] = ones

    def tile_rows(t):
        return pl.ds(pl.multiple_of(t * tq, tq), tq)

    km = kmean[...]
    km_hi = km.astype(BF16)
    rest = km - km_hi.astype(F32)
    km_mid = rest.astype(BF16)
    km_lo = (rest - km_mid.astype(F32)).astype(BF16)
    parts = _dot_nt(jnp.concatenate([km_hi, km_mid, km_lo], axis=0), q_ref[...])
    gate = parts[0:n_blocks] + parts[n_blocks:2 * n_blocks] + parts[2 * n_blocks:3 * n_blocks]
    bidx = lax.broadcasted_iota(jnp.int32, (n_blocks, seq), 0)
    bidx_f = bidx.astype(F32)
    qblk = lax.shift_right_logical(lax.broadcasted_iota(jnp.int32, (n_blocks, seq), 1), MOBA_BLOCK_SHIFT)
    g = jnp.where(bidx < qblk, gate, NEG_INF)
    sel_bias = jnp.full((n_blocks, seq), NEG_INF, F32)
    for k in range(MOBA_TOPK):
        mx = jnp.max(g, axis=0, keepdims=True)
        first = jnp.min(jnp.where(g == mx, bidx_f, float(n_blocks)), axis=0, keepdims=True)
        hit = bidx_f == first
        sel_bias = jnp.where(hit, jnp.where(k < qblk, 0.0, NEG_INF), sel_bias)
        g = jnp.where(hit, -jnp.inf, g)
    sel_bias = jnp.where(bidx == qblk, 0.0, sel_bias)
    sel_bias = jnp.concatenate([sel_bias, jnp.zeros((hd - n_blocks, seq), F32)], axis=0)
    qaug[:, 0:hd] = q_ref[...]
    for t in range(n_tiles):
        rows = slice(t * tq, (t + 1) * tq)
        qaug[rows, hd:2 * hd] = sel_bias[:, rows].T.astype(BF16)

    def scores(t, c):
        return _dot_nt(qaug[tile_rows(t), :], kaug[tile_rows(c), :])

    def two_steps(first, advance, n_steps, step):
        assert n_steps % 2 == 0
        s_a[...] = scores(*first)

        def pair(j, tc0):
            tc1 = advance(*tc0)
            tc2 = advance(*tc1)
            s_b[...] = scores(*tc1)
            step(s_a, *tc0)
            s_a[...] = scores(*(jnp.minimum(v, n_tiles - 1) for v in tc2))
            step(s_b, *tc1)
            return tc2
        lax.fori_loop(0, n_steps // 2, pair, tuple(jnp.int32(v) for v in first))

    r_i = lax.broadcasted_iota(jnp.int32, (tq, tq), 0)
    c_i = lax.broadcasted_iota(jnp.int32, (tq, tq), 1)

    def diagonal_step(s_ref, t, c):
        s = jnp.where(c_i <= r_i, s_ref[...], NEG_INF)
        m = jnp.max(s, axis=-1, keepdims=True)
        p = jnp.exp2((s - m) * EXP2_SCALE)
        m_all[tile_rows(t), :] = m
        acc_all[tile_rows(t), :] = jnp.dot(p.astype(BF16), vaug[tile_rows(c), :], preferred_element_type=F32)

    def past_step(s_ref, t, c):
        s = s_ref[...]
        m_old = m_all[tile_rows(t), :]
        m_new = jnp.maximum(m_old, jnp.max(s, axis=-1, keepdims=True))
        alpha = jnp.exp2((m_old - m_new) * EXP2_SCALE)
        p = jnp.exp2((s - m_new) * EXP2_SCALE)
        m_all[tile_rows(t), :] = m_new
        acc_all[tile_rows(t), :] = alpha * acc_all[tile_rows(t), :] + jnp.dot(
            p.astype(BF16), vaug[tile_rows(c), :], preferred_element_type=F32)

    def next_past(t, c):
        wrap = c + 1 == t
        return jnp.where(wrap, t + 1, t), jnp.where(wrap, 0, c + 1)

    two_steps((0, 0), lambda t, c: (t + 1, c + 1), n_tiles, diagonal_step)
    two_steps((1, 0), next_past, n_tiles * (n_tiles - 1) // 2, past_step)

    def finish(t, carry):
        acc = acc_all[tile_rows(t), :]
        o_ref[tile_rows(t), :] = (acc[:, 0:hd] / acc[:, hd:2 * hd]).astype(o_ref.dtype)
        return carry
    lax.fori_loop(0, n_tiles, finish, 0)


def _moba(proj, batch, seq):
    tq = MOBA_TQ
    hd = HEAD_DIM
    full = lambda sl: pl.BlockSpec((None, seq, hd), lambda b, h: (b, 0, _bf16_slice(sl) * GROUP_HEADS + h))
    return pl.pallas_call(
        _moba_kernel,
        out_shape=jax.ShapeDtypeStruct((batch, seq, GROUP_WIDTH), BF16),
        grid=(batch, GROUP_HEADS),
        in_specs=[full(SL_C_Q), full(SL_C_K), full(SL_C_V)],
        out_specs=pl.BlockSpec((None, seq, hd), lambda b, h: (b, 0, h)),
        scratch_shapes=[
            pltpu.VMEM((seq, 2 * hd), BF16),
            pltpu.VMEM((seq, 2 * hd), BF16),
            pltpu.VMEM((seq // MOBA_BLOCK, hd), F32),
            pltpu.VMEM((seq, 2 * hd), BF16),
            pltpu.VMEM((seq, 1), F32),
            pltpu.VMEM((seq, 2 * hd), F32),
            pltpu.VMEM((tq, tq), F32),
            pltpu.VMEM((tq, tq), F32),
        ],
        compiler_params=_params("parallel", "parallel"),
        name="moba",
    )(proj, proj, proj)


def _memkv_kernel(mem_ref, g_ref, w_ref, o_ref):
    x = mem_ref[...]
    y = x * lax.rsqrt(jnp.mean(x * x, axis=-1, keepdims=True) + NORM_EPS) * g_ref[...]
    o_ref[...] = jnp.dot(y.astype(BF16), w_ref[...].astype(BF16),
                         preferred_element_type=F32).astype(o_ref.dtype)


def _memkv(mem, mem_norm_g, w_stack, layer):
    batch, n_mem, d = mem.shape
    n_cols = w_stack.shape[2]
    return pl.pallas_call(
        _memkv_kernel,
        out_shape=jax.ShapeDtypeStruct((batch, n_mem, n_cols), BF16),
        grid=(batch,),
        in_specs=[
            pl.BlockSpec((None, n_mem, d), lambda b: (b, 0, 0)),
            pl.BlockSpec((1, d), lambda b: (0, 0)),
            pl.BlockSpec((None, d, n_cols), lambda b: (layer, 0, 0)),
        ],
        out_specs=pl.BlockSpec((None, n_mem, n_cols), lambda b: (b, 0, 0)),
        compiler_params=_params("parallel"),
        name="memkv",
    )(mem, mem_norm_g.reshape(1, d), w_stack)


LOCAL_TM = 512


def _local_kernel(bu_ref, bv_ref, db_ref, dc_ref, dh_ref, hc_ref, hh_ref, xq_ref,
                  ws_ref, bs_ref, lng_ref, lnb_ref, cw_ref, kv_ref, o_ref, zbuf):
    tm = LOCAL_TM
    gw = GROUP_WIDTH

    u = jax.nn.gelu(bu_ref[...].astype(F32))
    v = jax.nn.gelu(bv_ref[...].astype(F32))
    mu = jnp.mean(v, axis=-1, keepdims=True)
    vc = v - mu
    var = jnp.mean(vc * vc, axis=-1, keepdims=True)
    vnorm = (vc * lax.rsqrt(var + NORM_EPS) * lng_ref[...] + lnb_ref[...]).astype(BF16)
    r = lax.broadcasted_iota(jnp.int32, (SGU_CHUNK, SGU_CHUNK), 0)
    c = lax.broadcasted_iota(jnp.int32, (SGU_CHUNK, SGU_CHUNK), 1)
    for hd in range(GROUP_HEADS):
        cols = slice(hd * HEAD_DIM, (hd + 1) * HEAD_DIM)
        w_causal = jnp.where(c <= r, ws_ref[hd], 0.0).astype(BF16)
        bias = bs_ref[:, hd:hd + 1]
        for ch in range(tm // SGU_CHUNK):
            rows = slice(ch * SGU_CHUNK, (ch + 1) * SGU_CHUNK)
            mixed = jnp.dot(w_causal, vnorm[rows, cols], preferred_element_type=F32) + bias
            o_ref[rows, cols] = (u[rows, cols] * mixed).astype(o_ref.dtype)

    first_tile = pl.program_id(1) == 0
    zbuf[0:CONV_HALO, :] = jnp.where(first_tile, 0.0, hc_ref[...].astype(F32) * hh_ref[...].astype(F32))
    zbuf[CONV_HALO:CONV_HALO + tm, :] = dc_ref[...].astype(F32) * dh_ref[...].astype(F32)
    y = cw_ref[CONV_WIDTH - 1:CONV_WIDTH, :] * zbuf[CONV_HALO:CONV_HALO + tm, :]
    for tap in range(CONV_WIDTH - 1):
        back = CONV_WIDTH - 1 - tap
        y = y + cw_ref[tap:tap + 1, :] * zbuf[CONV_HALO - back:CONV_HALO - back + tm, :]
    o_ref[:, gw:2 * gw] = (db_ref[...].astype(F32) * y).astype(o_ref.dtype)

    for hd in range(GROUP_HEADS):
        cols = slice(hd * HEAD_DIM, (hd + 1) * HEAD_DIM)
        kh = kv_ref[:, hd * HEAD_DIM:(hd + 1) * HEAD_DIM]
        vh = kv_ref[:, gw + hd * HEAD_DIM:gw + (hd + 1) * HEAD_DIM]
        s = _dot_nt(xq_ref[:, cols], kh) * ATTN_SCALE
        m = jnp.max(s, axis=-1, keepdims=True)
        p = jnp.exp(s - m)
        l = jnp.sum(p, axis=-1, keepdims=True)
        o = jnp.dot(p.astype(BF16), vh, preferred_element_type=F32) / l
        o_ref[:, 2 * gw + hd * HEAD_DIM:2 * gw + (hd + 1) * HEAD_DIM] = o.astype(o_ref.dtype)


def _local(proj, memkv, sgu_w, sgu_b_t, ln_g, ln_b, conv_w, batch, seq):
    tm = LOCAL_TM
    gw = GROUP_WIDTH
    halo_per_tile = tm // CONV_HALO
    tile = lambda sl: pl.BlockSpec((None, tm, gw), lambda b, i: (b, i, _bf16_slice(sl)))
    halo = lambda sl: pl.BlockSpec(
        (None, CONV_HALO, gw), lambda b, i: (b, jnp.maximum(i * halo_per_tile - 1, 0), _bf16_slice(sl)))
    whole = lambda a: pl.BlockSpec(a.shape, lambda b, i: (0,) * a.ndim)
    n_mem = memkv.shape[1]
    return pl.pallas_call(
        _local_kernel,
        out_shape=jax.ShapeDtypeStruct((batch, seq, 3 * gw), BF16),
        grid=(batch, seq // tm),
        in_specs=[
            tile(SL_B_U), tile(SL_B_V), tile(SL_D_B), tile(SL_D_C), tile(SL_D_H),
            halo(SL_D_C), halo(SL_D_H), tile(SL_X_Q),
            whole(sgu_w), whole(sgu_b_t), whole(ln_g), whole(ln_b), whole(conv_w),
            pl.BlockSpec((None, n_mem, 2 * gw), lambda b, i: (b, 0, 0)),
        ],
        out_specs=pl.BlockSpec((None, tm, 3 * gw), lambda b, i: (b, i, 0)),
        scratch_shapes=[pltpu.VMEM((CONV_HALO + tm, gw), F32)],
        compiler_params=_params("parallel", "parallel"),
        name="local",
    )(proj, proj, proj, proj, proj, proj, proj, proj,
      sgu_w, sgu_b_t, ln_g, ln_b, conv_w, memkv)


OUTPROJ_TM = 256


def _outproj_kernel(ya_ref, yl_ref, yc_ref, ga_ref, gb_ref, gc_ref, gd_ref, gx_ref,
                    ng_ref, w_ref, x_ref, pg_ref, *outs, final):
    gw = GROUP_WIDTH
    branches = (
        (lambda: ya_ref[...], ga_ref),
        (lambda: yl_ref[:, 0:gw], gb_ref),
        (lambda: yc_ref[...], gc_ref),
        (lambda: yl_ref[:, gw:2 * gw], gd_ref),
        (lambda: yl_ref[:, 2 * gw:3 * gw], gx_ref),
    )
    acc = x_ref[...]
    for grp, (o, gate_ref) in enumerate(branches):
        rows = slice(grp * gw, (grp + 1) * gw)
        y = o().astype(F32) * jax.nn.silu(gate_ref[...].astype(F32))
        y = y * lax.rsqrt(jnp.mean(y * y, axis=-1, keepdims=True) + NORM_EPS)
        y = (y * ng_ref[:, rows]).astype(BF16)
        acc = acc + jnp.dot(y, w_ref[rows, :], preferred_element_type=F32)
    normed = acc * lax.rsqrt(jnp.mean(acc * acc, axis=-1, keepdims=True) + NORM_EPS) * pg_ref[...]
    if final:
        (o_ref,) = outs
        o_ref[...] = normed
    else:
        o_ref, h_ref = outs
        o_ref[...] = acc
        h_ref[...] = normed.astype(h_ref.dtype)


def _outproj(ya, yl, yc, proj_b, out_norm_g, w_bf16, x2d, post_g, final):
    m, d = x2d.shape
    tm = OUTPROJ_TM
    gw = GROUP_WIDTH
    gate = lambda sl: pl.BlockSpec((tm, gw), lambda i: (i, _bf16_slice(sl)))
    row_tile = pl.BlockSpec((tm, d), lambda i: (i, 0))
    x_out = jax.ShapeDtypeStruct((m, d), F32)
    return pl.pallas_call(
        functools.partial(_outproj_kernel, final=final),
        out_shape=x_out if final else (x_out, jax.ShapeDtypeStruct((m, d), BF16)),
        grid=(m // tm,),
        in_specs=[
            pl.BlockSpec((tm, gw), lambda i: (i, 0)),
            pl.BlockSpec((tm, 3 * gw), lambda i: (i, 0)),
            pl.BlockSpec((tm, gw), lambda i: (i, 0)),
            gate(SL_A_G), gate(SL_B_G), gate(SL_C_G), gate(SL_D_G), gate(SL_X_G),
            pl.BlockSpec((1, MIX_WIDTH), lambda i: (0, 0)),
            pl.BlockSpec((MIX_WIDTH, d), lambda i: (0, 0)),
            row_tile,
            pl.BlockSpec((1, d), lambda i: (0, 0)),
        ],
        out_specs=row_tile if final else (row_tile, row_tile),
        compiler_params=_params("parallel"),
        name="outproj_final" if final else "outproj",
    )(ya, yl, yc, proj_b, proj_b, proj_b, proj_b, proj_b,
      out_norm_g.reshape(1, MIX_WIDTH), w_bf16, x2d, post_g.reshape(1, d))


def _rotary_tables(seq):
    half = HEAD_DIM // 2
    inv_freq = ROPE_THETA ** (-jnp.arange(half, dtype=F32) / half)
    ang = jnp.arange(seq, dtype=jnp.int32).astype(F32)[:, None] * inv_freq[None, :]
    cos, sin = jnp.cos(ang), jnp.sin(ang)
    return jnp.concatenate([cos, cos], axis=-1), jnp.concatenate([-sin, sin], axis=-1)


def kernel(x, mem, norm_g, w_in, sgu_w, sgu_b, sgu_ln_g, sgu_ln_b, conv_w, mem_norm_g, w_mem_kv,
           out_norm_g, w_out, final_norm_g):
    batch, seq, d = x.shape
    depth = w_in.shape[0]
    cos_t, sin_t = _rotary_tables(seq)
    x2d = x.reshape(batch * seq, d)
    h2d = _rmsnorm(x2d, norm_g[0])
    flat = lambda t: t.reshape(batch * seq, t.shape[-1])
    for i in range(depth):
        final = i == depth - 1
        proj_f, proj_b = _inproj(h2d, w_in, i, cos_t, sin_t, seq)
        proj_a = proj_f.reshape(batch, seq, proj_f.shape[-1])
        proj_r = proj_b.reshape(batch, seq, proj_b.shape[-1])
        memkv = _memkv(mem, mem_norm_g[i], w_mem_kv, i)
        ya = _dilated(proj_a, batch, seq)
        yc = _moba(proj_r, batch, seq)
        yl = _local(proj_r, memkv, sgu_w[i], sgu_b[i].T, sgu_ln_g[i].reshape(1, GROUP_WIDTH),
                    sgu_ln_b[i].reshape(1, GROUP_WIDTH), conv_w[i], batch, seq)
        out = _outproj(flat(ya), flat(yl), flat(yc), proj_b, out_norm_g[i], w_out[i].astype(BF16),
                       x2d, final_norm_g if final else norm_g[i + 1], final)
        if final:
            x2d = out
        else:
            x2d, h2d = out
    return x2d.reshape(batch, seq, d)
```

```python
import functools

import jax
import jax.numpy as jnp
from jax import lax
from jax.experimental import pallas as pl
from jax.experimental.pallas import tpu as pltpu

F32 = jnp.float32
BF16 = jnp.bfloat16

HEAD_DIM = 128
GROUP_HEADS = 4
GROUP_WIDTH = GROUP_HEADS * HEAD_DIM
N_GROUPS = 5
MIX_WIDTH = N_GROUPS * GROUP_WIDTH
IN_SLICES = 17
ROPE_THETA = 10000.0
NORM_EPS = 1e-6
NEG_INF = -1e30
ATTN_SCALE = HEAD_DIM ** -0.5
LOG2E = 1.4426950408889634
EXP2_SCALE = ATTN_SCALE * LOG2E

DIL_BAND = 128
DIL_STRIDES = (1, 4, 16)
DIL_SUPER = DIL_BAND * DIL_STRIDES[-1]
SGU_CHUNK = 128
MOBA_BLOCK = 256
MOBA_TOPK = 3
CONV_WIDTH = 3
CONV_HALO = 16

SL_A_Q, SL_A_K, SL_A_V, SL_A_G = 0, 1, 2, 3
SL_B_U, SL_B_V, SL_B_G = 4, 5, 6
SL_C_Q, SL_C_K, SL_C_V, SL_C_G = 7, 8, 9, 10
SL_D_B, SL_D_C, SL_D_H, SL_D_G = 11, 12, 13, 14
SL_X_Q, SL_X_G = 15, 16
ROTARY_SLICES = (SL_A_Q, SL_A_K, SL_C_Q, SL_C_K)
N_F32_SLICES = 3


def _bf16_slice(sl):
    assert sl >= N_F32_SLICES
    return sl - N_F32_SLICES


VMEM_LIMIT_BYTES = 56 * 1024 * 1024


def _params(*semantics):
    return pltpu.CompilerParams(dimension_semantics=semantics, vmem_limit_bytes=VMEM_LIMIT_BYTES)


def _dot_nt(a, b):
    return lax.dot_general(a, b, (((1,), (1,)), ((), ())), preferred_element_type=F32)


NORM_TM = 512


def _rmsnorm_kernel(x_ref, g_ref, o_ref):
    x = x_ref[...]
    y = x * lax.rsqrt(jnp.mean(x * x, axis=-1, keepdims=True) + NORM_EPS) * g_ref[...]
    o_ref[...] = y.astype(o_ref.dtype)


def _rmsnorm(x2d, g):
    m, d = x2d.shape
    return pl.pallas_call(
        _rmsnorm_kernel,
        out_shape=jax.ShapeDtypeStruct((m, d), BF16),
        grid=(m // NORM_TM,),
        in_specs=[pl.BlockSpec((NORM_TM, d), lambda i: (i, 0)), pl.BlockSpec((1, d), lambda i: (0, 0))],
        out_specs=pl.BlockSpec((NORM_TM, d), lambda i: (i, 0)),
        compiler_params=_params("parallel"),
        name="rmsnorm",
    )(x2d, g.reshape(1, d))


INPROJ_TM = 2048


INPROJ_ROW_CHUNKS = 4


def _inproj_kernel(h_ref, w_ref, cos_ref, sin_ref, of_ref, ob_ref):
    n = pl.program_id(1)
    chunk_rows = INPROJ_TM // INPROJ_ROW_CHUNKS

    def project(o_ref, rotary):
        w = w_ref[...].astype(BF16)
        for r in range(INPROJ_ROW_CHUNKS):
            rows = slice(r * chunk_rows, (r + 1) * chunk_rows)
            acc = jnp.dot(h_ref[rows, :], w, preferred_element_type=F32)
            if not rotary:
                o_ref[rows, :] = acc.astype(o_ref.dtype)
                continue
            cos = cos_ref[rows, :]
            sin = sin_ref[rows, :]
            for hh in range(GROUP_HEADS):
                sl = slice(hh * HEAD_DIM, (hh + 1) * HEAD_DIM)
                a = acc[:, sl]
                o_ref[rows, sl] = (a * cos + pltpu.roll(a, HEAD_DIM // 2, 1) * sin).astype(o_ref.dtype)

    is_f32 = n < N_F32_SLICES
    is_rot = functools.reduce(jnp.logical_or, [n == s for s in ROTARY_SLICES])
    for f32_out in (True, False):
        for rotary in (True, False):
            cond = jnp.logical_and(is_f32 == f32_out, is_rot == rotary)
            pl.when(cond)(functools.partial(project, of_ref if f32_out else ob_ref, rotary))


def _inproj(h2d, w_stack, layer, cos_t, sin_t, seq):
    m, d = h2d.shape
    gw = GROUP_WIDTH
    n_slices = w_stack.shape[2] // gw
    tm = INPROJ_TM
    seq_tiles = seq // tm
    return pl.pallas_call(
        _inproj_kernel,
        out_shape=(jax.ShapeDtypeStruct((m, N_F32_SLICES * gw), F32),
                   jax.ShapeDtypeStruct((m, (n_slices - N_F32_SLICES) * gw), BF16)),
        grid=(m // tm, n_slices),
        in_specs=[
            pl.BlockSpec((tm, d), lambda i, n: (i, 0)),
            pl.BlockSpec((None, d, gw), lambda i, n: (layer, 0, n)),
            pl.BlockSpec((tm, HEAD_DIM), lambda i, n: (i % seq_tiles, 0)),
            pl.BlockSpec((tm, HEAD_DIM), lambda i, n: (i % seq_tiles, 0)),
        ],
        out_specs=(pl.BlockSpec((tm, gw), lambda i, n: (i, jnp.minimum(n, N_F32_SLICES - 1))),
                   pl.BlockSpec((tm, gw), lambda i, n: (i, jnp.maximum(n - N_F32_SLICES, 0)))),
        compiler_params=_params("parallel", "arbitrary"),
        name="inproj",
    )(h2d, w_stack, cos_t, sin_t)


def _dilated_kernel(q_ref, kp_ref, ko_ref, vp_ref, vo_ref, o_ref,
                    q4, k4, v4, o1, o2, o3, l1, l2, l3):
    sup = DIL_SUPER
    band = DIL_BAND
    mid = DIL_STRIDES[1]
    assert DIL_STRIDES == (1, mid, mid * mid)
    per = sup // mid
    has_prev = pl.program_id(1) > 0

    for r in range(mid):
        q4[r * per:(r + 1) * per, :] = q_ref[pl.ds(r, per, stride=mid), :]
        for dst, prev_ref, own_ref in ((k4, kp_ref, ko_ref), (v4, vp_ref, vo_ref)):
            dst[2 * r * per:(2 * r + 1) * per, :] = prev_ref[pl.ds(r, per, stride=mid), :]
            dst[(2 * r + 1) * per:(2 * r + 2) * per, :] = own_ref[pl.ds(r, per, stride=mid), :]

    qi = lax.broadcasted_iota(jnp.int32, (band, 2 * band), 0)
    kj = lax.broadcasted_iota(jnp.int32, (band, 2 * band), 1)
    in_band = (kj >= qi) & (kj <= qi + band)
    bias_any = jnp.where(in_band, 0.0, NEG_INF).astype(F32)
    bias_start = jnp.where(in_band & (kj >= band), 0.0, NEG_INF).astype(F32)
    at_start = jnp.logical_not(has_prev)
    ones = jnp.ones((2 * band, HEAD_DIM), BF16)

    def attend(qb, kb, vb, first_band):
        bias = jnp.where(at_start, bias_start, bias_any) if first_band else bias_any
        t = _dot_nt(qb.astype(BF16), kb.astype(BF16)) * EXP2_SCALE + bias
        m = jnp.max(t, axis=-1, keepdims=True)
        p = jnp.exp2(t - m)
        va = jnp.concatenate([vb.astype(BF16), ones], axis=1)
        r = jnp.dot(p.astype(BF16), va, preferred_element_type=F32)
        l = r[:, HEAD_DIM:]
        return r[:, :HEAD_DIM] / l, m + jnp.log(l) * LOG2E

    for n in range(sup // band):
        rows = slice(n * band, (n + 1) * band)
        if n == 0:
            kb = jnp.concatenate([kp_ref[sup - band:sup, :], ko_ref[0:band, :]], axis=0)
            vb = jnp.concatenate([vp_ref[sup - band:sup, :], vo_ref[0:band, :]], axis=0)
        else:
            kb = ko_ref[(n - 1) * band:(n + 1) * band, :]
            vb = vo_ref[(n - 1) * band:(n + 1) * band, :]
        o1[rows, :], l1[rows, :] = attend(q_ref[rows, :], kb, vb, n == 0)

    for r in range(mid):
        for n in range(per // band):
            rows = slice(r * per + n * band, r * per + (n + 1) * band)
            keys = slice((2 * r + 1) * per + (n - 1) * band, (2 * r + 1) * per + (n + 1) * band)
            o2[rows, :], l2[rows, :] = attend(q4[rows, :], k4[keys, :], v4[keys, :], n == 0)

    for r in range(mid):
        for a in range(mid):
            rows = pl.ds(r * per + a, band, stride=mid)
            keys = pl.ds(2 * r * per + a, 2 * band, stride=mid)
            o3[rows, :], l3[rows, :] = attend(q4[rows, :], k4[keys, :], v4[keys, :], True)

    rows_per_step = 256

    def merge(i, c):
        r = i // (per // rows_per_step)
        l0 = (i % (per // rows_per_step)) * rows_per_step
        rows = pl.ds(pl.multiple_of(r * per + l0, rows_per_step), rows_per_step)
        positions = pl.ds(l0 * mid + r, rows_per_step, stride=mid)
        a1, a2, a3 = l1[positions, :], l2[rows, :], l3[rows, :]
        mx = jnp.maximum(jnp.maximum(a1, a2), a3)
        w1, w2, w3 = jnp.exp2(a1 - mx), jnp.exp2(a2 - mx), jnp.exp2(a3 - mx)
        num = w1 * o1[positions, :] + w2 * o2[rows, :] + w3 * o3[rows, :]
        o_ref[positions, :] = num / (w1 + w2 + w3)
        return c
    lax.fori_loop(0, sup // rows_per_step, merge, 0)


def _dilated(proj, batch, seq):
    sup = DIL_SUPER
    blk = (None, sup, HEAD_DIM)
    prev = lambda sb: jnp.maximum(sb - 1, 0)
    scr = pltpu.VMEM((sup, HEAD_DIM), F32)
    win = pltpu.VMEM((2 * sup, HEAD_DIM), F32)
    return pl.pallas_call(
        _dilated_kernel,
        out_shape=jax.ShapeDtypeStruct((batch, seq, GROUP_WIDTH), F32),
        grid=(batch, seq // sup, GROUP_HEADS),
        in_specs=[
            pl.BlockSpec(blk, lambda b, sb, h: (b, sb, SL_A_Q * GROUP_HEADS + h)),
            pl.BlockSpec(blk, lambda b, sb, h: (b, prev(sb), SL_A_K * GROUP_HEADS + h)),
            pl.BlockSpec(blk, lambda b, sb, h: (b, sb, SL_A_K * GROUP_HEADS + h)),
            pl.BlockSpec(blk, lambda b, sb, h: (b, prev(sb), SL_A_V * GROUP_HEADS + h)),
            pl.BlockSpec(blk, lambda b, sb, h: (b, sb, SL_A_V * GROUP_HEADS + h)),
        ],
        out_specs=pl.BlockSpec(blk, lambda b, sb, h: (b, sb, h)),
        scratch_shapes=[scr, win, win, scr, scr, scr, scr, scr, scr],
        compiler_params=_params("parallel", "parallel", "parallel"),
        name="dilated",
    )(proj, proj, proj, proj, proj)


MOBA_TQ = 2 * MOBA_BLOCK
MOBA_BLOCK_SHIFT = 8
assert 1 << MOBA_BLOCK_SHIFT == MOBA_BLOCK
MOBA_DIAG_STEPS_PER_TRIP = 8
MOBA_PAST_STEPS_PER_TRIP = 4


def _moba_kernel(q_ref, k_ref, v_ref, o_ref, kaug, vaug, kmean, qaug, m_all, acc_all, s_a, s_b):
    blk = MOBA_BLOCK
    tq = MOBA_TQ
    hd = HEAD_DIM
    seq = k_ref.shape[0]
    n_blocks = seq // blk
    n_tiles = seq // tq

    lane = lax.broadcasted_iota(jnp.int32, (blk, hd), 1)
    ones = jnp.ones((blk, hd), BF16)
    for n in range(n_blocks):
        rows = slice(n * blk, (n + 1) * blk)
        kb = k_ref[rows, :]
        kmean[n:n + 1, :] = jnp.mean(kb.astype(F32), axis=0, keepdims=True)
        kaug[rows, 0:hd] = kb
        kaug[rows, hd:2 * hd] = jnp.where(lane == n, 1.0, 0.0).astype(BF16)
        vaug[rows, 0:hd] = v_ref[rows, :]
        vaug[rows, hd:2 * hd] = ones

    def tile_rows(t):
        return pl.ds(pl.multiple_of(t * tq, tq), tq)

    km = kmean[...]
    km_hi = km.astype(BF16)
    rest = km - km_hi.astype(F32)
    km_mid = rest.astype(BF16)
    km_lo = (rest - km_mid.astype(F32)).astype(BF16)
    parts = _dot_nt(jnp.concatenate([km_hi, km_mid, km_lo], axis=0), q_ref[...])
    gate = parts[0:n_blocks] + parts[n_blocks:2 * n_blocks] + parts[2 * n_blocks:3 * n_blocks]
    bidx = lax.broadcasted_iota(jnp.int32, (n_blocks, seq), 0)
    bidx_f = bidx.astype(F32)
    qblk = lax.shift_right_logical(lax.broadcasted_iota(jnp.int32, (n_blocks, seq), 1), MOBA_BLOCK_SHIFT)
    g = jnp.where(bidx < qblk, gate, NEG_INF)
    sel_bias = jnp.full((n_blocks, seq), NEG_INF, F32)
    for k in range(MOBA_TOPK):
        mx = jnp.max(g, axis=0, keepdims=True)
        first = jnp.min(jnp.where(g == mx, bidx_f, float(n_blocks)), axis=0, keepdims=True)
        hit = bidx_f == first
        sel_bias = jnp.where(hit, jnp.where(k < qblk, 0.0, NEG_INF), sel_bias)
        g = jnp.where(hit, -jnp.inf, g)
    sel_bias = jnp.where(bidx == qblk, 0.0, sel_bias)
    sel_bias = jnp.concatenate([sel_bias, jnp.zeros((hd - n_blocks, seq), F32)], axis=0)
    qaug[:, 0:hd] = q_ref[...]
    for t in range(n_tiles):
        rows = slice(t * tq, (t + 1) * tq)
        qaug[rows, hd:2 * hd] = sel_bias[:, rows].T.astype(BF16)

    def scores(t, c):
        return _dot_nt(qaug[tile_rows(t), :], kaug[tile_rows(c), :])

    def run_steps(first, advance, n_steps, step, per_trip):
        assert n_steps % per_trip == 0 and per_trip % 2 == 0
        bufs = (s_a, s_b)
        s_a[...] = scores(*first)

        def trip(j, tc):
            for i in range(per_trip):
                nxt = advance(*tc)
                bufs[(i + 1) % 2][...] = scores(*(jnp.minimum(v, n_tiles - 1) for v in nxt))
                step(bufs[i % 2], *tc)
                tc = nxt
            return tc
        lax.fori_loop(0, n_steps // per_trip, trip, tuple(jnp.int32(v) for v in first))

    r_i = lax.broadcasted_iota(jnp.int32, (tq, tq), 0)
    c_i = lax.broadcasted_iota(jnp.int32, (tq, tq), 1)

    def diagonal_step(s_ref, t, c):
        s = jnp.where(c_i <= r_i, s_ref[...], NEG_INF)
        m = jnp.max(s, axis=-1, keepdims=True)
        p = jnp.exp2((s - m) * EXP2_SCALE)
        m_all[tile_rows(t), :] = m
        acc_all[tile_rows(t), :] = jnp.dot(p.astype(BF16), vaug[tile_rows(c), :], preferred_element_type=F32)

    def past_step(s_ref, t, c):
        s = s_ref[...]
        m_old = m_all[tile_rows(t), :]
        m_new = jnp.maximum(m_old, jnp.max(s, axis=-1, keepdims=True))
        alpha = jnp.exp2((m_old - m_new) * EXP2_SCALE)
        p = jnp.exp2((s - m_new) * EXP2_SCALE)
        m_all[tile_rows(t), :] = m_new
        acc_all[tile_rows(t), :] = alpha * acc_all[tile_rows(t), :] + jnp.dot(
            p.astype(BF16), vaug[tile_rows(c), :], preferred_element_type=F32)

    def next_past(t, c):
        wrap = c + 1 == t
        return jnp.where(wrap, t + 1, t), jnp.where(wrap, 0, c + 1)

    run_steps((0, 0), lambda t, c: (t + 1, c + 1), n_tiles, diagonal_step, MOBA_DIAG_STEPS_PER_TRIP)
    run_steps((1, 0), next_past, n_tiles * (n_tiles - 1) // 2, past_step, MOBA_PAST_STEPS_PER_TRIP)

    def finish(t, carry):
        acc = acc_all[tile_rows(t), :]
        o_ref[tile_rows(t), :] = (acc[:, 0:hd] / acc[:, hd:2 * hd]).astype(o_ref.dtype)
        return carry
    lax.fori_loop(0, n_tiles, finish, 0)


def _moba(proj, batch, seq):
    tq = MOBA_TQ
    hd = HEAD_DIM
    full = lambda sl: pl.BlockSpec((None, seq, hd), lambda b, h: (b, 0, _bf16_slice(sl) * GROUP_HEADS + h))
    return pl.pallas_call(
        _moba_kernel,
        out_shape=jax.ShapeDtypeStruct((batch, seq, GROUP_WIDTH), BF16),
        grid=(batch, GROUP_HEADS),
        in_specs=[full(SL_C_Q), full(SL_C_K), full(SL_C_V)],
        out_specs=pl.BlockSpec((None, seq, hd), lambda b, h: (b, 0, h)),
        scratch_shapes=[
            pltpu.VMEM((seq, 2 * hd), BF16),
            pltpu.VMEM((seq, 2 * hd), BF16),
            pltpu.VMEM((seq // MOBA_BLOCK, hd), F32),
            pltpu.VMEM((seq, 2 * hd), BF16),
            pltpu.VMEM((seq, 1), F32),
            pltpu.VMEM((seq, 2 * hd), F32),
            pltpu.VMEM((tq, tq), F32),
            pltpu.VMEM((tq, tq), F32),
        ],
        compiler_params=_params("parallel", "parallel"),
        name="moba",
    )(proj, proj, proj)


def _memkv_kernel(mem_ref, g_ref, w_ref, o_ref):
    x = mem_ref[...]
    y = x * lax.rsqrt(jnp.mean(x * x, axis=-1, keepdims=True) + NORM_EPS) * g_ref[...]
    o_ref[...] = jnp.dot(y.astype(BF16), w_ref[...].astype(BF16),
                         preferred_element_type=F32).astype(o_ref.dtype)


def _memkv(mem, mem_norm_g, w_stack, layer):
    batch, n_mem, d = mem.shape
    n_cols = w_stack.shape[2]
    return pl.pallas_call(
        _memkv_kernel,
        out_shape=jax.ShapeDtypeStruct((batch, n_mem, n_cols), BF16),
        grid=(batch,),
        in_specs=[
            pl.BlockSpec((None, n_mem, d), lambda b: (b, 0, 0)),
            pl.BlockSpec((1, d), lambda b: (0, 0)),
            pl.BlockSpec((None, d, n_cols), lambda b: (layer, 0, 0)),
        ],
        out_specs=pl.BlockSpec((None, n_mem, n_cols), lambda b: (b, 0, 0)),
        compiler_params=_params("parallel"),
        name="memkv",
    )(mem, mem_norm_g.reshape(1, d), w_stack)


LOCAL_TM = 512


def _local_kernel(bu_ref, bv_ref, db_ref, dc_ref, dh_ref, hc_ref, hh_ref, xq_ref,
                  ws_ref, bs_ref, lng_ref, lnb_ref, cw_ref, kv_ref, o_ref, zbuf):
    tm = LOCAL_TM
    gw = GROUP_WIDTH

    u = jax.nn.gelu(bu_ref[...].astype(F32))
    v = jax.nn.gelu(bv_ref[...].astype(F32))
    mu = jnp.mean(v, axis=-1, keepdims=True)
    vc = v - mu
    var = jnp.mean(vc * vc, axis=-1, keepdims=True)
    vnorm = (vc * lax.rsqrt(var + NORM_EPS) * lng_ref[...] + lnb_ref[...]).astype(BF16)
    r = lax.broadcasted_iota(jnp.int32, (SGU_CHUNK, SGU_CHUNK), 0)
    c = lax.broadcasted_iota(jnp.int32, (SGU_CHUNK, SGU_CHUNK), 1)
    for hd in range(GROUP_HEADS):
        cols = slice(hd * HEAD_DIM, (hd + 1) * HEAD_DIM)
        w_causal = jnp.where(c <= r, ws_ref[hd], 0.0).astype(BF16)
        bias = bs_ref[:, hd:hd + 1]
        for ch in range(tm // SGU_CHUNK):
            rows = slice(ch * SGU_CHUNK, (ch + 1) * SGU_CHUNK)
            mixed = jnp.dot(w_causal, vnorm[rows, cols], preferred_element_type=F32) + bias
            o_ref[rows, cols] = (u[rows, cols] * mixed).astype(o_ref.dtype)

    first_tile = pl.program_id(1) == 0
    zbuf[0:CONV_HALO, :] = jnp.where(first_tile, 0.0, hc_ref[...].astype(F32) * hh_ref[...].astype(F32))
    zbuf[CONV_HALO:CONV_HALO + tm, :] = dc_ref[...].astype(F32) * dh_ref[...].astype(F32)
    y = cw_ref[CONV_WIDTH - 1:CONV_WIDTH, :] * zbuf[CONV_HALO:CONV_HALO + tm, :]
    for tap in range(CONV_WIDTH - 1):
        back = CONV_WIDTH - 1 - tap
        y = y + cw_ref[tap:tap + 1, :] * zbuf[CONV_HALO - back:CONV_HALO - back + tm, :]
    o_ref[:, gw:2 * gw] = (db_ref[...].astype(F32) * y).astype(o_ref.dtype)

    for hd in range(GROUP_HEADS):
        cols = slice(hd * HEAD_DIM, (hd + 1) * HEAD_DIM)
        kh = kv_ref[:, hd * HEAD_DIM:(hd + 1) * HEAD_DIM]
        vh = kv_ref[:, gw + hd * HEAD_DIM:gw + (hd + 1) * HEAD_DIM]
        s = _dot_nt(xq_ref[:, cols], kh) * ATTN_SCALE
        m = jnp.max(s, axis=-1, keepdims=True)
        p = jnp.exp(s - m)
        l = jnp.sum(p, axis=-1, keepdims=True)
        o = jnp.dot(p.astype(BF16), vh, preferred_element_type=F32) / l
        o_ref[:, 2 * gw + hd * HEAD_DIM:2 * gw + (hd + 1) * HEAD_DIM] = o.astype(o_ref.dtype)


def _local(proj, memkv, sgu_w, sgu_b_t, ln_g, ln_b, conv_w, batch, seq):
    tm = LOCAL_TM
    gw = GROUP_WIDTH
    halo_per_tile = tm // CONV_HALO
    tile = lambda sl: pl.BlockSpec((None, tm, gw), lambda b, i: (b, i, _bf16_slice(sl)))
    halo = lambda sl: pl.BlockSpec(
        (None, CONV_HALO, gw), lambda b, i: (b, jnp.maximum(i * halo_per_tile - 1, 0), _bf16_slice(sl)))
    whole = lambda a: pl.BlockSpec(a.shape, lambda b, i: (0,) * a.ndim)
    n_mem = memkv.shape[1]
    return pl.pallas_call(
        _local_kernel,
        out_shape=jax.ShapeDtypeStruct((batch, seq, 3 * gw), BF16),
        grid=(batch, seq // tm),
        in_specs=[
            tile(SL_B_U), tile(SL_B_V), tile(SL_D_B), tile(SL_D_C), tile(SL_D_H),
            halo(SL_D_C), halo(SL_D_H), tile(SL_X_Q),
            whole(sgu_w), whole(sgu_b_t), whole(ln_g), whole(ln_b), whole(conv_w),
            pl.BlockSpec((None, n_mem, 2 * gw), lambda b, i: (b, 0, 0)),
        ],
        out_specs=pl.BlockSpec((None, tm, 3 * gw), lambda b, i: (b, i, 0)),
        scratch_shapes=[pltpu.VMEM((CONV_HALO + tm, gw), F32)],
        compiler_params=_params("parallel", "parallel"),
        name="local",
    )(proj, proj, proj, proj, proj, proj, proj, proj,
      sgu_w, sgu_b_t, ln_g, ln_b, conv_w, memkv)


OUTPROJ_TM = 256


def _outproj_kernel(ya_ref, yl_ref, yc_ref, ga_ref, gb_ref, gc_ref, gd_ref, gx_ref,
                    ng_ref, w_ref, x_ref, pg_ref, *outs, final):
    gw = GROUP_WIDTH
    branches = (
        (lambda: ya_ref[...], ga_ref),
        (lambda: yl_ref[:, 0:gw], gb_ref),
        (lambda: yc_ref[...], gc_ref),
        (lambda: yl_ref[:, gw:2 * gw], gd_ref),
        (lambda: yl_ref[:, 2 * gw:3 * gw], gx_ref),
    )
    acc = x_ref[...]
    for grp, (o, gate_ref) in enumerate(branches):
        rows = slice(grp * gw, (grp + 1) * gw)
        y = o().astype(F32) * jax.nn.silu(gate_ref[...].astype(F32))
        y = y * lax.rsqrt(jnp.mean(y * y, axis=-1, keepdims=True) + NORM_EPS)
        y = (y * ng_ref[:, rows]).astype(BF16)
        acc = acc + jnp.dot(y, w_ref[rows, :], preferred_element_type=F32)
    normed = acc * lax.rsqrt(jnp.mean(acc * acc, axis=-1, keepdims=True) + NORM_EPS) * pg_ref[...]
    if final:
        (o_ref,) = outs
        o_ref[...] = normed
    else:
        o_ref, h_ref = outs
        o_ref[...] = acc
        h_ref[...] = normed.astype(h_ref.dtype)


def _outproj(ya, yl, yc, proj_b, out_norm_g, w_bf16, x2d, post_g, final):
    m, d = x2d.shape
    tm = OUTPROJ_TM
    gw = GROUP_WIDTH
    gate = lambda sl: pl.BlockSpec((tm, gw), lambda i: (i, _bf16_slice(sl)))
    row_tile = pl.BlockSpec((tm, d), lambda i: (i, 0))
    x_out = jax.ShapeDtypeStruct((m, d), F32)
    return pl.pallas_call(
        functools.partial(_outproj_kernel, final=final),
        out_shape=x_out if final else (x_out, jax.ShapeDtypeStruct((m, d), BF16)),
        grid=(m // tm,),
        in_specs=[
            pl.BlockSpec((tm, gw), lambda i: (i, 0)),
            pl.BlockSpec((tm, 3 * gw), lambda i: (i, 0)),
            pl.BlockSpec((tm, gw), lambda i: (i, 0)),
            gate(SL_A_G), gate(SL_B_G), gate(SL_C_G), gate(SL_D_G), gate(SL_X_G),
            pl.BlockSpec((1, MIX_WIDTH), lambda i: (0, 0)),
            pl.BlockSpec((MIX_WIDTH, d), lambda i: (0, 0)),
            row_tile,
            pl.BlockSpec((1, d), lambda i: (0, 0)),
        ],
        out_specs=row_tile if final else (row_tile, row_tile),
        compiler_params=_params("parallel"),
        name="outproj_final" if final else "outproj",
    )(ya, yl, yc, proj_b, proj_b, proj_b, proj_b, proj_b,
      out_norm_g.reshape(1, MIX_WIDTH), w_bf16, x2d, post_g.reshape(1, d))


def _rotary_tables(seq):
    half = HEAD_DIM // 2
    inv_freq = ROPE_THETA ** (-jnp.arange(half, dtype=F32) / half)
    ang = jnp.arange(seq, dtype=jnp.int32).astype(F32)[:, None] * inv_freq[None, :]
    cos, sin = jnp.cos(ang), jnp.sin(ang)
    return jnp.concatenate([cos, cos], axis=-1), jnp.concatenate([-sin, sin], axis=-1)


def kernel(x, mem, norm_g, w_in, sgu_w, sgu_b, sgu_ln_g, sgu_ln_b, conv_w, mem_norm_g, w_mem_kv,
           out_norm_g, w_out, final_norm_g):
    batch, seq, d = x.shape
    depth = w_in.shape[0]
    cos_t, sin_t = _rotary_tables(seq)
    x2d = x.reshape(batch * seq, d)
    h2d = _rmsnorm(x2d, norm_g[0])
    flat = lambda t: t.reshape(batch * seq, t.shape[-1])
    for i in range(depth):
        final = i == depth - 1
        proj_f, proj_b = _inproj(h2d, w_in, i, cos_t, sin_t, seq)
        proj_a = proj_f.reshape(batch, seq, proj_f.shape[-1])
        proj_r = proj_b.reshape(batch, seq, proj_b.shape[-1])
        memkv = _memkv(mem, mem_norm_g[i], w_mem_kv, i)
        ya = _dilated(proj_a, batch, seq)
        yc = _moba(proj_r, batch, seq)
        yl = _local(proj_r, memkv, sgu_w[i], sgu_b[i].T, sgu_ln_g[i].reshape(1, GROUP_WIDTH),
                    sgu_ln_b[i].reshape(1, GROUP_WIDTH), conv_w[i], batch, seq)
        out = _outproj(flat(ya), flat(yl), flat(yc), proj_b, out_norm_g[i], w_out[i].astype(BF16),
                       x2d, final_norm_g if final else norm_g[i + 1], final)
        if final:
            x2d = out
        else:
            x2d, h2d = out
    return x2d.reshape(batch, seq, d)
```

```python
import functools

import jax
import jax.numpy as jnp
from jax import lax
from jax.experimental import pallas as pl
from jax.experimental.pallas import tpu as pltpu

F32 = jnp.float32
BF16 = jnp.bfloat16

HEAD_DIM = 128
GROUP_HEADS = 4
GROUP_WIDTH = GROUP_HEADS * HEAD_DIM
N_GROUPS = 5
MIX_WIDTH = N_GROUPS * GROUP_WIDTH
IN_SLICES = 17
ROPE_THETA = 10000.0
NORM_EPS = 1e-6
NEG_INF = -1e30
ATTN_SCALE = HEAD_DIM ** -0.5
LOG2E = 1.4426950408889634
EXP2_SCALE = ATTN_SCALE * LOG2E

DIL_BAND = 128
DIL_STRIDES = (1, 4, 16)
DIL_SUPER = DIL_BAND * DIL_STRIDES[-1]
SGU_CHUNK = 128
MOBA_BLOCK = 256
MOBA_TOPK = 3
CONV_WIDTH = 3
CONV_HALO = 16

SL_A_Q, SL_A_K, SL_A_V, SL_A_G = 0, 1, 2, 3
SL_B_U, SL_B_V, SL_B_G = 4, 5, 6
SL_C_Q, SL_C_K, SL_C_V, SL_C_G = 7, 8, 9, 10
SL_D_B, SL_D_C, SL_D_H, SL_D_G = 11, 12, 13, 14
SL_X_Q, SL_X_G = 15, 16
ROTARY_SLICES = (SL_A_Q, SL_A_K, SL_C_Q, SL_C_K)
N_F32_SLICES = 3


def _bf16_slice(sl):
    assert sl >= N_F32_SLICES
    return sl - N_F32_SLICES


VMEM_LIMIT_BYTES = 56 * 1024 * 1024


def _params(*semantics):
    return pltpu.CompilerParams(dimension_semantics=semantics, vmem_limit_bytes=VMEM_LIMIT_BYTES)


def _dot_nt(a, b):
    return lax.dot_general(a, b, (((1,), (1,)), ((), ())), preferred_element_type=F32)


NORM_TM = 512


def _rmsnorm_kernel(x_ref, g_ref, o_ref):
    x = x_ref[...]
    y = x * lax.rsqrt(jnp.mean(x * x, axis=-1, keepdims=True) + NORM_EPS) * g_ref[...]
    o_ref[...] = y.astype(o_ref.dtype)


def _rmsnorm(x2d, g):
    m, d = x2d.shape
    return pl.pallas_call(
        _rmsnorm_kernel,
        out_shape=jax.ShapeDtypeStruct((m, d), BF16),
        grid=(m // NORM_TM,),
        in_specs=[pl.BlockSpec((NORM_TM, d), lambda i: (i, 0)), pl.BlockSpec((1, d), lambda i: (0, 0))],
        out_specs=pl.BlockSpec((NORM_TM, d), lambda i: (i, 0)),
        compiler_params=_params("parallel"),
        name="rmsnorm",
    )(x2d, g.reshape(1, d))


INPROJ_TM = 2048


INPROJ_ROW_CHUNKS = 8


def _inproj_kernel(h_ref, w_ref, cos_ref, sin_ref, of_ref, ob_ref):
    n = pl.program_id(1)
    chunk_rows = INPROJ_TM // INPROJ_ROW_CHUNKS

    def project(slices, finish):
        o_ref = of_ref if slices[0] < N_F32_SLICES else ob_ref
        assert all((s < N_F32_SLICES) == (slices[0] < N_F32_SLICES) for s in slices)

        @pl.when(functools.reduce(jnp.logical_or, [n == s for s in slices]))
        def _():
            w = w_ref[...].astype(BF16)
            for r in range(INPROJ_ROW_CHUNKS):
                rows = slice(r * chunk_rows, (r + 1) * chunk_rows)
                acc = jnp.dot(h_ref[rows, :], w, preferred_element_type=F32)
                if finish == "rotary":
                    cos = cos_ref[rows, :]
                    sin = sin_ref[rows, :]
                    for hh in range(GROUP_HEADS):
                        sl = slice(hh * HEAD_DIM, (hh + 1) * HEAD_DIM)
                        a = acc[:, sl]
                        o_ref[rows, sl] = (a * cos + pltpu.roll(a, HEAD_DIM // 2, 1) * sin).astype(o_ref.dtype)
                else:
                    if finish is not None:
                        acc = finish(acc)
                    o_ref[rows, :] = acc.astype(o_ref.dtype)

    project((SL_A_Q, SL_A_K), "rotary")
    project((SL_A_V,), None)
    project((SL_C_Q, SL_C_K), "rotary")
    project((SL_A_G, SL_B_G, SL_C_G, SL_D_G, SL_X_G), jax.nn.silu)
    project((SL_B_U, SL_B_V, SL_C_V, SL_D_B, SL_D_C, SL_D_H, SL_X_Q), None)


def _inproj(h2d, w_stack, layer, cos_t, sin_t, seq):
    m, d = h2d.shape
    gw = GROUP_WIDTH
    n_slices = w_stack.shape[2] // gw
    tm = INPROJ_TM
    seq_tiles = seq // tm
    return pl.pallas_call(
        _inproj_kernel,
        out_shape=(jax.ShapeDtypeStruct((m, N_F32_SLICES * gw), F32),
                   jax.ShapeDtypeStruct((m, (n_slices - N_F32_SLICES) * gw), BF16)),
        grid=(m // tm, n_slices),
        in_specs=[
            pl.BlockSpec((tm, d), lambda i, n: (i, 0)),
            pl.BlockSpec((None, d, gw), lambda i, n: (layer, 0, n)),
            pl.BlockSpec((tm, HEAD_DIM), lambda i, n: (i % seq_tiles, 0)),
            pl.BlockSpec((tm, HEAD_DIM), lambda i, n: (i % seq_tiles, 0)),
        ],
        out_specs=(pl.BlockSpec((tm, gw), lambda i, n: (i, jnp.minimum(n, N_F32_SLICES - 1))),
                   pl.BlockSpec((tm, gw), lambda i, n: (i, jnp.maximum(n - N_F32_SLICES, 0)))),
        compiler_params=_params("parallel", "arbitrary"),
        name="inproj",
    )(h2d, w_stack, cos_t, sin_t)


def _dilated_kernel(q_ref, kp_ref, ko_ref, vp_ref, vo_ref, o_ref,
                    q4, k4, v4, o1, o2, o3, l1, l2, l3):
    sup = DIL_SUPER
    band = DIL_BAND
    mid = DIL_STRIDES[1]
    assert DIL_STRIDES == (1, mid, mid * mid)
    per = sup // mid
    has_prev = pl.program_id(1) > 0

    for r in range(mid):
        q4[r * per:(r + 1) * per, :] = q_ref[pl.ds(r, per, stride=mid), :]
        for dst, prev_ref, own_ref in ((k4, kp_ref, ko_ref), (v4, vp_ref, vo_ref)):
            dst[2 * r * per:(2 * r + 1) * per, :] = prev_ref[pl.ds(r, per, stride=mid), :]
            dst[(2 * r + 1) * per:(2 * r + 2) * per, :] = own_ref[pl.ds(r, per, stride=mid), :]

    qi = lax.broadcasted_iota(jnp.int32, (band, 2 * band), 0)
    kj = lax.broadcasted_iota(jnp.int32, (band, 2 * band), 1)
    in_band = (kj >= qi) & (kj <= qi + band)
    bias_any = jnp.where(in_band, 0.0, NEG_INF).astype(F32)
    bias_start = jnp.where(in_band & (kj >= band), 0.0, NEG_INF).astype(F32)
    at_start = jnp.logical_not(has_prev)
    ones = jnp.ones((2 * band, HEAD_DIM), BF16)

    def attend(qb, kb, vb, first_band):
        bias = jnp.where(at_start, bias_start, bias_any) if first_band else bias_any
        t = _dot_nt(qb.astype(BF16), kb.astype(BF16)) * EXP2_SCALE + bias
        m = jnp.max(t, axis=-1, keepdims=True)
        p = jnp.exp2(t - m)
        va = jnp.concatenate([vb.astype(BF16), ones], axis=1)
        r = jnp.dot(p.astype(BF16), va, preferred_element_type=F32)
        l = r[:, HEAD_DIM:]
        return r[:, :HEAD_DIM] / l, m + jnp.log(l) * LOG2E

    for n in range(sup // band):
        rows = slice(n * band, (n + 1) * band)
        if n == 0:
            kb = jnp.concatenate([kp_ref[sup - band:sup, :], ko_ref[0:band, :]], axis=0)
            vb = jnp.concatenate([vp_ref[sup - band:sup, :], vo_ref[0:band, :]], axis=0)
        else:
            kb = ko_ref[(n - 1) * band:(n + 1) * band, :]
            vb = vo_ref[(n - 1) * band:(n + 1) * band, :]
        o1[rows, :], l1[rows, :] = attend(q_ref[rows, :], kb, vb, n == 0)

    for r in range(mid):
        for n in range(per // band):
            rows = slice(r * per + n * band, r * per + (n + 1) * band)
            keys = slice((2 * r + 1) * per + (n - 1) * band, (2 * r + 1) * per + (n + 1) * band)
            o2[rows, :], l2[rows, :] = attend(q4[rows, :], k4[keys, :], v4[keys, :], n == 0)

    for r in range(mid):
        for a in range(mid):
            rows = pl.ds(r * per + a, band, stride=mid)
            keys = pl.ds(2 * r * per + a, 2 * band, stride=mid)
            o3[rows, :], l3[rows, :] = attend(q4[rows, :], k4[keys, :], v4[keys, :], True)

    rows_per_step = 256

    def merge(i, c):
        r = i // (per // rows_per_step)
        l0 = (i % (per // rows_per_step)) * rows_per_step
        rows = pl.ds(pl.multiple_of(r * per + l0, rows_per_step), rows_per_step)
        positions = pl.ds(l0 * mid + r, rows_per_step, stride=mid)
        a1, a2, a3 = l1[positions, :], l2[rows, :], l3[rows, :]
        mx = jnp.maximum(jnp.maximum(a1, a2), a3)
        w1, w2, w3 = jnp.exp2(a1 - mx), jnp.exp2(a2 - mx), jnp.exp2(a3 - mx)
        num = w1 * o1[positions, :] + w2 * o2[rows, :] + w3 * o3[rows, :]
        o_ref[positions, :] = num / (w1 + w2 + w3)
        return c
    lax.fori_loop(0, sup // rows_per_step, merge, 0)


def _dilated(proj, batch, seq):
    sup = DIL_SUPER
    blk = (None, sup, HEAD_DIM)
    prev = lambda sb: jnp.maximum(sb - 1, 0)
    scr = pltpu.VMEM((sup, HEAD_DIM), F32)
    win = pltpu.VMEM((2 * sup, HEAD_DIM), F32)
    return pl.pallas_call(
        _dilated_kernel,
        out_shape=jax.ShapeDtypeStruct((batch, seq, GROUP_WIDTH), F32),
        grid=(batch, seq // sup, GROUP_HEADS),
        in_specs=[
            pl.BlockSpec(blk, lambda b, sb, h: (b, sb, SL_A_Q * GROUP_HEADS + h)),
            pl.BlockSpec(blk, lambda b, sb, h: (b, prev(sb), SL_A_K * GROUP_HEADS + h)),
            pl.BlockSpec(blk, lambda b, sb, h: (b, sb, SL_A_K * GROUP_HEADS + h)),
            pl.BlockSpec(blk, lambda b, sb, h: (b, prev(sb), SL_A_V * GROUP_HEADS + h)),
            pl.BlockSpec(blk, lambda b, sb, h: (b, sb, SL_A_V * GROUP_HEADS + h)),
        ],
        out_specs=pl.BlockSpec(blk, lambda b, sb, h: (b, sb, h)),
        scratch_shapes=[scr, win, win, scr, scr, scr, scr, scr, scr],
        compiler_params=_params("parallel", "parallel", "parallel"),
        name="dilated",
    )(proj, proj, proj, proj, proj)


MOBA_TQ = 2 * MOBA_BLOCK
MOBA_BLOCK_SHIFT = 8
assert 1 << MOBA_BLOCK_SHIFT == MOBA_BLOCK
MOBA_DIAG_STEPS_PER_TRIP = 8
MOBA_PAST_STEPS_PER_TRIP = 4


def _moba_kernel(q_ref, k_ref, v_ref, o_ref, kaug, vaug, kmean, qaug, m_all, acc_all, s_a, s_b):
    blk = MOBA_BLOCK
    tq = MOBA_TQ
    hd = HEAD_DIM
    seq = k_ref.shape[0]
    n_blocks = seq // blk
    n_tiles = seq // tq

    lane = lax.broadcasted_iota(jnp.int32, (blk, hd), 1)
    ones = jnp.ones((blk, hd), BF16)
    for n in range(n_blocks):
        rows = slice(n * blk, (n + 1) * blk)
        kb = k_ref[rows, :]
        kmean[n:n + 1, :] = jnp.mean(kb.astype(F32), axis=0, keepdims=True)
        kaug[rows, 0:hd] = kb
        kaug[rows, hd:2 * hd] = jnp.where(lane == n, 1.0, 0.0).astype(BF16)
        vaug[rows, 0:hd] = v_ref[rows, :]
        vaug[rows, hd:2 * hd] = ones

    def tile_rows(t):
        return pl.ds(pl.multiple_of(t * tq, tq), tq)

    km = kmean[...]
    km_hi = km.astype(BF16)
    rest = km - km_hi.astype(F32)
    km_mid = rest.astype(BF16)
    km_lo = (rest - km_mid.astype(F32)).astype(BF16)
    parts = _dot_nt(jnp.concatenate([km_hi, km_mid, km_lo], axis=0), q_ref[...])
    gate = parts[0:n_blocks] + parts[n_blocks:2 * n_blocks] + parts[2 * n_blocks:3 * n_blocks]
    bidx = lax.broadcasted_iota(jnp.int32, (n_blocks, seq), 0)
    bidx_f = bidx.astype(F32)
    qblk = lax.shift_right_logical(lax.broadcasted_iota(jnp.int32, (n_blocks, seq), 1), MOBA_BLOCK_SHIFT)
    g = jnp.where(bidx < qblk, gate, NEG_INF)
    sel_bias = jnp.full((n_blocks, seq), NEG_INF, F32)
    for k in range(MOBA_TOPK):
        mx = jnp.max(g, axis=0, keepdims=True)
        first = jnp.min(jnp.where(g == mx, bidx_f, float(n_blocks)), axis=0, keepdims=True)
        hit = bidx_f == first
        sel_bias = jnp.where(hit, jnp.where(k < qblk, 0.0, NEG_INF), sel_bias)
        g = jnp.where(hit, -jnp.inf, g)
    sel_bias = jnp.where(bidx == qblk, 0.0, sel_bias)
    sel_bias = jnp.concatenate([sel_bias, jnp.zeros((hd - n_blocks, seq), F32)], axis=0)
    qaug[:, 0:hd] = q_ref[...]
    for t in range(n_tiles):
        rows = slice(t * tq, (t + 1) * tq)
        qaug[rows, hd:2 * hd] = sel_bias[:, rows].T.astype(BF16)

    def scores(t, c):
        return _dot_nt(qaug[tile_rows(t), :], kaug[tile_rows(c), :])

    def run_steps(first, advance, n_steps, step, per_trip):
        assert n_steps % per_trip == 0 and per_trip % 2 == 0
        bufs = (s_a, s_b)
        s_a[...] = scores(*first)

        def trip(j, tc):
            for i in range(per_trip):
                nxt = advance(*tc)
                bufs[(i + 1) % 2][...] = scores(*(jnp.minimum(v, n_tiles - 1) for v in nxt))
                step(bufs[i % 2], *tc)
                tc = nxt
            return tc
        lax.fori_loop(0, n_steps // per_trip, trip, tuple(jnp.int32(v) for v in first))

    r_i = lax.broadcasted_iota(jnp.int32, (tq, tq), 0)
    c_i = lax.broadcasted_iota(jnp.int32, (tq, tq), 1)

    def diagonal_step(s_ref, t, c):
        s = jnp.where(c_i <= r_i, s_ref[...], NEG_INF)
        m = jnp.max(s, axis=-1, keepdims=True)
        p = jnp.exp2((s - m) * EXP2_SCALE)
        m_all[tile_rows(t), :] = m
        acc_all[tile_rows(t), :] = jnp.dot(p.astype(BF16), vaug[tile_rows(c), :], preferred_element_type=F32)

    def past_step(s_ref, t, c):
        s = s_ref[...]
        m_old = m_all[tile_rows(t), :]
        m_new = jnp.maximum(m_old, jnp.max(s, axis=-1, keepdims=True))
        alpha = jnp.exp2((m_old - m_new) * EXP2_SCALE)
        p = jnp.exp2((s - m_new) * EXP2_SCALE)
        m_all[tile_rows(t), :] = m_new
        acc_all[tile_rows(t), :] = alpha * acc_all[tile_rows(t), :] + jnp.dot(
            p.astype(BF16), vaug[tile_rows(c), :], preferred_element_type=F32)

    def next_past(t, c):
        wrap = c + 1 == t
        return jnp.where(wrap, t + 1, t), jnp.where(wrap, 0, c + 1)

    run_steps((0, 0), lambda t, c: (t + 1, c + 1), n_tiles, diagonal_step, MOBA_DIAG_STEPS_PER_TRIP)
    run_steps((1, 0), next_past, n_tiles * (n_tiles - 1) // 2, past_step, MOBA_PAST_STEPS_PER_TRIP)

    def finish(t, carry):
        acc = acc_all[tile_rows(t), :]
        o_ref[tile_rows(t), :] = (acc[:, 0:hd] / acc[:, hd:2 * hd]).astype(o_ref.dtype)
        return carry
    lax.fori_loop(0, n_tiles, finish, 0)


def _moba(proj, batch, seq):
    tq = MOBA_TQ
    hd = HEAD_DIM
    full = lambda sl: pl.BlockSpec((None, seq, hd), lambda b, h: (b, 0, _bf16_slice(sl) * GROUP_HEADS + h))
    return pl.pallas_call(
        _moba_kernel,
        out_shape=jax.ShapeDtypeStruct((batch, seq, GROUP_WIDTH), BF16),
        grid=(batch, GROUP_HEADS),
        in_specs=[full(SL_C_Q), full(SL_C_K), full(SL_C_V)],
        out_specs=pl.BlockSpec((None, seq, hd), lambda b, h: (b, 0, h)),
        scratch_shapes=[
            pltpu.VMEM((seq, 2 * hd), BF16),
            pltpu.VMEM((seq, 2 * hd), BF16),
            pltpu.VMEM((seq // MOBA_BLOCK, hd), F32),
            pltpu.VMEM((seq, 2 * hd), BF16),
            pltpu.VMEM((seq, 1), F32),
            pltpu.VMEM((seq, 2 * hd), F32),
            pltpu.VMEM((tq, tq), F32),
            pltpu.VMEM((tq, tq), F32),
        ],
        compiler_params=_params("parallel", "parallel"),
        name="moba",
    )(proj, proj, proj)


def _memkv_kernel(mem_ref, g_ref, w_ref, o_ref):
    x = mem_ref[...]
    y = x * lax.rsqrt(jnp.mean(x * x, axis=-1, keepdims=True) + NORM_EPS) * g_ref[...]
    o_ref[...] = jnp.dot(y.astype(BF16), w_ref[...].astype(BF16),
                         preferred_element_type=F32).astype(o_ref.dtype)


def _memkv(mem, mem_norm_g, w_stack, layer):
    batch, n_mem, d = mem.shape
    n_cols = w_stack.shape[2]
    return pl.pallas_call(
        _memkv_kernel,
        out_shape=jax.ShapeDtypeStruct((batch, n_mem, n_cols), BF16),
        grid=(batch,),
        in_specs=[
            pl.BlockSpec((None, n_mem, d), lambda b: (b, 0, 0)),
            pl.BlockSpec((1, d), lambda b: (0, 0)),
            pl.BlockSpec((None, d, n_cols), lambda b: (layer, 0, 0)),
        ],
        out_specs=pl.BlockSpec((None, n_mem, n_cols), lambda b: (b, 0, 0)),
        compiler_params=_params("parallel"),
        name="memkv",
    )(mem, mem_norm_g.reshape(1, d), w_stack)


LOCAL_TM = 512


def _local_kernel(bu_ref, bv_ref, db_ref, dc_ref, dh_ref, hc_ref, hh_ref, xq_ref,
                  ws_ref, bs_ref, lng_ref, lnb_ref, cw_ref, kv_ref, o_ref, zbuf):
    tm = LOCAL_TM
    gw = GROUP_WIDTH

    u = jax.nn.gelu(bu_ref[...].astype(F32))
    v = jax.nn.gelu(bv_ref[...].astype(F32))
    mu = jnp.mean(v, axis=-1, keepdims=True)
    vc = v - mu
    var = jnp.mean(vc * vc, axis=-1, keepdims=True)
    vnorm = (vc * lax.rsqrt(var + NORM_EPS) * lng_ref[...] + lnb_ref[...]).astype(BF16)
    r = lax.broadcasted_iota(jnp.int32, (SGU_CHUNK, SGU_CHUNK), 0)
    c = lax.broadcasted_iota(jnp.int32, (SGU_CHUNK, SGU_CHUNK), 1)
    for hd in range(GROUP_HEADS):
        cols = slice(hd * HEAD_DIM, (hd + 1) * HEAD_DIM)
        w_causal = jnp.where(c <= r, ws_ref[hd], 0.0).astype(BF16)
        bias = bs_ref[:, hd:hd + 1]
        for ch in range(tm // SGU_CHUNK):
            rows = slice(ch * SGU_CHUNK, (ch + 1) * SGU_CHUNK)
            mixed = jnp.dot(w_causal, vnorm[rows, cols], preferred_element_type=F32) + bias
            o_ref[rows, cols] = (u[rows, cols] * mixed).astype(o_ref.dtype)

    first_tile = pl.program_id(1) == 0
    zbuf[0:CONV_HALO, :] = jnp.where(first_tile, 0.0, hc_ref[...].astype(F32) * hh_ref[...].astype(F32))
    zbuf[CONV_HALO:CONV_HALO + tm, :] = dc_ref[...].astype(F32) * dh_ref[...].astype(F32)
    y = cw_ref[CONV_WIDTH - 1:CONV_WIDTH, :] * zbuf[CONV_HALO:CONV_HALO + tm, :]
    for tap in range(CONV_WIDTH - 1):
        back = CONV_WIDTH - 1 - tap
        y = y + cw_ref[tap:tap + 1, :] * zbuf[CONV_HALO - back:CONV_HALO - back + tm, :]
    o_ref[:, gw:2 * gw] = (db_ref[...].astype(F32) * y).astype(o_ref.dtype)

    for hd in range(GROUP_HEADS):
        cols = slice(hd * HEAD_DIM, (hd + 1) * HEAD_DIM)
        kh = kv_ref[:, hd * HEAD_DIM:(hd + 1) * HEAD_DIM]
        vh = kv_ref[:, gw + hd * HEAD_DIM:gw + (hd + 1) * HEAD_DIM]
        s = _dot_nt(xq_ref[:, cols], kh) * ATTN_SCALE
        m = jnp.max(s, axis=-1, keepdims=True)
        p = jnp.exp(s - m)
        l = jnp.sum(p, axis=-1, keepdims=True)
        o = jnp.dot(p.astype(BF16), vh, preferred_element_type=F32) / l
        o_ref[:, 2 * gw + hd * HEAD_DIM:2 * gw + (hd + 1) * HEAD_DIM] = o.astype(o_ref.dtype)


def _local(proj, memkv, sgu_w, sgu_b_t, ln_g, ln_b, conv_w, batch, seq):
    tm = LOCAL_TM
    gw = GROUP_WIDTH
    halo_per_tile = tm // CONV_HALO
    tile = lambda sl: pl.BlockSpec((None, tm, gw), lambda b, i: (b, i, _bf16_slice(sl)))
    halo = lambda sl: pl.BlockSpec(
        (None, CONV_HALO, gw), lambda b, i: (b, jnp.maximum(i * halo_per_tile - 1, 0), _bf16_slice(sl)))
    whole = lambda a: pl.BlockSpec(a.shape, lambda b, i: (0,) * a.ndim)
    n_mem = memkv.shape[1]
    return pl.pallas_call(
        _local_kernel,
        out_shape=jax.ShapeDtypeStruct((batch, seq, 3 * gw), BF16),
        grid=(batch, seq // tm),
        in_specs=[
            tile(SL_B_U), tile(SL_B_V), tile(SL_D_B), tile(SL_D_C), tile(SL_D_H),
            halo(SL_D_C), halo(SL_D_H), tile(SL_X_Q),
            whole(sgu_w), whole(sgu_b_t), whole(ln_g), whole(ln_b), whole(conv_w),
            pl.BlockSpec((None, n_mem, 2 * gw), lambda b, i: (b, 0, 0)),
        ],
        out_specs=pl.BlockSpec((None, tm, 3 * gw), lambda b, i: (b, i, 0)),
        scratch_shapes=[pltpu.VMEM((CONV_HALO + tm, gw), F32)],
        compiler_params=_params("parallel", "parallel"),
        name="local",
    )(proj, proj, proj, proj, proj, proj, proj, proj,
      sgu_w, sgu_b_t, ln_g, ln_b, conv_w, memkv)


OUTPROJ_TM = 256


def _outproj_kernel(ya_ref, yl_ref, yc_ref, ga_ref, gb_ref, gc_ref, gd_ref, gx_ref,
                    ng_ref, w_ref, x_ref, pg_ref, *outs, final):
    gw = GROUP_WIDTH
    branches = (
        (lambda: ya_ref[...], ga_ref),
        (lambda: yl_ref[:, 0:gw], gb_ref),
        (lambda: yc_ref[...], gc_ref),
        (lambda: yl_ref[:, gw:2 * gw], gd_ref),
        (lambda: yl_ref[:, 2 * gw:3 * gw], gx_ref),
    )
    acc = x_ref[...]
    for grp, (o, gate_ref) in enumerate(branches):
        rows = slice(grp * gw, (grp + 1) * gw)
        y = o().astype(F32) * gate_ref[...].astype(F32)
        y = y * lax.rsqrt(jnp.mean(y * y, axis=-1, keepdims=True) + NORM_EPS)
        y = (y * ng_ref[:, rows]).astype(BF16)
        acc = acc + jnp.dot(y, w_ref[rows, :], preferred_element_type=F32)
    normed = acc * lax.rsqrt(jnp.mean(acc * acc, axis=-1, keepdims=True) + NORM_EPS) * pg_ref[...]
    if final:
        (o_ref,) = outs
        o_ref[...] = normed
    else:
        o_ref, h_ref = outs
        o_ref[...] = acc
        h_ref[...] = normed.astype(h_ref.dtype)


def _outproj(ya, yl, yc, proj_b, out_norm_g, w_bf16, x2d, post_g, final):
    m, d = x2d.shape
    tm = OUTPROJ_TM
    gw = GROUP_WIDTH
    gate = lambda sl: pl.BlockSpec((tm, gw), lambda i: (i, _bf16_slice(sl)))
    row_tile = pl.BlockSpec((tm, d), lambda i: (i, 0))
    x_out = jax.ShapeDtypeStruct((m, d), F32)
    return pl.pallas_call(
        functools.partial(_outproj_kernel, final=final),
        out_shape=x_out if final else (x_out, jax.ShapeDtypeStruct((m, d), BF16)),
        grid=(m // tm,),
        in_specs=[
            pl.BlockSpec((tm, gw), lambda i: (i, 0)),
            pl.BlockSpec((tm, 3 * gw), lambda i: (i, 0)),
            pl.BlockSpec((tm, gw), lambda i: (i, 0)),
            gate(SL_A_G), gate(SL_B_G), gate(SL_C_G), gate(SL_D_G), gate(SL_X_G),
            pl.BlockSpec((1, MIX_WIDTH), lambda i: (0, 0)),
            pl.BlockSpec((MIX_WIDTH, d), lambda i: (0, 0)),
            row_tile,
            pl.BlockSpec((1, d), lambda i: (0, 0)),
        ],
        out_specs=row_tile if final else (row_tile, row_tile),
        compiler_params=_params("parallel"),
        name="outproj_final" if final else "outproj",
    )(ya, yl, yc, proj_b, proj_b, proj_b, proj_b, proj_b,
      out_norm_g.reshape(1, MIX_WIDTH), w_bf16, x2d, post_g.reshape(1, d))


def _rotary_tables(seq):
    half = HEAD_DIM // 2
    inv_freq = ROPE_THETA ** (-jnp.arange(half, dtype=F32) / half)
    ang = jnp.arange(seq, dtype=jnp.int32).astype(F32)[:, None] * inv_freq[None, :]
    cos, sin = jnp.cos(ang), jnp.sin(ang)
    return jnp.concatenate([cos, cos], axis=-1), jnp.concatenate([-sin, sin], axis=-1)


def kernel(x, mem, norm_g, w_in, sgu_w, sgu_b, sgu_ln_g, sgu_ln_b, conv_w, mem_norm_g, w_mem_kv,
           out_norm_g, w_out, final_norm_g):
    batch, seq, d = x.shape
    depth = w_in.shape[0]
    cos_t, sin_t = _rotary_tables(seq)
    x2d = x.reshape(batch * seq, d)
    h2d = _rmsnorm(x2d, norm_g[0])
    flat = lambda t: t.reshape(batch * seq, t.shape[-1])
    for i in range(depth):
        final = i == depth - 1
        proj_f, proj_b = _inproj(h2d, w_in, i, cos_t, sin_t, seq)
        proj_a = proj_f.reshape(batch, seq, proj_f.shape[-1])
        proj_r = proj_b.reshape(batch, seq, proj_b.shape[-1])
        memkv = _memkv(mem, mem_norm_g[i], w_mem_kv, i)
        ya = _dilated(proj_a, batch, seq)
        yc = _moba(proj_r, batch, seq)
        yl = _local(proj_r, memkv, sgu_w[i], sgu_b[i].T, sgu_ln_g[i].reshape(1, GROUP_WIDTH),
                    sgu_ln_b[i].reshape(1, GROUP_WIDTH), conv_w[i], batch, seq)
        out = _outproj(flat(ya), flat(yl), flat(yc), proj_b, out_norm_g[i], w_out[i].astype(BF16),
                       x2d, final_norm_g if final else norm_g[i + 1], final)
        if final:
            x2d = out
        else:
            x2d, h2d = out
    return x2d.reshape(batch, seq, d)
```

```python
import functools

import jax
import jax.numpy as jnp
from jax import lax
from jax.experimental import pallas as pl
from jax.experimental.pallas import tpu as pltpu

F32 = jnp.float32
BF16 = jnp.bfloat16

HEAD_DIM = 128
GROUP_HEADS = 4
GROUP_WIDTH = GROUP_HEADS * HEAD_DIM
N_GROUPS = 5
MIX_WIDTH = N_GROUPS * GROUP_WIDTH
IN_SLICES = 17
ROPE_THETA = 10000.0
NORM_EPS = 1e-6
NEG_INF = -1e30
ATTN_SCALE = HEAD_DIM ** -0.5
LOG2E = 1.4426950408889634
EXP2_SCALE = ATTN_SCALE * LOG2E

DIL_BAND = 128
DIL_STRIDES = (1, 4, 16)
DIL_SUPER = DIL_BAND * DIL_STRIDES[-1]
SGU_CHUNK = 128
MOBA_BLOCK = 256
MOBA_TOPK = 3
CONV_WIDTH = 3
CONV_HALO = 16

SL_A_Q, SL_A_K, SL_A_V, SL_A_G = 0, 1, 2, 3
SL_B_U, SL_B_V, SL_B_G = 4, 5, 6
SL_C_Q, SL_C_K, SL_C_V, SL_C_G = 7, 8, 9, 10
SL_D_B, SL_D_C, SL_D_H, SL_D_G = 11, 12, 13, 14
SL_X_Q, SL_X_G = 15, 16
ROTARY_SLICES = (SL_A_Q, SL_A_K, SL_C_Q, SL_C_K)
N_F32_SLICES = 3


def _bf16_slice(sl):
    assert sl >= N_F32_SLICES
    return sl - N_F32_SLICES


VMEM_LIMIT_BYTES = 56 * 1024 * 1024


def _params(*semantics):
    return pltpu.CompilerParams(dimension_semantics=semantics, vmem_limit_bytes=VMEM_LIMIT_BYTES)


def _dot_nt(a, b):
    return lax.dot_general(a, b, (((1,), (1,)), ((), ())), preferred_element_type=F32)


NORM_TM = 512


def _rmsnorm_kernel(x_ref, g_ref, o_ref):
    x = x_ref[...]
    y = x * lax.rsqrt(jnp.mean(x * x, axis=-1, keepdims=True) + NORM_EPS) * g_ref[...]
    o_ref[...] = y.astype(o_ref.dtype)


def _rmsnorm(x2d, g):
    m, d = x2d.shape
    return pl.pallas_call(
        _rmsnorm_kernel,
        out_shape=jax.ShapeDtypeStruct((m, d), BF16),
        grid=(m // NORM_TM,),
        in_specs=[pl.BlockSpec((NORM_TM, d), lambda i: (i, 0)), pl.BlockSpec((1, d), lambda i: (0, 0))],
        out_specs=pl.BlockSpec((NORM_TM, d), lambda i: (i, 0)),
        compiler_params=_params("parallel"),
        name="rmsnorm",
    )(x2d, g.reshape(1, d))


INPROJ_TM = 2048


INPROJ_ROW_CHUNKS = 8


def _inproj_kernel(h_ref, w_ref, cos_ref, sin_ref, of_ref, ob_ref):
    n = pl.program_id(1)
    chunk_rows = INPROJ_TM // INPROJ_ROW_CHUNKS

    def project(slices, finish):
        o_ref = of_ref if slices[0] < N_F32_SLICES else ob_ref
        assert all((s < N_F32_SLICES) == (slices[0] < N_F32_SLICES) for s in slices)

        @pl.when(functools.reduce(jnp.logical_or, [n == s for s in slices]))
        def _():
            w = w_ref[...].astype(BF16)
            for r in range(INPROJ_ROW_CHUNKS):
                rows = slice(r * chunk_rows, (r + 1) * chunk_rows)
                acc = jnp.dot(h_ref[rows, :], w, preferred_element_type=F32)
                if finish == "rotary":
                    cos = cos_ref[rows, :]
                    sin = sin_ref[rows, :]
                    for hh in range(GROUP_HEADS):
                        sl = slice(hh * HEAD_DIM, (hh + 1) * HEAD_DIM)
                        a = acc[:, sl]
                        o_ref[rows, sl] = (a * cos + pltpu.roll(a, HEAD_DIM // 2, 1) * sin).astype(o_ref.dtype)
                else:
                    if finish is not None:
                        acc = finish(acc)
                    o_ref[rows, :] = acc.astype(o_ref.dtype)

    project((SL_A_Q, SL_A_K), "rotary")
    project((SL_A_V,), None)
    project((SL_C_Q, SL_C_K), "rotary")
    project((SL_A_G, SL_B_G, SL_C_G, SL_D_G, SL_X_G), jax.nn.silu)
    project((SL_B_U, SL_B_V, SL_C_V, SL_D_B, SL_D_C, SL_D_H, SL_X_Q), None)


def _inproj(h2d, w_stack, layer, cos_t, sin_t, seq):
    m, d = h2d.shape
    gw = GROUP_WIDTH
    n_slices = w_stack.shape[2] // gw
    tm = INPROJ_TM
    seq_tiles = seq // tm
    return pl.pallas_call(
        _inproj_kernel,
        out_shape=(jax.ShapeDtypeStruct((m, N_F32_SLICES * gw), F32),
                   jax.ShapeDtypeStruct((m, (n_slices - N_F32_SLICES) * gw), BF16)),
        grid=(m // tm, n_slices),
        in_specs=[
            pl.BlockSpec((tm, d), lambda i, n: (i, 0)),
            pl.BlockSpec((None, d, gw), lambda i, n: (layer, 0, n)),
            pl.BlockSpec((tm, HEAD_DIM), lambda i, n: (i % seq_tiles, 0)),
            pl.BlockSpec((tm, HEAD_DIM), lambda i, n: (i % seq_tiles, 0)),
        ],
        out_specs=(pl.BlockSpec((tm, gw), lambda i, n: (i, jnp.minimum(n, N_F32_SLICES - 1))),
                   pl.BlockSpec((tm, gw), lambda i, n: (i, jnp.maximum(n - N_F32_SLICES, 0)))),
        compiler_params=_params("parallel", "arbitrary"),
        name="inproj",
    )(h2d, w_stack, cos_t, sin_t)


def _dilated_kernel(q_ref, kp_ref, ko_ref, vp_ref, vo_ref, o_ref,
                    q4, k4, v4, o1, o2, o3, l1, l2, l3):
    sup = DIL_SUPER
    band = DIL_BAND
    mid = DIL_STRIDES[1]
    assert DIL_STRIDES == (1, mid, mid * mid)
    per = sup // mid
    has_prev = pl.program_id(1) > 0

    for r in range(mid):
        q4[r * per:(r + 1) * per, :] = q_ref[pl.ds(r, per, stride=mid), :]
        for dst, prev_ref, own_ref in ((k4, kp_ref, ko_ref), (v4, vp_ref, vo_ref)):
            dst[2 * r * per:(2 * r + 1) * per, :] = prev_ref[pl.ds(r, per, stride=mid), :]
            dst[(2 * r + 1) * per:(2 * r + 2) * per, :] = own_ref[pl.ds(r, per, stride=mid), :]

    qi = lax.broadcasted_iota(jnp.int32, (band, 2 * band), 0)
    kj = lax.broadcasted_iota(jnp.int32, (band, 2 * band), 1)
    in_band = (kj >= qi) & (kj <= qi + band)
    bias_any = jnp.where(in_band, 0.0, NEG_INF).astype(F32)
    bias_start = jnp.where(in_band & (kj >= band), 0.0, NEG_INF).astype(F32)
    at_start = jnp.logical_not(has_prev)
    ones = jnp.ones((2 * band, HEAD_DIM), BF16)

    def attend(qb, kb, vb, first_band):
        bias = jnp.where(at_start, bias_start, bias_any) if first_band else bias_any
        t = _dot_nt(qb.astype(BF16), kb.astype(BF16)) * EXP2_SCALE + bias
        m = jnp.max(t, axis=-1, keepdims=True)
        p = jnp.exp2(t - m)
        va = jnp.concatenate([vb.astype(BF16), ones], axis=1)
        r = jnp.dot(p.astype(BF16), va, preferred_element_type=F32)
        l = r[:, HEAD_DIM:]
        return r[:, :HEAD_DIM] / l, m + jnp.log(l) * LOG2E

    for n in range(sup // band):
        rows = slice(n * band, (n + 1) * band)
        if n == 0:
            kb = jnp.concatenate([kp_ref[sup - band:sup, :], ko_ref[0:band, :]], axis=0)
            vb = jnp.concatenate([vp_ref[sup - band:sup, :], vo_ref[0:band, :]], axis=0)
        else:
            kb = ko_ref[(n - 1) * band:(n + 1) * band, :]
            vb = vo_ref[(n - 1) * band:(n + 1) * band, :]
        o1[rows, :], l1[rows, :] = attend(q_ref[rows, :], kb, vb, n == 0)

    for r in range(mid):
        for n in range(per // band):
            rows = slice(r * per + n * band, r * per + (n + 1) * band)
            keys = slice((2 * r + 1) * per + (n - 1) * band, (2 * r + 1) * per + (n + 1) * band)
            o2[rows, :], l2[rows, :] = attend(q4[rows, :], k4[keys, :], v4[keys, :], n == 0)

    for r in range(mid):
        for a in range(mid):
            rows = pl.ds(r * per + a, band, stride=mid)
            keys = pl.ds(2 * r * per + a, 2 * band, stride=mid)
            o3[rows, :], l3[rows, :] = attend(q4[rows, :], k4[keys, :], v4[keys, :], True)

    rows_per_step = 256

    def merge(i, c):
        r = i // (per // rows_per_step)
        l0 = (i % (per // rows_per_step)) * rows_per_step
        rows = pl.ds(pl.multiple_of(r * per + l0, rows_per_step), rows_per_step)
        positions = pl.ds(l0 * mid + r, rows_per_step, stride=mid)
        a1, a2, a3 = l1[positions, :], l2[rows, :], l3[rows, :]
        mx = jnp.maximum(jnp.maximum(a1, a2), a3)
        w1, w2, w3 = jnp.exp2(a1 - mx), jnp.exp2(a2 - mx), jnp.exp2(a3 - mx)
        num = w1 * o1[positions, :] + w2 * o2[rows, :] + w3 * o3[rows, :]
        o_ref[positions, :] = num / (w1 + w2 + w3)
        return c
    lax.fori_loop(0, sup // rows_per_step, merge, 0)


def _dilated(proj, batch, seq):
    sup = DIL_SUPER
    blk = (None, sup, HEAD_DIM)
    prev = lambda sb: jnp.maximum(sb - 1, 0)
    scr = pltpu.VMEM((sup, HEAD_DIM), F32)
    win = pltpu.VMEM((2 * sup, HEAD_DIM), F32)
    return pl.pallas_call(
        _dilated_kernel,
        out_shape=jax.ShapeDtypeStruct((batch, seq, GROUP_WIDTH), F32),
        grid=(batch, seq // sup, GROUP_HEADS),
        in_specs=[
            pl.BlockSpec(blk, lambda b, sb, h: (b, sb, SL_A_Q * GROUP_HEADS + h)),
            pl.BlockSpec(blk, lambda b, sb, h: (b, prev(sb), SL_A_K * GROUP_HEADS + h)),
            pl.BlockSpec(blk, lambda b, sb, h: (b, sb, SL_A_K * GROUP_HEADS + h)),
            pl.BlockSpec(blk, lambda b, sb, h: (b, prev(sb), SL_A_V * GROUP_HEADS + h)),
            pl.BlockSpec(blk, lambda b, sb, h: (b, sb, SL_A_V * GROUP_HEADS + h)),
        ],
        out_specs=pl.BlockSpec(blk, lambda b, sb, h: (b, sb, h)),
        scratch_shapes=[scr, win, win, scr, scr, scr, scr, scr, scr],
        compiler_params=_params("parallel", "parallel", "parallel"),
        name="dilated",
    )(proj, proj, proj, proj, proj)


MOBA_TQ = 2 * MOBA_BLOCK
MOBA_BLOCK_SHIFT = 8
assert 1 << MOBA_BLOCK_SHIFT == MOBA_BLOCK
MOBA_PAST_STEPS_PER_TRIP = 4


def _moba_kernel(q_ref, k_ref, v_ref, o_ref, kaug, vaug, kmean, qaug, m_all, acc_all, s_a, s_b):
    blk = MOBA_BLOCK
    tq = MOBA_TQ
    hd = HEAD_DIM
    seq = k_ref.shape[0]
    n_blocks = seq // blk
    n_tiles = seq // tq

    lane = lax.broadcasted_iota(jnp.int32, (blk, hd), 1)
    ones = jnp.ones((blk, hd), BF16)
    for n in range(n_blocks):
        rows = slice(n * blk, (n + 1) * blk)
        kb = k_ref[rows, :]
        kmean[n:n + 1, :] = jnp.mean(kb.astype(F32), axis=0, keepdims=True)
        kaug[rows, 0:hd] = kb
        kaug[rows, hd:2 * hd] = jnp.where(lane == n, 1.0, 0.0).astype(BF16)
        vaug[rows, 0:hd] = v_ref[rows, :]
        vaug[rows, hd:2 * hd] = ones

    def tile_rows(t):
        return pl.ds(pl.multiple_of(t * tq, tq), tq)

    km = kmean[...]
    km_hi = km.astype(BF16)
    rest = km - km_hi.astype(F32)
    km_mid = rest.astype(BF16)
    km_lo = (rest - km_mid.astype(F32)).astype(BF16)
    parts = _dot_nt(jnp.concatenate([km_hi, km_mid, km_lo], axis=0), q_ref[...])
    gate = parts[0:n_blocks] + parts[n_blocks:2 * n_blocks] + parts[2 * n_blocks:3 * n_blocks]
    bidx = lax.broadcasted_iota(jnp.int32, (n_blocks, seq), 0)
    bidx_f = bidx.astype(F32)
    qblk = lax.shift_right_logical(lax.broadcasted_iota(jnp.int32, (n_blocks, seq), 1), MOBA_BLOCK_SHIFT)
    g = jnp.where(bidx < qblk, gate, NEG_INF)
    sel_bias = jnp.full((n_blocks, seq), NEG_INF, F32)
    for k in range(MOBA_TOPK):
        mx = jnp.max(g, axis=0, keepdims=True)
        first = jnp.min(jnp.where(g == mx, bidx_f, float(n_blocks)), axis=0, keepdims=True)
        hit = bidx_f == first
        sel_bias = jnp.where(hit, jnp.where(k < qblk, 0.0, NEG_INF), sel_bias)
        g = jnp.where(hit, -jnp.inf, g)
    sel_bias = jnp.where(bidx == qblk, 0.0, sel_bias)
    sel_bias = jnp.concatenate([sel_bias, jnp.zeros((hd - n_blocks, seq), F32)], axis=0)
    qaug[:, 0:hd] = q_ref[...]
    for t in range(n_tiles):
        rows = slice(t * tq, (t + 1) * tq)
        qaug[rows, hd:2 * hd] = sel_bias[:, rows].T.astype(BF16)

    def key_rows(c, width):
        return pl.ds(pl.multiple_of(c * (width * tq), width * tq), width * tq)

    def scores(t, c, width):
        return _dot_nt(qaug[tile_rows(t), :], kaug[key_rows(c, width), :])

    def run_steps(first, advance, n_steps, step, per_trip, width):
        assert n_steps % per_trip == 0 and per_trip % 2 == 0
        cols = slice(0, width * tq)
        bufs = (s_a, s_b)
        s_a[:, cols] = scores(*first, width)

        def trip(j, tc):
            for i in range(per_trip):
                nxt = advance(*tc)
                bufs[(i + 1) % 2][:, cols] = scores(jnp.minimum(nxt[0], n_tiles - 1),
                                                    jnp.minimum(nxt[1], n_tiles // width - 1), width)
                step(bufs[i % 2].at[:, cols], *tc, width)
                tc = nxt
            return tc
        lax.fori_loop(0, n_steps // per_trip, trip, tuple(jnp.int32(v) for v in first))

    def diagonal_step(s_ref, t, c, width):
        r_i = lax.broadcasted_iota(jnp.int32, (tq, width * tq), 0)
        c_i = lax.broadcasted_iota(jnp.int32, (tq, width * tq), 1)
        s = jnp.where(c_i <= r_i + (width - 1) * tq, s_ref[...], NEG_INF)
        m = jnp.max(s, axis=-1, keepdims=True)
        p = jnp.exp2((s - m) * EXP2_SCALE)
        m_all[tile_rows(t), :] = m
        acc_all[tile_rows(t), :] = jnp.dot(p.astype(BF16), vaug[key_rows(c, width), :],
                                           preferred_element_type=F32)

    def past_step(s_ref, t, c, width):
        s = s_ref[...]
        m_old = m_all[tile_rows(t), :]
        m_new = jnp.maximum(m_old, jnp.max(s, axis=-1, keepdims=True))
        alpha = jnp.exp2((m_old - m_new) * EXP2_SCALE)
        p = jnp.exp2((s - m_new) * EXP2_SCALE)
        m_all[tile_rows(t), :] = m_new
        acc_all[tile_rows(t), :] = alpha * acc_all[tile_rows(t), :] + jnp.dot(
            p.astype(BF16), vaug[key_rows(c, width), :], preferred_element_type=F32)

    def next_past(t, c):
        wrap = c + 1 == lax.shift_right_logical(t, 1)
        return jnp.where(wrap, t + 1, t), jnp.where(wrap, 0, c + 1)

    half = n_tiles // 2
    run_steps((0, 0), lambda t, c: (t + 2, c + 2), half, diagonal_step, half, 1)
    run_steps((1, 0), lambda t, c: (t + 2, c + 1), half, diagonal_step, half, 2)
    run_steps((2, 0), next_past, sum(t // 2 for t in range(n_tiles)), past_step, MOBA_PAST_STEPS_PER_TRIP, 2)

    def finish(t, carry):
        acc = acc_all[tile_rows(t), :]
        o_ref[tile_rows(t), :] = (acc[:, 0:hd] / acc[:, hd:2 * hd]).astype(o_ref.dtype)
        return carry
    lax.fori_loop(0, n_tiles, finish, 0)


def _moba(proj, batch, seq):
    tq = MOBA_TQ
    hd = HEAD_DIM
    full = lambda sl: pl.BlockSpec((None, seq, hd), lambda b, h: (b, 0, _bf16_slice(sl) * GROUP_HEADS + h))
    return pl.pallas_call(
        _moba_kernel,
        out_shape=jax.ShapeDtypeStruct((batch, seq, GROUP_WIDTH), BF16),
        grid=(batch, GROUP_HEADS),
        in_specs=[full(SL_C_Q), full(SL_C_K), full(SL_C_V)],
        out_specs=pl.BlockSpec((None, seq, hd), lambda b, h: (b, 0, h)),
        scratch_shapes=[
            pltpu.VMEM((seq, 2 * hd), BF16),
            pltpu.VMEM((seq, 2 * hd), BF16),
            pltpu.VMEM((seq // MOBA_BLOCK, hd), F32),
            pltpu.VMEM((seq, 2 * hd), BF16),
            pltpu.VMEM((seq, 1), F32),
            pltpu.VMEM((seq, 2 * hd), F32),
            pltpu.VMEM((tq, 2 * tq), F32),
            pltpu.VMEM((tq, 2 * tq), F32),
        ],
        compiler_params=_params("parallel", "parallel"),
        name="moba",
    )(proj, proj, proj)


def _memkv_kernel(mem_ref, g_ref, w_ref, o_ref):
    x = mem_ref[...]
    y = x * lax.rsqrt(jnp.mean(x * x, axis=-1, keepdims=True) + NORM_EPS) * g_ref[...]
    o_ref[...] = jnp.dot(y.astype(BF16), w_ref[...].astype(BF16),
                         preferred_element_type=F32).astype(o_ref.dtype)


def _memkv(mem, mem_norm_g, w_stack, layer):
    batch, n_mem, d = mem.shape
    n_cols = w_stack.shape[2]
    return pl.pallas_call(
        _memkv_kernel,
        out_shape=jax.ShapeDtypeStruct((batch, n_mem, n_cols), BF16),
        grid=(batch,),
        in_specs=[
            pl.BlockSpec((None, n_mem, d), lambda b: (b, 0, 0)),
            pl.BlockSpec((1, d), lambda b: (0, 0)),
            pl.BlockSpec((None, d, n_cols), lambda b: (layer, 0, 0)),
        ],
        out_specs=pl.BlockSpec((None, n_mem, n_cols), lambda b: (b, 0, 0)),
        compiler_params=_params("parallel"),
        name="memkv",
    )(mem, mem_norm_g.reshape(1, d), w_stack)


LOCAL_TM = 512


def _local_kernel(bu_ref, bv_ref, db_ref, dc_ref, dh_ref, hc_ref, hh_ref, xq_ref,
                  ws_ref, bs_ref, lng_ref, lnb_ref, cw_ref, kv_ref, o_ref, zbuf):
    tm = LOCAL_TM
    gw = GROUP_WIDTH

    u = jax.nn.gelu(bu_ref[...].astype(F32))
    v = jax.nn.gelu(bv_ref[...].astype(F32))
    mu = jnp.mean(v, axis=-1, keepdims=True)
    vc = v - mu
    var = jnp.mean(vc * vc, axis=-1, keepdims=True)
    vnorm = (vc * lax.rsqrt(var + NORM_EPS) * lng_ref[...] + lnb_ref[...]).astype(BF16)
    r = lax.broadcasted_iota(jnp.int32, (SGU_CHUNK, SGU_CHUNK), 0)
    c = lax.broadcasted_iota(jnp.int32, (SGU_CHUNK, SGU_CHUNK), 1)
    for hd in range(GROUP_HEADS):
        cols = slice(hd * HEAD_DIM, (hd + 1) * HEAD_DIM)
        w_causal = jnp.where(c <= r, ws_ref[hd], 0.0).astype(BF16)
        bias = bs_ref[:, hd:hd + 1]
        for ch in range(tm // SGU_CHUNK):
            rows = slice(ch * SGU_CHUNK, (ch + 1) * SGU_CHUNK)
            mixed = jnp.dot(w_causal, vnorm[rows, cols], preferred_element_type=F32) + bias
            o_ref[rows, cols] = (u[rows, cols] * mixed).astype(o_ref.dtype)

    first_tile = pl.program_id(1) == 0
    zbuf[0:CONV_HALO, :] = jnp.where(first_tile, 0.0, hc_ref[...].astype(F32) * hh_ref[...].astype(F32))
    zbuf[CONV_HALO:CONV_HALO + tm, :] = dc_ref[...].astype(F32) * dh_ref[...].astype(F32)
    y = cw_ref[CONV_WIDTH - 1:CONV_WIDTH, :] * zbuf[CONV_HALO:CONV_HALO + tm, :]
    for tap in range(CONV_WIDTH - 1):
        back = CONV_WIDTH - 1 - tap
        y = y + cw_ref[tap:tap + 1, :] * zbuf[CONV_HALO - back:CONV_HALO - back + tm, :]
    o_ref[:, gw:2 * gw] = (db_ref[...].astype(F32) * y).astype(o_ref.dtype)

    for hd in range(GROUP_HEADS):
        cols = slice(hd * HEAD_DIM, (hd + 1) * HEAD_DIM)
        kh = kv_ref[:, hd * HEAD_DIM:(hd + 1) * HEAD_DIM]
        vh = kv_ref[:, gw + hd * HEAD_DIM:gw + (hd + 1) * HEAD_DIM]
        s = _dot_nt(xq_ref[:, cols], kh) * ATTN_SCALE
        m = jnp.max(s, axis=-1, keepdims=True)
        p = jnp.exp(s - m)
        l = jnp.sum(p, axis=-1, keepdims=True)
        o = jnp.dot(p.astype(BF16), vh, preferred_element_type=F32) / l
        o_ref[:, 2 * gw + hd * HEAD_DIM:2 * gw + (hd + 1) * HEAD_DIM] = o.astype(o_ref.dtype)


def _local(proj, memkv, sgu_w, sgu_b_t, ln_g, ln_b, conv_w, batch, seq):
    tm = LOCAL_TM
    gw = GROUP_WIDTH
    halo_per_tile = tm // CONV_HALO
    tile = lambda sl: pl.BlockSpec((None, tm, gw), lambda b, i: (b, i, _bf16_slice(sl)))
    halo = lambda sl: pl.BlockSpec(
        (None, CONV_HALO, gw), lambda b, i: (b, jnp.maximum(i * halo_per_tile - 1, 0), _bf16_slice(sl)))
    whole = lambda a: pl.BlockSpec(a.shape, lambda b, i: (0,) * a.ndim)
    n_mem = memkv.shape[1]
    return pl.pallas_call(
        _local_kernel,
        out_shape=jax.ShapeDtypeStruct((batch, seq, 3 * gw), BF16),
        grid=(batch, seq // tm),
        in_specs=[
            tile(SL_B_U), tile(SL_B_V), tile(SL_D_B), tile(SL_D_C), tile(SL_D_H),
            halo(SL_D_C), halo(SL_D_H), tile(SL_X_Q),
            whole(sgu_w), whole(sgu_b_t), whole(ln_g), whole(ln_b), whole(conv_w),
            pl.BlockSpec((None, n_mem, 2 * gw), lambda b, i: (b, 0, 0)),
        ],
        out_specs=pl.BlockSpec((None, tm, 3 * gw), lambda b, i: (b, i, 0)),
        scratch_shapes=[pltpu.VMEM((CONV_HALO + tm, gw), F32)],
        compiler_params=_params("parallel", "parallel"),
        name="local",
    )(proj, proj, proj, proj, proj, proj, proj, proj,
      sgu_w, sgu_b_t, ln_g, ln_b, conv_w, memkv)


OUTPROJ_TM = 256


def _outproj_kernel(ya_ref, yl_ref, yc_ref, ga_ref, gb_ref, gc_ref, gd_ref, gx_ref,
                    ng_ref, w_ref, x_ref, pg_ref, *outs, final):
    gw = GROUP_WIDTH
    branches = (
        (lambda: ya_ref[...], ga_ref),
        (lambda: yl_ref[:, 0:gw], gb_ref),
        (lambda: yc_ref[...], gc_ref),
        (lambda: yl_ref[:, gw:2 * gw], gd_ref),
        (lambda: yl_ref[:, 2 * gw:3 * gw], gx_ref),
    )
    acc = x_ref[...]
    for grp, (o, gate_ref) in enumerate(branches):
        rows = slice(grp * gw, (grp + 1) * gw)
        y = o().astype(F32) * gate_ref[...].astype(F32)
        y = y * lax.rsqrt(jnp.mean(y * y, axis=-1, keepdims=True) + NORM_EPS)
        y = (y * ng_ref[:, rows]).astype(BF16)
        acc = acc + jnp.dot(y, w_ref[rows, :], preferred_element_type=F32)
    normed = acc * lax.rsqrt(jnp.mean(acc * acc, axis=-1, keepdims=True) + NORM_EPS) * pg_ref[...]
    if final:
        (o_ref,) = outs
        o_ref[...] = normed
    else:
        o_ref, h_ref = outs
        o_ref[...] = acc
        h_ref[...] = normed.astype(h_ref.dtype)


def _outproj(ya, yl, yc, proj_b, out_norm_g, w_bf16, x2d, post_g, final):
    m, d = x2d.shape
    tm = OUTPROJ_TM
    gw = GROUP_WIDTH
    gate = lambda sl: pl.BlockSpec((tm, gw), lambda i: (i, _bf16_slice(sl)))
    row_tile = pl.BlockSpec((tm, d), lambda i: (i, 0))
    x_out = jax.ShapeDtypeStruct((m, d), F32)
    return pl.pallas_call(
        functools.partial(_outproj_kernel, final=final),
        out_shape=x_out if final else (x_out, jax.ShapeDtypeStruct((m, d), BF16)),
        grid=(m // tm,),
        in_specs=[
            pl.BlockSpec((tm, gw), lambda i: (i, 0)),
            pl.BlockSpec((tm, 3 * gw), lambda i: (i, 0)),
            pl.BlockSpec((tm, gw), lambda i: (i, 0)),
            gate(SL_A_G), gate(SL_B_G), gate(SL_C_G), gate(SL_D_G), gate(SL_X_G),
            pl.BlockSpec((1, MIX_WIDTH), lambda i: (0, 0)),
            pl.BlockSpec((MIX_WIDTH, d), lambda i: (0, 0)),
            row_tile,
            pl.BlockSpec((1, d), lambda i: (0, 0)),
        ],
        out_specs=row_tile if final else (row_tile, row_tile),
        compiler_params=_params("parallel"),
        name="outproj_final" if final else "outproj",
    )(ya, yl, yc, proj_b, proj_b, proj_b, proj_b, proj_b,
      out_norm_g.reshape(1, MIX_WIDTH), w_bf16, x2d, post_g.reshape(1, d))


def _rotary_tables(seq):
    half = HEAD_DIM // 2
    inv_freq = ROPE_THETA ** (-jnp.arange(half, dtype=F32) / half)
    ang = jnp.arange(seq, dtype=jnp.int32).astype(F32)[:, None] * inv_freq[None, :]
    cos, sin = jnp.cos(ang), jnp.sin(ang)
    return jnp.concatenate([cos, cos], axis=-1), jnp.concatenate([-sin, sin], axis=-1)


def kernel(x, mem, norm_g, w_in, sgu_w, sgu_b, sgu_ln_g, sgu_ln_b, conv_w, mem_norm_g, w_mem_kv,
           out_norm_g, w_out, final_norm_g):
    batch, seq, d = x.shape
    depth = w_in.shape[0]
    cos_t, sin_t = _rotary_tables(seq)
    x2d = x.reshape(batch * seq, d)
    h2d = _rmsnorm(x2d, norm_g[0])
    flat = lambda t: t.reshape(batch * seq, t.shape[-1])
    for i in range(depth):
        final = i == depth - 1
        proj_f, proj_b = _inproj(h2d, w_in, i, cos_t, sin_t, seq)
        proj_a = proj_f.reshape(batch, seq, proj_f.shape[-1])
        proj_r = proj_b.reshape(batch, seq, proj_b.shape[-1])
        memkv = _memkv(mem, mem_norm_g[i], w_mem_kv, i)
        ya = _dilated(proj_a, batch, seq)
        yc = _moba(proj_r, batch, seq)
        yl = _local(proj_r, memkv, sgu_w[i], sgu_b[i].T, sgu_ln_g[i].reshape(1, GROUP_WIDTH),
                    sgu_ln_b[i].reshape(1, GROUP_WIDTH), conv_w[i], batch, seq)
        out = _outproj(flat(ya), flat(yl), flat(yc), proj_b, out_norm_g[i], w_out[i].astype(BF16),
                       x2d, final_norm_g if final else norm_g[i + 1], final)
        if final:
            x2d = out
        else:
            x2d, h2d = out
    return x2d.reshape(batch, seq, d)
```

```python
import functools

import jax
import jax.numpy as jnp
import numpy as np
from jax import lax
from jax.experimental import pallas as pl
from jax.experimental.pallas import tpu as pltpu

F32 = jnp.float32
BF16 = jnp.bfloat16

HEAD_DIM = 128
GROUP_HEADS = 4
GROUP_WIDTH = GROUP_HEADS * HEAD_DIM
N_GROUPS = 5
MIX_WIDTH = N_GROUPS * GROUP_WIDTH
IN_SLICES = 17
ROPE_THETA = 10000.0
NORM_EPS = 1e-6
NEG_INF = -1e30
ATTN_SCALE = HEAD_DIM ** -0.5
LOG2E = 1.4426950408889634
EXP2_SCALE = ATTN_SCALE * LOG2E

DIL_BAND = 128
DIL_STRIDES = (1, 4, 16)
DIL_SUPER = DIL_BAND * DIL_STRIDES[-1]
SGU_CHUNK = 128
MOBA_BLOCK = 256
MOBA_TOPK = 3
CONV_WIDTH = 3
CONV_HALO = 16

SL_A_Q, SL_A_K, SL_A_V, SL_A_G = 0, 1, 2, 3
SL_B_U, SL_B_V, SL_B_G = 4, 5, 6
SL_C_Q, SL_C_K, SL_C_V, SL_C_G = 7, 8, 9, 10
SL_D_B, SL_D_C, SL_D_H, SL_D_G = 11, 12, 13, 14
SL_X_Q, SL_X_G = 15, 16
ROTARY_SLICES = (SL_A_Q, SL_A_K, SL_C_Q, SL_C_K)
N_F32_SLICES = 3


def _bf16_slice(sl):
    assert sl >= N_F32_SLICES
    return sl - N_F32_SLICES


VMEM_LIMIT_BYTES = 56 * 1024 * 1024


def _params(*semantics):
    return pltpu.CompilerParams(dimension_semantics=semantics, vmem_limit_bytes=VMEM_LIMIT_BYTES)


def _dot_nt(a, b):
    return lax.dot_general(a, b, (((1,), (1,)), ((), ())), preferred_element_type=F32)


NORM_TM = 512


def _rmsnorm_kernel(x_ref, g_ref, o_ref):
    x = x_ref[...]
    y = x * lax.rsqrt(jnp.mean(x * x, axis=-1, keepdims=True) + NORM_EPS) * g_ref[...]
    o_ref[...] = y.astype(o_ref.dtype)


def _rmsnorm(x2d, g):
    m, d = x2d.shape
    return pl.pallas_call(
        _rmsnorm_kernel,
        out_shape=jax.ShapeDtypeStruct((m, d), BF16),
        grid=(m // NORM_TM,),
        in_specs=[pl.BlockSpec((NORM_TM, d), lambda i: (i, 0)), pl.BlockSpec((1, d), lambda i: (0, 0))],
        out_specs=pl.BlockSpec((NORM_TM, d), lambda i: (i, 0)),
        compiler_params=_params("parallel"),
        name="rmsnorm",
    )(x2d, g.reshape(1, d))


INPROJ_TM = 2048


INPROJ_ROW_CHUNKS = 8


def _inproj_kernel(h_ref, w_ref, cos_ref, sin_ref, of_ref, ob_ref):
    n = pl.program_id(1)
    chunk_rows = INPROJ_TM // INPROJ_ROW_CHUNKS

    def project(slices, finish):
        o_ref = of_ref if slices[0] < N_F32_SLICES else ob_ref
        assert all((s < N_F32_SLICES) == (slices[0] < N_F32_SLICES) for s in slices)

        @pl.when(functools.reduce(jnp.logical_or, [n == s for s in slices]))
        def _():
            w = w_ref[...].astype(BF16)
            for r in range(INPROJ_ROW_CHUNKS):
                rows = slice(r * chunk_rows, (r + 1) * chunk_rows)
                acc = jnp.dot(h_ref[rows, :], w, preferred_element_type=F32)
                if finish == "rotary":
                    cos = cos_ref[rows, :]
                    sin = sin_ref[rows, :]
                    for hh in range(GROUP_HEADS):
                        sl = slice(hh * HEAD_DIM, (hh + 1) * HEAD_DIM)
                        a = acc[:, sl]
                        o_ref[rows, sl] = (a * cos + pltpu.roll(a, HEAD_DIM // 2, 1) * sin).astype(o_ref.dtype)
                else:
                    if finish is not None:
                        acc = finish(acc)
                    o_ref[rows, :] = acc.astype(o_ref.dtype)

    project((SL_A_Q, SL_A_K), "rotary")
    project((SL_A_V,), None)
    project((SL_C_Q, SL_C_K), "rotary")
    project((SL_A_G, SL_B_G, SL_C_G, SL_D_G, SL_X_G), jax.nn.silu)
    project((SL_B_U, SL_B_V, SL_C_V, SL_D_B, SL_D_C, SL_D_H, SL_X_Q), None)


def _inproj(h2d, w_stack, layer, cos_t, sin_t, seq):
    m, d = h2d.shape
    gw = GROUP_WIDTH
    n_slices = w_stack.shape[2] // gw
    tm = INPROJ_TM
    seq_tiles = seq // tm
    return pl.pallas_call(
        _inproj_kernel,
        out_shape=(jax.ShapeDtypeStruct((m, N_F32_SLICES * gw), F32),
                   jax.ShapeDtypeStruct((m, (n_slices - N_F32_SLICES) * gw), BF16)),
        grid=(m // tm, n_slices),
        in_specs=[
            pl.BlockSpec((tm, d), lambda i, n: (i, 0)),
            pl.BlockSpec((None, d, gw), lambda i, n: (layer, 0, n)),
            pl.BlockSpec((tm, HEAD_DIM), lambda i, n: (i % seq_tiles, 0)),
            pl.BlockSpec((tm, HEAD_DIM), lambda i, n: (i % seq_tiles, 0)),
        ],
        out_specs=(pl.BlockSpec((tm, gw), lambda i, n: (i, jnp.minimum(n, N_F32_SLICES - 1))),
                   pl.BlockSpec((tm, gw), lambda i, n: (i, jnp.maximum(n - N_F32_SLICES, 0)))),
        compiler_params=_params("parallel", "arbitrary"),
        name="inproj",
    )(h2d, w_stack, cos_t, sin_t)


def _dilated_kernel(q_ref, kp_ref, ko_ref, vp_ref, vo_ref, o_ref,
                    q4, k4, v4, o1, o2, o3, l1, l2, l3):
    sup = DIL_SUPER
    band = DIL_BAND
    mid = DIL_STRIDES[1]
    assert DIL_STRIDES == (1, mid, mid * mid)
    per = sup // mid
    has_prev = pl.program_id(1) > 0

    for r in range(mid):
        q4[r * per:(r + 1) * per, :] = q_ref[pl.ds(r, per, stride=mid), :]
        for dst, prev_ref, own_ref in ((k4, kp_ref, ko_ref), (v4, vp_ref, vo_ref)):
            dst[2 * r * per:(2 * r + 1) * per, :] = prev_ref[pl.ds(r, per, stride=mid), :]
            dst[(2 * r + 1) * per:(2 * r + 2) * per, :] = own_ref[pl.ds(r, per, stride=mid), :]

    qi = lax.broadcasted_iota(jnp.int32, (band, 2 * band), 0)
    kj = lax.broadcasted_iota(jnp.int32, (band, 2 * band), 1)
    in_band = (kj >= qi) & (kj <= qi + band)
    bias_any = jnp.where(in_band, 0.0, NEG_INF).astype(F32)
    bias_start = jnp.where(in_band & (kj >= band), 0.0, NEG_INF).astype(F32)
    at_start = jnp.logical_not(has_prev)
    ones = jnp.ones((2 * band, HEAD_DIM), BF16)

    def attend(qb, kb, vb, first_band):
        bias = jnp.where(at_start, bias_start, bias_any) if first_band else bias_any
        t = _dot_nt(qb.astype(BF16), kb.astype(BF16)) * EXP2_SCALE + bias
        m = jnp.max(t, axis=-1, keepdims=True)
        p = jnp.exp2(t - m)
        va = jnp.concatenate([vb.astype(BF16), ones], axis=1)
        r = jnp.dot(p.astype(BF16), va, preferred_element_type=F32)
        l = r[:, HEAD_DIM:]
        return r[:, :HEAD_DIM] / l, m + jnp.log(l) * LOG2E

    for n in range(sup // band):
        rows = slice(n * band, (n + 1) * band)
        if n == 0:
            kb = jnp.concatenate([kp_ref[sup - band:sup, :], ko_ref[0:band, :]], axis=0)
            vb = jnp.concatenate([vp_ref[sup - band:sup, :], vo_ref[0:band, :]], axis=0)
        else:
            kb = ko_ref[(n - 1) * band:(n + 1) * band, :]
            vb = vo_ref[(n - 1) * band:(n + 1) * band, :]
        o1[rows, :], l1[rows, :] = attend(q_ref[rows, :], kb, vb, n == 0)

    for r in range(mid):
        for n in range(per // band):
            rows = slice(r * per + n * band, r * per + (n + 1) * band)
            keys = slice((2 * r + 1) * per + (n - 1) * band, (2 * r + 1) * per + (n + 1) * band)
            o2[rows, :], l2[rows, :] = attend(q4[rows, :], k4[keys, :], v4[keys, :], n == 0)

    for r in range(mid):
        for a in range(mid):
            rows = pl.ds(r * per + a, band, stride=mid)
            keys = pl.ds(2 * r * per + a, 2 * band, stride=mid)
            o3[rows, :], l3[rows, :] = attend(q4[rows, :], k4[keys, :], v4[keys, :], True)

    rows_per_step = 256

    def merge(i, c):
        r = i // (per // rows_per_step)
        l0 = (i % (per // rows_per_step)) * rows_per_step
        rows = pl.ds(pl.multiple_of(r * per + l0, rows_per_step), rows_per_step)
        positions = pl.ds(l0 * mid + r, rows_per_step, stride=mid)
        a1, a2, a3 = l1[positions, :], l2[rows, :], l3[rows, :]
        mx = jnp.maximum(jnp.maximum(a1, a2), a3)
        w1, w2, w3 = jnp.exp2(a1 - mx), jnp.exp2(a2 - mx), jnp.exp2(a3 - mx)
        num = w1 * o1[positions, :] + w2 * o2[rows, :] + w3 * o3[rows, :]
        o_ref[positions, :] = num / (w1 + w2 + w3)
        return c
    lax.fori_loop(0, sup // rows_per_step, merge, 0)


def _dilated(proj, batch, seq):
    sup = DIL_SUPER
    blk = (None, sup, HEAD_DIM)
    prev = lambda sb: jnp.maximum(sb - 1, 0)
    scr = pltpu.VMEM((sup, HEAD_DIM), F32)
    win = pltpu.VMEM((2 * sup, HEAD_DIM), F32)
    return pl.pallas_call(
        _dilated_kernel,
        out_shape=jax.ShapeDtypeStruct((batch, seq, GROUP_WIDTH), F32),
        grid=(batch, seq // sup, GROUP_HEADS),
        in_specs=[
            pl.BlockSpec(blk, lambda b, sb, h: (b, sb, SL_A_Q * GROUP_HEADS + h)),
            pl.BlockSpec(blk, lambda b, sb, h: (b, prev(sb), SL_A_K * GROUP_HEADS + h)),
            pl.BlockSpec(blk, lambda b, sb, h: (b, sb, SL_A_K * GROUP_HEADS + h)),
            pl.BlockSpec(blk, lambda b, sb, h: (b, prev(sb), SL_A_V * GROUP_HEADS + h)),
            pl.BlockSpec(blk, lambda b, sb, h: (b, sb, SL_A_V * GROUP_HEADS + h)),
        ],
        out_specs=pl.BlockSpec(blk, lambda b, sb, h: (b, sb, h)),
        scratch_shapes=[scr, win, win, scr, scr, scr, scr, scr, scr],
        compiler_params=_params("parallel", "parallel", "parallel"),
        name="dilated",
    )(proj, proj, proj, proj, proj)


MOBA_TQ = 2 * MOBA_BLOCK
MOBA_BLOCK_SHIFT = 8
assert 1 << MOBA_BLOCK_SHIFT == MOBA_BLOCK
MOBA_PAST_STEPS_PER_TRIP = 4


def _moba_kernel(q_ref, k_ref, v_ref, o_ref, kaug, vaug, kmean, qaug, m_all, acc_all, s_a, s_b):
    blk = MOBA_BLOCK
    tq = MOBA_TQ
    hd = HEAD_DIM
    seq = k_ref.shape[0]
    n_blocks = seq // blk
    n_tiles = seq // tq

    lane = lax.broadcasted_iota(jnp.int32, (blk, hd), 1)
    ones = jnp.ones((blk, hd), BF16)
    for n in range(n_blocks):
        rows = slice(n * blk, (n + 1) * blk)
        kb = k_ref[rows, :]
        kmean[n:n + 1, :] = jnp.mean(kb.astype(F32), axis=0, keepdims=True)
        kaug[rows, 0:hd] = kb
        kaug[rows, hd:2 * hd] = jnp.where(lane == n, 1.0, 0.0).astype(BF16)
        vaug[rows, 0:hd] = v_ref[rows, :]
        vaug[rows, hd:2 * hd] = ones

    def tile_rows(t):
        return pl.ds(pl.multiple_of(t * tq, tq), tq)

    km = kmean[...]
    km_hi = km.astype(BF16)
    rest = km - km_hi.astype(F32)
    km_mid = rest.astype(BF16)
    km_lo = (rest - km_mid.astype(F32)).astype(BF16)
    parts = _dot_nt(jnp.concatenate([km_hi, km_mid, km_lo], axis=0), q_ref[...])
    gate = parts[0:n_blocks] + parts[n_blocks:2 * n_blocks] + parts[2 * n_blocks:3 * n_blocks]
    bidx = lax.broadcasted_iota(jnp.int32, (n_blocks, seq), 0)
    bidx_f = bidx.astype(F32)
    qblk = lax.shift_right_logical(lax.broadcasted_iota(jnp.int32, (n_blocks, seq), 1), MOBA_BLOCK_SHIFT)
    g = jnp.where(bidx < qblk, gate, NEG_INF)
    sel_bias = jnp.full((n_blocks, seq), NEG_INF, F32)
    for k in range(MOBA_TOPK):
        mx = jnp.max(g, axis=0, keepdims=True)
        first = jnp.min(jnp.where(g == mx, bidx_f, float(n_blocks)), axis=0, keepdims=True)
        hit = bidx_f == first
        sel_bias = jnp.where(hit, jnp.where(k < qblk, 0.0, NEG_INF), sel_bias)
        g = jnp.where(hit, -jnp.inf, g)
    sel_bias = jnp.where(bidx == qblk, 0.0, sel_bias)
    sel_bias = jnp.concatenate([sel_bias, jnp.zeros((hd - n_blocks, seq), F32)], axis=0)
    qaug[:, 0:hd] = q_ref[...]
    for t in range(n_tiles):
        rows = slice(t * tq, (t + 1) * tq)
        qaug[rows, hd:2 * hd] = sel_bias[:, rows].T.astype(BF16)

    def key_rows(c, width):
        return pl.ds(pl.multiple_of(c * (width * tq), width * tq), width * tq)

    def scores(t, c, width):
        return _dot_nt(qaug[tile_rows(t), :], kaug[key_rows(c, width), :])

    def run_steps(first, advance, n_steps, step, per_trip, width):
        assert n_steps % per_trip == 0 and per_trip % 2 == 0
        cols = slice(0, width * tq)
        bufs = (s_a, s_b)
        s_a[:, cols] = scores(*first, width)

        def trip(j, tc):
            for i in range(per_trip):
                nxt = advance(*tc)
                bufs[(i + 1) % 2][:, cols] = scores(jnp.minimum(nxt[0], n_tiles - 1),
                                                    jnp.minimum(nxt[1], n_tiles // width - 1), width)
                step(bufs[i % 2].at[:, cols], *tc, width)
                tc = nxt
            return tc
        lax.fori_loop(0, n_steps // per_trip, trip, tuple(jnp.int32(v) for v in first))

    def diagonal_step(s_ref, t, c, width):
        r_i = lax.broadcasted_iota(jnp.int32, (tq, width * tq), 0)
        c_i = lax.broadcasted_iota(jnp.int32, (tq, width * tq), 1)
        s = jnp.where(c_i <= r_i + (width - 1) * tq, s_ref[...], NEG_INF)
        m = jnp.max(s, axis=-1, keepdims=True)
        p = jnp.exp2((s - m) * EXP2_SCALE)
        m_all[tile_rows(t), :] = m
        acc_all[tile_rows(t), :] = jnp.dot(p.astype(BF16), vaug[key_rows(c, width), :],
                                           preferred_element_type=F32)

    def past_step(s_ref, t, c, width):
        s = s_ref[...]
        m_old = m_all[tile_rows(t), :]
        m_new = jnp.maximum(m_old, jnp.max(s, axis=-1, keepdims=True))
        alpha = jnp.exp2((m_old - m_new) * EXP2_SCALE)
        p = jnp.exp2((s - m_new) * EXP2_SCALE)
        m_all[tile_rows(t), :] = m_new
        acc_all[tile_rows(t), :] = alpha * acc_all[tile_rows(t), :] + jnp.dot(
            p.astype(BF16), vaug[key_rows(c, width), :], preferred_element_type=F32)

    def next_past(t, c):
        wrap = c + 1 == lax.shift_right_logical(t, 1)
        return jnp.where(wrap, t + 1, t), jnp.where(wrap, 0, c + 1)

    half = n_tiles // 2
    run_steps((0, 0), lambda t, c: (t + 2, c + 2), half, diagonal_step, half, 1)
    run_steps((1, 0), lambda t, c: (t + 2, c + 1), half, diagonal_step, half, 2)
    run_steps((2, 0), next_past, sum(t // 2 for t in range(n_tiles)), past_step, MOBA_PAST_STEPS_PER_TRIP, 2)

    def finish(t, carry):
        acc = acc_all[tile_rows(t), :]
        o_ref[tile_rows(t), :] = (acc[:, 0:hd] / acc[:, hd:2 * hd]).astype(o_ref.dtype)
        return carry
    lax.fori_loop(0, n_tiles, finish, 0)


def _moba(proj, batch, seq):
    tq = MOBA_TQ
    hd = HEAD_DIM
    full = lambda sl: pl.BlockSpec((None, seq, hd), lambda b, h: (b, 0, _bf16_slice(sl) * GROUP_HEADS + h))
    return pl.pallas_call(
        _moba_kernel,
        out_shape=jax.ShapeDtypeStruct((batch, seq, GROUP_WIDTH), BF16),
        grid=(batch, GROUP_HEADS),
        in_specs=[full(SL_C_Q), full(SL_C_K), full(SL_C_V)],
        out_specs=pl.BlockSpec((None, seq, hd), lambda b, h: (b, 0, h)),
        scratch_shapes=[
            pltpu.VMEM((seq, 2 * hd), BF16),
            pltpu.VMEM((seq, 2 * hd), BF16),
            pltpu.VMEM((seq // MOBA_BLOCK, hd), F32),
            pltpu.VMEM((seq, 2 * hd), BF16),
            pltpu.VMEM((seq, 1), F32),
            pltpu.VMEM((seq, 2 * hd), F32),
            pltpu.VMEM((tq, 2 * tq), F32),
            pltpu.VMEM((tq, 2 * tq), F32),
        ],
        compiler_params=_params("parallel", "parallel"),
        name="moba",
    )(proj, proj, proj)


def _memkv_kernel(mem_ref, g_ref, w_ref, o_ref):
    x = mem_ref[...]
    y = x * lax.rsqrt(jnp.mean(x * x, axis=-1, keepdims=True) + NORM_EPS) * g_ref[...]
    o_ref[...] = jnp.dot(y.astype(BF16), w_ref[...].astype(BF16),
                         preferred_element_type=F32).astype(o_ref.dtype)


def _memkv(mem, mem_norm_g, w_stack):
    batch, n_mem, d = mem.shape
    depth, _, n_cols = w_stack.shape
    return pl.pallas_call(
        _memkv_kernel,
        out_shape=jax.ShapeDtypeStruct((depth, batch, n_mem, n_cols), BF16),
        grid=(depth, batch),
        in_specs=[
            pl.BlockSpec((None, n_mem, d), lambda i, b: (b, 0, 0)),
            pl.BlockSpec((None, 1, d), lambda i, b: (i, 0, 0)),
            pl.BlockSpec((None, d, n_cols), lambda i, b: (i, 0, 0)),
        ],
        out_specs=pl.BlockSpec((None, None, n_mem, n_cols), lambda i, b: (i, b, 0, 0)),
        compiler_params=_params("parallel", "parallel"),
        name="memkv",
    )(mem, mem_norm_g.reshape(depth, 1, d), w_stack)


LOCAL_TM = 512


def _local_kernel(bu_ref, bv_ref, db_ref, dc_ref, dh_ref, hc_ref, hh_ref, xq_ref,
                  ws_ref, bs_ref, lng_ref, lnb_ref, cw_ref, kv_ref, o_ref, zbuf):
    tm = LOCAL_TM
    gw = GROUP_WIDTH

    u = jax.nn.gelu(bu_ref[...].astype(F32))
    v = jax.nn.gelu(bv_ref[...].astype(F32))
    mu = jnp.mean(v, axis=-1, keepdims=True)
    vc = v - mu
    var = jnp.mean(vc * vc, axis=-1, keepdims=True)
    vnorm = (vc * lax.rsqrt(var + NORM_EPS) * lng_ref[...] + lnb_ref[...]).astype(BF16)
    r = lax.broadcasted_iota(jnp.int32, (SGU_CHUNK, SGU_CHUNK), 0)
    c = lax.broadcasted_iota(jnp.int32, (SGU_CHUNK, SGU_CHUNK), 1)
    for hd in range(GROUP_HEADS):
        cols = slice(hd * HEAD_DIM, (hd + 1) * HEAD_DIM)
        w_causal = jnp.where(c <= r, ws_ref[hd], 0.0).astype(BF16)
        bias = bs_ref[:, hd:hd + 1]
        for ch in range(tm // SGU_CHUNK):
            rows = slice(ch * SGU_CHUNK, (ch + 1) * SGU_CHUNK)
            mixed = jnp.dot(w_causal, vnorm[rows, cols], preferred_element_type=F32) + bias
            o_ref[rows, cols] = (u[rows, cols] * mixed).astype(o_ref.dtype)

    first_tile = pl.program_id(1) == 0
    zbuf[0:CONV_HALO, :] = jnp.where(first_tile, 0.0, hc_ref[...].astype(F32) * hh_ref[...].astype(F32))
    zbuf[CONV_HALO:CONV_HALO + tm, :] = dc_ref[...].astype(F32) * dh_ref[...].astype(F32)
    y = cw_ref[CONV_WIDTH - 1:CONV_WIDTH, :] * zbuf[CONV_HALO:CONV_HALO + tm, :]
    for tap in range(CONV_WIDTH - 1):
        back = CONV_WIDTH - 1 - tap
        y = y + cw_ref[tap:tap + 1, :] * zbuf[CONV_HALO - back:CONV_HALO - back + tm, :]
    o_ref[:, gw:2 * gw] = (db_ref[...].astype(F32) * y).astype(o_ref.dtype)

    for hd in range(GROUP_HEADS):
        cols = slice(hd * HEAD_DIM, (hd + 1) * HEAD_DIM)
        kh = kv_ref[:, hd * HEAD_DIM:(hd + 1) * HEAD_DIM]
        vh = kv_ref[:, gw + hd * HEAD_DIM:gw + (hd + 1) * HEAD_DIM]
        s = _dot_nt(xq_ref[:, cols], kh) * ATTN_SCALE
        m = jnp.max(s, axis=-1, keepdims=True)
        p = jnp.exp(s - m)
        l = jnp.sum(p, axis=-1, keepdims=True)
        o = jnp.dot(p.astype(BF16), vh, preferred_element_type=F32) / l
        o_ref[:, 2 * gw + hd * HEAD_DIM:2 * gw + (hd + 1) * HEAD_DIM] = o.astype(o_ref.dtype)


def _local(proj, memkv, layer, sgu_w, sgu_b_t, ln_g, ln_b, conv_w, batch, seq):
    tm = LOCAL_TM
    gw = GROUP_WIDTH
    halo_per_tile = tm // CONV_HALO
    tile = lambda sl: pl.BlockSpec((None, tm, gw), lambda b, i: (b, i, _bf16_slice(sl)))
    halo = lambda sl: pl.BlockSpec(
        (None, CONV_HALO, gw), lambda b, i: (b, jnp.maximum(i * halo_per_tile - 1, 0), _bf16_slice(sl)))
    whole = lambda a: pl.BlockSpec(a.shape, lambda b, i: (0,) * a.ndim)
    n_mem = memkv.shape[2]
    return pl.pallas_call(
        _local_kernel,
        out_shape=jax.ShapeDtypeStruct((batch, seq, 3 * gw), BF16),
        grid=(batch, seq // tm),
        in_specs=[
            tile(SL_B_U), tile(SL_B_V), tile(SL_D_B), tile(SL_D_C), tile(SL_D_H),
            halo(SL_D_C), halo(SL_D_H), tile(SL_X_Q),
            whole(sgu_w), whole(sgu_b_t), whole(ln_g), whole(ln_b), whole(conv_w),
            pl.BlockSpec((None, None, n_mem, 2 * gw), lambda b, i: (layer, b, 0, 0)),
        ],
        out_specs=pl.BlockSpec((None, tm, 3 * gw), lambda b, i: (b, i, 0)),
        scratch_shapes=[pltpu.VMEM((CONV_HALO + tm, gw), F32)],
        compiler_params=_params("parallel", "parallel"),
        name="local",
    )(proj, proj, proj, proj, proj, proj, proj, proj,
      sgu_w, sgu_b_t, ln_g, ln_b, conv_w, memkv)


OUTPROJ_TM = 256


def _outproj_kernel(ya_ref, yl_ref, yc_ref, ga_ref, gb_ref, gc_ref, gd_ref, gx_ref,
                    ng_ref, w_ref, x_ref, pg_ref, *outs, final):
    gw = GROUP_WIDTH
    branches = (
        (lambda: ya_ref[...], ga_ref),
        (lambda: yl_ref[:, 0:gw], gb_ref),
        (lambda: yc_ref[...], gc_ref),
        (lambda: yl_ref[:, gw:2 * gw], gd_ref),
        (lambda: yl_ref[:, 2 * gw:3 * gw], gx_ref),
    )
    acc = x_ref[...]
    for grp, (o, gate_ref) in enumerate(branches):
        rows = slice(grp * gw, (grp + 1) * gw)
        y = o().astype(F32) * gate_ref[...].astype(F32)
        y = y * lax.rsqrt(jnp.mean(y * y, axis=-1, keepdims=True) + NORM_EPS)
        y = (y * ng_ref[:, rows]).astype(BF16)
        acc = acc + jnp.dot(y, w_ref[rows, :], preferred_element_type=F32)
    normed = acc * lax.rsqrt(jnp.mean(acc * acc, axis=-1, keepdims=True) + NORM_EPS) * pg_ref[...]
    if final:
        (o_ref,) = outs
        o_ref[...] = normed
    else:
        o_ref, h_ref = outs
        o_ref[...] = acc
        h_ref[...] = normed.astype(h_ref.dtype)


def _outproj(ya, yl, yc, proj_b, out_norm_g, w_stack_bf16, layer, x2d, post_g, final):
    m, d = x2d.shape
    tm = OUTPROJ_TM
    gw = GROUP_WIDTH
    gate = lambda sl: pl.BlockSpec((tm, gw), lambda i: (i, _bf16_slice(sl)))
    row_tile = pl.BlockSpec((tm, d), lambda i: (i, 0))
    x_out = jax.ShapeDtypeStruct((m, d), F32)
    return pl.pallas_call(
        functools.partial(_outproj_kernel, final=final),
        out_shape=x_out if final else (x_out, jax.ShapeDtypeStruct((m, d), BF16)),
        grid=(m // tm,),
        in_specs=[
            pl.BlockSpec((tm, gw), lambda i: (i, 0)),
            pl.BlockSpec((tm, 3 * gw), lambda i: (i, 0)),
            pl.BlockSpec((tm, gw), lambda i: (i, 0)),
            gate(SL_A_G), gate(SL_B_G), gate(SL_C_G), gate(SL_D_G), gate(SL_X_G),
            pl.BlockSpec((1, MIX_WIDTH), lambda i: (0, 0)),
            pl.BlockSpec((None, MIX_WIDTH, d), lambda i: (layer, 0, 0)),
            row_tile,
            pl.BlockSpec((1, d), lambda i: (0, 0)),
        ],
        out_specs=row_tile if final else (row_tile, row_tile),
        compiler_params=_params("parallel"),
        name="outproj_final" if final else "outproj",
    )(ya, yl, yc, proj_b, proj_b, proj_b, proj_b, proj_b,
      out_norm_g.reshape(1, MIX_WIDTH), w_stack_bf16, x2d, post_g.reshape(1, d))


def _rotary_tables(seq):
    half = HEAD_DIM // 2
    inv_freq = ROPE_THETA ** (-np.arange(half, dtype=np.float64) / half)
    ang = np.arange(seq, dtype=np.float64)[:, None] * inv_freq[None, :]
    cos, sin = np.cos(ang), np.sin(ang)
    return (jnp.asarray(np.concatenate([cos, cos], axis=-1), F32),
            jnp.asarray(np.concatenate([-sin, sin], axis=-1), F32))


def kernel(x, mem, norm_g, w_in, sgu_w, sgu_b, sgu_ln_g, sgu_ln_b, conv_w, mem_norm_g, w_mem_kv,
           out_norm_g, w_out, final_norm_g):
    batch, seq, d = x.shape
    depth = w_in.shape[0]
    cos_t, sin_t = _rotary_tables(seq)
    x2d = x.reshape(batch * seq, d)
    h2d = _rmsnorm(x2d, norm_g[0])
    memkv = _memkv(mem, mem_norm_g, w_mem_kv)
    w_out_bf16 = w_out.astype(BF16)
    flat = lambda t: t.reshape(batch * seq, t.shape[-1])
    for i in range(depth):
        final = i == depth - 1
        proj_f, proj_b = _inproj(h2d, w_in, i, cos_t, sin_t, seq)
        proj_a = proj_f.reshape(batch, seq, proj_f.shape[-1])
        proj_r = proj_b.reshape(batch, seq, proj_b.shape[-1])
        ya = _dilated(proj_a, batch, seq)
        yc = _moba(proj_r, batch, seq)
        yl = _local(proj_r, memkv, i, sgu_w[i], sgu_b[i].T, sgu_ln_g[i].reshape(1, GROUP_WIDTH),
                    sgu_ln_b[i].reshape(1, GROUP_WIDTH), conv_w[i], batch, seq)
        out = _outproj(flat(ya), flat(yl), flat(yc), proj_b, out_norm_g[i], w_out_bf16, i, x2d,
                       final_norm_g if final else norm_g[i + 1], final)
        if final:
            x2d = out
        else:
            x2d, h2d = out
    return x2d.reshape(batch, seq, d)
```

```python
import functools

import jax
import jax.numpy as jnp
import numpy as np
from jax import lax
from jax.experimental import pallas as pl
from jax.experimental.pallas import tpu as pltpu

F32 = jnp.float32
BF16 = jnp.bfloat16

HEAD_DIM = 128
GROUP_HEADS = 4
GROUP_WIDTH = GROUP_HEADS * HEAD_DIM
N_GROUPS = 5
MIX_WIDTH = N_GROUPS * GROUP_WIDTH
ROPE_THETA = 10000.0
NORM_EPS = 1e-6
NEG_INF = -1e30
ATTN_SCALE = HEAD_DIM ** -0.5
LOG2E = 1.4426950408889634
EXP2_SCALE = ATTN_SCALE * LOG2E

DIL_BAND = 128
DIL_STRIDES = (1, 4, 16)
DIL_SUPER = DIL_BAND * DIL_STRIDES[-1]
SGU_CHUNK = 128
MOBA_BLOCK = 256
MOBA_TOPK = 3
CONV_WIDTH = 3
CONV_HALO = 16

SL_A_Q, SL_A_K, SL_A_V, SL_A_G = 0, 1, 2, 3
SL_B_U, SL_B_V, SL_B_G = 4, 5, 6
SL_C_Q, SL_C_K, SL_C_V, SL_C_G = 7, 8, 9, 10
SL_D_B, SL_D_C, SL_D_H, SL_D_G = 11, 12, 13, 14
SL_X_Q, SL_X_G = 15, 16
N_F32_SLICES = 3


def _bf16_slice(sl):
    assert sl >= N_F32_SLICES
    return sl - N_F32_SLICES


VMEM_LIMIT_BYTES = 56 * 1024 * 1024


def _params(*semantics):
    return pltpu.CompilerParams(dimension_semantics=semantics, vmem_limit_bytes=VMEM_LIMIT_BYTES)


def _dot_nt(a, b):
    return lax.dot_general(a, b, (((1,), (1,)), ((), ())), preferred_element_type=F32)


NORM_TM = 512


def _rmsnorm_kernel(x_ref, g_ref, o_ref):
    x = x_ref[...]
    y = x * lax.rsqrt(jnp.mean(x * x, axis=-1, keepdims=True) + NORM_EPS) * g_ref[...]
    o_ref[...] = y.astype(o_ref.dtype)


def _rmsnorm(x2d, g):
    m, d = x2d.shape
    return pl.pallas_call(
        _rmsnorm_kernel,
        out_shape=jax.ShapeDtypeStruct((m, d), BF16),
        grid=(m // NORM_TM,),
        in_specs=[pl.BlockSpec((NORM_TM, d), lambda i: (i, 0)), pl.BlockSpec((1, d), lambda i: (0, 0))],
        out_specs=pl.BlockSpec((NORM_TM, d), lambda i: (i, 0)),
        compiler_params=_params("parallel"),
        name="rmsnorm",
    )(x2d, g.reshape(1, d))


INPROJ_TM = 2048
INPROJ_ROW_CHUNKS = 16


def _inproj_kernel(h_ref, w_ref, cos_ref, sin_ref, of_ref, ob_ref):
    n = pl.program_id(1)
    chunk_rows = INPROJ_TM // INPROJ_ROW_CHUNKS

    def project(slices, finish):
        o_ref = of_ref if slices[0] < N_F32_SLICES else ob_ref
        assert all((s < N_F32_SLICES) == (slices[0] < N_F32_SLICES) for s in slices)

        @pl.when(functools.reduce(jnp.logical_or, [n == s for s in slices]))
        def _():
            w = w_ref[...].astype(BF16)
            for r in range(INPROJ_ROW_CHUNKS):
                rows = slice(r * chunk_rows, (r + 1) * chunk_rows)
                acc = jnp.dot(h_ref[rows, :], w, preferred_element_type=F32)
                if finish == "rotary":
                    cos = cos_ref[rows, :]
                    sin = sin_ref[rows, :]
                    for hh in range(GROUP_HEADS):
                        sl = slice(hh * HEAD_DIM, (hh + 1) * HEAD_DIM)
                        a = acc[:, sl]
                        o_ref[rows, sl] = (a * cos + pltpu.roll(a, HEAD_DIM // 2, 1) * sin).astype(o_ref.dtype)
                else:
                    if finish is not None:
                        acc = finish(acc)
                    o_ref[rows, :] = acc.astype(o_ref.dtype)

    project((SL_A_Q, SL_A_K), "rotary")
    project((SL_A_V,), None)
    project((SL_C_Q, SL_C_K), "rotary")
    project((SL_A_G, SL_B_G, SL_C_G, SL_D_G, SL_X_G), jax.nn.silu)
    project((SL_B_U, SL_B_V, SL_C_V, SL_D_B, SL_D_C, SL_D_H, SL_X_Q), None)


def _inproj(h2d, w_stack, layer, cos_t, sin_t, seq):
    m, d = h2d.shape
    gw = GROUP_WIDTH
    n_slices = w_stack.shape[2] // gw
    tm = INPROJ_TM
    seq_tiles = seq // tm
    return pl.pallas_call(
        _inproj_kernel,
        out_shape=(jax.ShapeDtypeStruct((m, N_F32_SLICES * gw), F32),
                   jax.ShapeDtypeStruct((m, (n_slices - N_F32_SLICES) * gw), BF16)),
        grid=(m // tm, n_slices),
        in_specs=[
            pl.BlockSpec((tm, d), lambda i, n: (i, 0)),
            pl.BlockSpec((None, d, gw), lambda i, n: (layer, 0, n)),
            pl.BlockSpec((tm, HEAD_DIM), lambda i, n: (i % seq_tiles, 0)),
            pl.BlockSpec((tm, HEAD_DIM), lambda i, n: (i % seq_tiles, 0)),
        ],
        out_specs=(pl.BlockSpec((tm, gw), lambda i, n: (i, jnp.minimum(n, N_F32_SLICES - 1))),
                   pl.BlockSpec((tm, gw), lambda i, n: (i, jnp.maximum(n - N_F32_SLICES, 0)))),
        compiler_params=_params("parallel", "arbitrary"),
        name="inproj",
    )(h2d, w_stack, cos_t, sin_t)


def _dilated_kernel(q_ref, kp_ref, ko_ref, vp_ref, vo_ref, o_ref,
                    q4, k4, v4, o1, o2, o3, l1, l2, l3):
    sup = DIL_SUPER
    band = DIL_BAND
    mid = DIL_STRIDES[1]
    assert DIL_STRIDES == (1, mid, mid * mid)
    per = sup // mid
    has_prev = pl.program_id(1) > 0

    for r in range(mid):
        q4[r * per:(r + 1) * per, :] = q_ref[pl.ds(r, per, stride=mid), :]
        for dst, prev_ref, own_ref in ((k4, kp_ref, ko_ref), (v4, vp_ref, vo_ref)):
            dst[2 * r * per:(2 * r + 1) * per, :] = prev_ref[pl.ds(r, per, stride=mid), :]
            dst[(2 * r + 1) * per:(2 * r + 2) * per, :] = own_ref[pl.ds(r, per, stride=mid), :]

    qi = lax.broadcasted_iota(jnp.int32, (band, 2 * band), 0)
    kj = lax.broadcasted_iota(jnp.int32, (band, 2 * band), 1)
    in_band = (kj >= qi) & (kj <= qi + band)
    bias_any = jnp.where(in_band, 0.0, NEG_INF).astype(F32)
    bias_start = jnp.where(in_band & (kj >= band), 0.0, NEG_INF).astype(F32)
    at_start = jnp.logical_not(has_prev)
    ones = jnp.ones((2 * band, HEAD_DIM), BF16)

    def attend(qb, kb, vb, first_band):
        bias = jnp.where(at_start, bias_start, bias_any) if first_band else bias_any
        t = _dot_nt(qb.astype(BF16), kb.astype(BF16)) * EXP2_SCALE + bias
        m = jnp.max(t, axis=-1, keepdims=True)
        p = jnp.exp2(t - m)
        va = jnp.concatenate([vb.astype(BF16), ones], axis=1)
        r = jnp.dot(p.astype(BF16), va, preferred_element_type=F32)
        l = r[:, HEAD_DIM:]
        return r[:, :HEAD_DIM] / l, m + jnp.log(l) * LOG2E

    for n in range(sup // band):
        rows = slice(n * band, (n + 1) * band)
        if n == 0:
            kb = jnp.concatenate([kp_ref[sup - band:sup, :], ko_ref[0:band, :]], axis=0)
            vb = jnp.concatenate([vp_ref[sup - band:sup, :], vo_ref[0:band, :]], axis=0)
        else:
            kb = ko_ref[(n - 1) * band:(n + 1) * band, :]
            vb = vo_ref[(n - 1) * band:(n + 1) * band, :]
        o1[rows, :], l1[rows, :] = attend(q_ref[rows, :], kb, vb, n == 0)

    for r in range(mid):
        for n in range(per // band):
            rows = slice(r * per + n * band, r * per + (n + 1) * band)
            keys = slice((2 * r + 1) * per + (n - 1) * band, (2 * r + 1) * per + (n + 1) * band)
            o2[rows, :], l2[rows, :] = attend(q4[rows, :], k4[keys, :], v4[keys, :], n == 0)

    for r in range(mid):
        for a in range(mid):
            rows = pl.ds(r * per + a, band, stride=mid)
            keys = pl.ds(2 * r * per + a, 2 * band, stride=mid)
            o3[rows, :], l3[rows, :] = attend(q4[rows, :], k4[keys, :], v4[keys, :], True)

    rows_per_step = 256

    def merge(i, c):
        r = i // (per // rows_per_step)
        l0 = (i % (per // rows_per_step)) * rows_per_step
        rows = pl.ds(pl.multiple_of(r * per + l0, rows_per_step), rows_per_step)
        positions = pl.ds(l0 * mid + r, rows_per_step, stride=mid)
        a1, a2, a3 = l1[positions, :], l2[rows, :], l3[rows, :]
        mx = jnp.maximum(jnp.maximum(a1, a2), a3)
        w1, w2, w3 = jnp.exp2(a1 - mx), jnp.exp2(a2 - mx), jnp.exp2(a3 - mx)
        num = w1 * o1[positions, :] + w2 * o2[rows, :] + w3 * o3[rows, :]
        o_ref[positions, :] = num / (w1 + w2 + w3)
        return c
    lax.fori_loop(0, sup // rows_per_step, merge, 0)


def _dilated(proj, batch, seq):
    sup = DIL_SUPER
    blk = (None, sup, HEAD_DIM)
    prev = lambda sb: jnp.maximum(sb - 1, 0)
    scr = pltpu.VMEM((sup, HEAD_DIM), F32)
    win = pltpu.VMEM((2 * sup, HEAD_DIM), F32)
    return pl.pallas_call(
        _dilated_kernel,
        out_shape=jax.ShapeDtypeStruct((batch, seq, GROUP_WIDTH), F32),
        grid=(batch, seq // sup, GROUP_HEADS),
        in_specs=[
            pl.BlockSpec(blk, lambda b, sb, h: (b, sb, SL_A_Q * GROUP_HEADS + h)),
            pl.BlockSpec(blk, lambda b, sb, h: (b, prev(sb), SL_A_K * GROUP_HEADS + h)),
            pl.BlockSpec(blk, lambda b, sb, h: (b, sb, SL_A_K * GROUP_HEADS + h)),
            pl.BlockSpec(blk, lambda b, sb, h: (b, prev(sb), SL_A_V * GROUP_HEADS + h)),
            pl.BlockSpec(blk, lambda b, sb, h: (b, sb, SL_A_V * GROUP_HEADS + h)),
        ],
        out_specs=pl.BlockSpec(blk, lambda b, sb, h: (b, sb, h)),
        scratch_shapes=[scr, win, win, scr, scr, scr, scr, scr, scr],
        compiler_params=_params("parallel", "parallel", "parallel"),
        name="dilated",
    )(proj, proj, proj, proj, proj)


MOBA_TQ = 2 * MOBA_BLOCK
MOBA_BLOCK_SHIFT = 8
assert 1 << MOBA_BLOCK_SHIFT == MOBA_BLOCK
MOBA_PAST_STEPS_PER_TRIP = 6


def _moba_kernel(q_ref, k_ref, v_ref, o_ref, kaug, vaug, kmean, qaug, m_all, acc_all, s_a, s_b):
    blk = MOBA_BLOCK
    tq = MOBA_TQ
    hd = HEAD_DIM
    seq = k_ref.shape[0]
    n_blocks = seq // blk
    n_tiles = seq // tq

    lane = lax.broadcasted_iota(jnp.int32, (blk, hd), 1)
    ones = jnp.ones((blk, hd), BF16)
    for n in range(n_blocks):
        rows = slice(n * blk, (n + 1) * blk)
        kb = k_ref[rows, :]
        kmean[n:n + 1, :] = jnp.mean(kb.astype(F32), axis=0, keepdims=True)
        kaug[rows, 0:hd] = kb
        kaug[rows, hd:2 * hd] = jnp.where(lane == n, 1.0, 0.0).astype(BF16)
        vaug[rows, 0:hd] = v_ref[rows, :]
        vaug[rows, hd:2 * hd] = ones

    def tile_rows(t):
        return pl.ds(pl.multiple_of(t * tq, tq), tq)

    km = kmean[...]
    km_hi = km.astype(BF16)
    rest = km - km_hi.astype(F32)
    km_mid = rest.astype(BF16)
    km_lo = (rest - km_mid.astype(F32)).astype(BF16)
    parts = _dot_nt(jnp.concatenate([km_hi, km_mid, km_lo], axis=0), q_ref[...])
    gate = parts[0:n_blocks] + parts[n_blocks:2 * n_blocks] + parts[2 * n_blocks:3 * n_blocks]
    bidx = lax.broadcasted_iota(jnp.int32, (n_blocks, seq), 0)
    bidx_f = bidx.astype(F32)
    qblk = lax.shift_right_logical(lax.broadcasted_iota(jnp.int32, (n_blocks, seq), 1), MOBA_BLOCK_SHIFT)
    g = jnp.where(bidx < qblk, gate, NEG_INF)
    sel_bias = jnp.full((n_blocks, seq), NEG_INF, F32)
    for k in range(MOBA_TOPK):
        mx = jnp.max(g, axis=0, keepdims=True)
        first = jnp.min(jnp.where(g == mx, bidx_f, float(n_blocks)), axis=0, keepdims=True)
        hit = bidx_f == first
        sel_bias = jnp.where(hit, jnp.where(k < qblk, 0.0, NEG_INF), sel_bias)
        g = jnp.where(hit, -jnp.inf, g)
    sel_bias = jnp.where(bidx == qblk, 0.0, sel_bias)
    sel_bias = jnp.concatenate([sel_bias, jnp.zeros((hd - n_blocks, seq), F32)], axis=0)
    qaug[:, 0:hd] = q_ref[...]
    for t in range(n_tiles):
        rows = slice(t * tq, (t + 1) * tq)
        qaug[rows, hd:2 * hd] = sel_bias[:, rows].T.astype(BF16)

    def key_rows(c, width):
        return pl.ds(pl.multiple_of(c * (width * tq), width * tq), width * tq)

    def scores(t, c, width):
        return _dot_nt(qaug[tile_rows(t), :], kaug[key_rows(c, width), :])

    def run_steps(first, advance, n_steps, step, per_trip, width):
        assert n_steps % per_trip == 0 and per_trip % 2 == 0
        cols = slice(0, width * tq)
        bufs = (s_a, s_b)
        s_a[:, cols] = scores(*first, width)

        def trip(j, tc):
            for i in range(per_trip):
                nxt = advance(*tc)
                bufs[(i + 1) % 2][:, cols] = scores(jnp.minimum(nxt[0], n_tiles - 1),
                                                    jnp.minimum(nxt[1], n_tiles // width - 1), width)
                step(bufs[i % 2].at[:, cols], *tc, width)
                tc = nxt
            return tc
        lax.fori_loop(0, n_steps // per_trip, trip, tuple(jnp.int32(v) for v in first))

    def diagonal_step(s_ref, t, c, width):
        r_i = lax.broadcasted_iota(jnp.int32, (tq, width * tq), 0)
        c_i = lax.broadcasted_iota(jnp.int32, (tq, width * tq), 1)
        s = jnp.where(c_i <= r_i + (width - 1) * tq, s_ref[...], NEG_INF)
        m = jnp.max(s, axis=-1, keepdims=True)
        p = jnp.exp2((s - m) * EXP2_SCALE)
        m_all[tile_rows(t), :] = m
        acc_all[tile_rows(t), :] = jnp.dot(p.astype(BF16), vaug[key_rows(c, width), :],
                                           preferred_element_type=F32)

    def past_step(s_ref, t, c, width):
        s = s_ref[...]
        m_old = m_all[tile_rows(t), :]
        m_new = jnp.maximum(m_old, jnp.max(s, axis=-1, keepdims=True))
        alpha = jnp.exp2((m_old - m_new) * EXP2_SCALE)
        p = jnp.exp2((s - m_new) * EXP2_SCALE)
        m_all[tile_rows(t), :] = m_new
        acc_all[tile_rows(t), :] = alpha * acc_all[tile_rows(t), :] + jnp.dot(
            p.astype(BF16), vaug[key_rows(c, width), :], preferred_element_type=F32)

    def next_past(t, c):
        wrap = c + 1 == lax.shift_right_logical(t, 1)
        return jnp.where(wrap, t + 1, t), jnp.where(wrap, 0, c + 1)

    first_steps = [(t, t, 1) if t % 2 == 0 else (t, t // 2, 2) for t in range(n_tiles)]
    bufs = (s_a, s_b)
    t0, c0, w0 = first_steps[0]
    s_a[:, 0:w0 * tq] = scores(t0, c0, w0)
    for i, (t, c, w) in enumerate(first_steps):
        if i + 1 < len(first_steps):
            tn, cn, wn = first_steps[i + 1]
            bufs[(i + 1) % 2][:, 0:wn * tq] = scores(tn, cn, wn)
        diagonal_step(bufs[i % 2].at[:, 0:w * tq], t, c, w)
    run_steps((2, 0), next_past, sum(t // 2 for t in range(n_tiles)), past_step, MOBA_PAST_STEPS_PER_TRIP, 2)

    def finish(t, carry):
        acc = acc_all[tile_rows(t), :]
        o_ref[tile_rows(t), :] = (acc[:, 0:hd] / acc[:, hd:2 * hd]).astype(o_ref.dtype)
        return carry
    lax.fori_loop(0, n_tiles, finish, 0)


def _moba(proj, batch, seq):
    tq = MOBA_TQ
    hd = HEAD_DIM
    full = lambda sl: pl.BlockSpec((None, seq, hd), lambda b, h: (b, 0, _bf16_slice(sl) * GROUP_HEADS + h))
    return pl.pallas_call(
        _moba_kernel,
        out_shape=jax.ShapeDtypeStruct((batch, seq, GROUP_WIDTH), BF16),
        grid=(batch, GROUP_HEADS),
        in_specs=[full(SL_C_Q), full(SL_C_K), full(SL_C_V)],
        out_specs=pl.BlockSpec((None, seq, hd), lambda b, h: (b, 0, h)),
        scratch_shapes=[
            pltpu.VMEM((seq, 2 * hd), BF16),
            pltpu.VMEM((seq, 2 * hd), BF16),
            pltpu.VMEM((seq // MOBA_BLOCK, hd), F32),
            pltpu.VMEM((seq, 2 * hd), BF16),
            pltpu.VMEM((seq, 1), F32),
            pltpu.VMEM((seq, 2 * hd), F32),
            pltpu.VMEM((tq, 2 * tq), F32),
            pltpu.VMEM((tq, 2 * tq), F32),
        ],
        compiler_params=_params("parallel", "parallel"),
        name="moba",
    )(proj, proj, proj)


def _memkv_kernel(mem_ref, g_ref, w_ref, o_ref):
    x = mem_ref[...]
    y = x * lax.rsqrt(jnp.mean(x * x, axis=-1, keepdims=True) + NORM_EPS) * g_ref[...]
    o_ref[...] = jnp.dot(y.astype(BF16), w_ref[...].astype(BF16),
                         preferred_element_type=F32).astype(o_ref.dtype)


def _memkv(mem, mem_norm_g, w_stack):
    batch, n_mem, d = mem.shape
    depth, _, n_cols = w_stack.shape
    return pl.pallas_call(
        _memkv_kernel,
        out_shape=jax.ShapeDtypeStruct((depth, batch, n_mem, n_cols), BF16),
        grid=(depth, batch),
        in_specs=[
            pl.BlockSpec((None, n_mem, d), lambda i, b: (b, 0, 0)),
            pl.BlockSpec((None, 1, d), lambda i, b: (i, 0, 0)),
            pl.BlockSpec((None, d, n_cols), lambda i, b: (i, 0, 0)),
        ],
        out_specs=pl.BlockSpec((None, None, n_mem, n_cols), lambda i, b: (i, b, 0, 0)),
        compiler_params=_params("parallel", "parallel"),
        name="memkv",
    )(mem, mem_norm_g.reshape(depth, 1, d), w_stack)


LOCAL_TM = 512


def _local_kernel(bu_ref, bv_ref, db_ref, dc_ref, dh_ref, hc_ref, hh_ref, xq_ref,
                  ws_ref, bs_ref, lng_ref, lnb_ref, cw_ref, kv_ref, o_ref, zbuf):
    tm = LOCAL_TM
    gw = GROUP_WIDTH

    u = jax.nn.gelu(bu_ref[...].astype(F32))
    v = jax.nn.gelu(bv_ref[...].astype(F32))
    mu = jnp.mean(v, axis=-1, keepdims=True)
    vc = v - mu
    var = jnp.mean(vc * vc, axis=-1, keepdims=True)
    vnorm = (vc * lax.rsqrt(var + NORM_EPS) * lng_ref[...] + lnb_ref[...]).astype(BF16)
    r = lax.broadcasted_iota(jnp.int32, (SGU_CHUNK, SGU_CHUNK), 0)
    c = lax.broadcasted_iota(jnp.int32, (SGU_CHUNK, SGU_CHUNK), 1)
    for hd in range(GROUP_HEADS):
        cols = slice(hd * HEAD_DIM, (hd + 1) * HEAD_DIM)
        w_causal = jnp.where(c <= r, ws_ref[hd], 0.0).astype(BF16)
        bias = bs_ref[:, hd:hd + 1]
        for ch in range(tm // SGU_CHUNK):
            rows = slice(ch * SGU_CHUNK, (ch + 1) * SGU_CHUNK)
            mixed = jnp.dot(w_causal, vnorm[rows, cols], preferred_element_type=F32) + bias
            o_ref[rows, cols] = (u[rows, cols] * mixed).astype(o_ref.dtype)

    first_tile = pl.program_id(1) == 0
    zbuf[0:CONV_HALO, :] = jnp.where(first_tile, 0.0, hc_ref[...].astype(F32) * hh_ref[...].astype(F32))
    zbuf[CONV_HALO:CONV_HALO + tm, :] = dc_ref[...].astype(F32) * dh_ref[...].astype(F32)
    y = cw_ref[CONV_WIDTH - 1:CONV_WIDTH, :] * zbuf[CONV_HALO:CONV_HALO + tm, :]
    for tap in range(CONV_WIDTH - 1):
        back = CONV_WIDTH - 1 - tap
        y = y + cw_ref[tap:tap + 1, :] * zbuf[CONV_HALO - back:CONV_HALO - back + tm, :]
    o_ref[:, gw:2 * gw] = (db_ref[...].astype(F32) * y).astype(o_ref.dtype)

    for hd in range(GROUP_HEADS):
        cols = slice(hd * HEAD_DIM, (hd + 1) * HEAD_DIM)
        kh = kv_ref[:, hd * HEAD_DIM:(hd + 1) * HEAD_DIM]
        vh = kv_ref[:, gw + hd * HEAD_DIM:gw + (hd + 1) * HEAD_DIM]
        s = _dot_nt(xq_ref[:, cols], kh) * ATTN_SCALE
        m = jnp.max(s, axis=-1, keepdims=True)
        p = jnp.exp(s - m)
        l = jnp.sum(p, axis=-1, keepdims=True)
        o = jnp.dot(p.astype(BF16), vh, preferred_element_type=F32) / l
        o_ref[:, 2 * gw + hd * HEAD_DIM:2 * gw + (hd + 1) * HEAD_DIM] = o.astype(o_ref.dtype)


def _local(proj, memkv, layer, sgu_w, sgu_b_t, ln_g, ln_b, conv_w, batch, seq):
    tm = LOCAL_TM
    gw = GROUP_WIDTH
    halo_per_tile = tm // CONV_HALO
    tile = lambda sl: pl.BlockSpec((None, tm, gw), lambda b, i: (b, i, _bf16_slice(sl)))
    halo = lambda sl: pl.BlockSpec(
        (None, CONV_HALO, gw), lambda b, i: (b, jnp.maximum(i * halo_per_tile - 1, 0), _bf16_slice(sl)))
    whole = lambda a: pl.BlockSpec(a.shape, lambda b, i: (0,) * a.ndim)
    n_mem = memkv.shape[2]
    return pl.pallas_call(
        _local_kernel,
        out_shape=jax.ShapeDtypeStruct((batch, seq, 3 * gw), BF16),
        grid=(batch, seq // tm),
        in_specs=[
            tile(SL_B_U), tile(SL_B_V), tile(SL_D_B), tile(SL_D_C), tile(SL_D_H),
            halo(SL_D_C), halo(SL_D_H), tile(SL_X_Q),
            whole(sgu_w), whole(sgu_b_t), whole(ln_g), whole(ln_b), whole(conv_w),
            pl.BlockSpec((None, None, n_mem, 2 * gw), lambda b, i: (layer, b, 0, 0)),
        ],
        out_specs=pl.BlockSpec((None, tm, 3 * gw), lambda b, i: (b, i, 0)),
        scratch_shapes=[pltpu.VMEM((CONV_HALO + tm, gw), F32)],
        compiler_params=_params("parallel", "parallel"),
        name="local",
    )(proj, proj, proj, proj, proj, proj, proj, proj,
      sgu_w, sgu_b_t, ln_g, ln_b, conv_w, memkv)


OUTPROJ_TM = 256


def _outproj_kernel(ya_ref, yl_ref, yc_ref, ga_ref, gb_ref, gc_ref, gd_ref, gx_ref,
                    ng_ref, w_ref, x_ref, pg_ref, *outs, final):
    gw = GROUP_WIDTH
    branches = (
        (lambda: ya_ref[...], ga_ref),
        (lambda: yl_ref[:, 0:gw], gb_ref),
        (lambda: yc_ref[...], gc_ref),
        (lambda: yl_ref[:, gw:2 * gw], gd_ref),
        (lambda: yl_ref[:, 2 * gw:3 * gw], gx_ref),
    )
    acc = x_ref[...]
    for grp, (o, gate_ref) in enumerate(branches):
        rows = slice(grp * gw, (grp + 1) * gw)
        y = o().astype(F32) * gate_ref[...].astype(F32)
        y = y * lax.rsqrt(jnp.mean(y * y, axis=-1, keepdims=True) + NORM_EPS)
        y = (y * ng_ref[:, rows]).astype(BF16)
        acc = acc + jnp.dot(y, w_ref[rows, :].astype(BF16), preferred_element_type=F32)
    normed = acc * lax.rsqrt(jnp.mean(acc * acc, axis=-1, keepdims=True) + NORM_EPS) * pg_ref[...]
    if final:
        (o_ref,) = outs
        o_ref[...] = normed
    else:
        o_ref, h_ref = outs
        o_ref[...] = acc
        h_ref[...] = normed.astype(h_ref.dtype)


def _outproj(ya, yl, yc, proj_b, out_norm_g, w_stack, layer, x2d, post_g, final):
    m, d = x2d.shape
    tm = OUTPROJ_TM
    gw = GROUP_WIDTH
    gate = lambda sl: pl.BlockSpec((tm, gw), lambda i: (i, _bf16_slice(sl)))
    row_tile = pl.BlockSpec((tm, d), lambda i: (i, 0))
    x_out = jax.ShapeDtypeStruct((m, d), F32)
    return pl.pallas_call(
        functools.partial(_outproj_kernel, final=final),
        out_shape=x_out if final else (x_out, jax.ShapeDtypeStruct((m, d), BF16)),
        grid=(m // tm,),
        in_specs=[
            pl.BlockSpec((tm, gw), lambda i: (i, 0)),
            pl.BlockSpec((tm, 3 * gw), lambda i: (i, 0)),
            pl.BlockSpec((tm, gw), lambda i: (i, 0)),
            gate(SL_A_G), gate(SL_B_G), gate(SL_C_G), gate(SL_D_G), gate(SL_X_G),
            pl.BlockSpec((1, MIX_WIDTH), lambda i: (0, 0)),
            pl.BlockSpec((None, MIX_WIDTH, d), lambda i: (layer, 0, 0), pipeline_mode=pl.Buffered(1)),
            row_tile,
            pl.BlockSpec((1, d), lambda i: (0, 0)),
        ],
        out_specs=row_tile if final else (row_tile, row_tile),
        compiler_params=_params("parallel"),
        name="outproj_final" if final else "outproj",
    )(ya, yl, yc, proj_b, proj_b, proj_b, proj_b, proj_b,
      out_norm_g.reshape(1, MIX_WIDTH), w_stack, x2d, post_g.reshape(1, d))


def _rotary_tables(seq):
    half = HEAD_DIM // 2
    inv_freq = ROPE_THETA ** (-np.arange(half, dtype=np.float64) / half)
    ang = np.arange(seq, dtype=np.float64)[:, None] * inv_freq[None, :]
    cos, sin = np.cos(ang), np.sin(ang)
    return (jnp.asarray(np.concatenate([cos, cos], axis=-1), F32),
            jnp.asarray(np.concatenate([-sin, sin], axis=-1), F32))


def kernel(x, mem, norm_g, w_in, sgu_w, sgu_b, sgu_ln_g, sgu_ln_b, conv_w, mem_norm_g, w_mem_kv,
           out_norm_g, w_out, final_norm_g):
    batch, seq, d = x.shape
    depth = w_in.shape[0]
    cos_t, sin_t = _rotary_tables(seq)
    x2d = x.reshape(batch * seq, d)
    h2d = _rmsnorm(x2d, norm_g[0])
    memkv = _memkv(mem, mem_norm_g, w_mem_kv)
    flat = lambda t: t.reshape(batch * seq, t.shape[-1])
    for i in range(depth):
        final = i == depth - 1
        proj_f, proj_b = _inproj(h2d, w_in, i, cos_t, sin_t, seq)
        proj_a = proj_f.reshape(batch, seq, proj_f.shape[-1])
        proj_r = proj_b.reshape(batch, seq, proj_b.shape[-1])
        ya = _dilated(proj_a, batch, seq)
        yc = _moba(proj_r, batch, seq)
        yl = _local(proj_r, memkv, i, sgu_w[i], sgu_b[i].T, sgu_ln_g[i].reshape(1, GROUP_WIDTH),
                    sgu_ln_b[i].reshape(1, GROUP_WIDTH), conv_w[i], batch, seq)
        out = _outproj(flat(ya), flat(yl), flat(yc), proj_b, out_norm_g[i], w_out, i, x2d,
                       final_norm_g if final else norm_g[i + 1], final)
        if final:
            x2d = out
        else:
            x2d, h2d = out
    return x2d.reshape(batch, seq, d)
```

```python
import functools

import jax
import jax.numpy as jnp
import numpy as np
from jax import lax
from jax.experimental import pallas as pl
from jax.experimental.pallas import tpu as pltpu

F32 = jnp.float32
BF16 = jnp.bfloat16

HEAD_DIM = 128
GROUP_HEADS = 4
GROUP_WIDTH = GROUP_HEADS * HEAD_DIM
N_GROUPS = 5
MIX_WIDTH = N_GROUPS * GROUP_WIDTH
ROPE_THETA = 10000.0
NORM_EPS = 1e-6
NEG_INF = -1e30
ATTN_SCALE = HEAD_DIM ** -0.5
LOG2E = 1.4426950408889634
EXP2_SCALE = ATTN_SCALE * LOG2E

DIL_BAND = 128
DIL_STRIDES = (1, 4, 16)
DIL_SUPER = DIL_BAND * DIL_STRIDES[-1]
SGU_CHUNK = 128
MOBA_BLOCK = 256
MOBA_TOPK = 3
CONV_WIDTH = 3
CONV_HALO = 16

SL_A_Q, SL_A_K, SL_A_V, SL_A_G = 0, 1, 2, 3
SL_B_U, SL_B_V, SL_B_G = 4, 5, 6
SL_C_Q, SL_C_K, SL_C_V, SL_C_G = 7, 8, 9, 10
SL_D_B, SL_D_C, SL_D_H, SL_D_G = 11, 12, 13, 14
SL_X_Q, SL_X_G = 15, 16
N_F32_SLICES = 3


def _bf16_slice(sl):
    assert sl >= N_F32_SLICES
    return sl - N_F32_SLICES


VMEM_LIMIT_BYTES = 56 * 1024 * 1024


def _params(*semantics):
    return pltpu.CompilerParams(dimension_semantics=semantics, vmem_limit_bytes=VMEM_LIMIT_BYTES)


def _dot_nt(a, b):
    return lax.dot_general(a, b, (((1,), (1,)), ((), ())), preferred_element_type=F32)


NORM_TM = 1024


def _rmsnorm_kernel(x_ref, g_ref, o_ref):
    x = x_ref[...]
    y = x * lax.rsqrt(jnp.mean(x * x, axis=-1, keepdims=True) + NORM_EPS) * g_ref[...]
    o_ref[...] = y.astype(o_ref.dtype)


def _rmsnorm(x2d, g):
    m, d = x2d.shape
    return pl.pallas_call(
        _rmsnorm_kernel,
        out_shape=jax.ShapeDtypeStruct((m, d), BF16),
        grid=(m // NORM_TM,),
        in_specs=[pl.BlockSpec((NORM_TM, d), lambda i: (i, 0)), pl.BlockSpec((1, d), lambda i: (0, 0))],
        out_specs=pl.BlockSpec((NORM_TM, d), lambda i: (i, 0)),
        compiler_params=_params("parallel"),
        name="rmsnorm",
    )(x2d, g.reshape(1, d))


INPROJ_TM = 2048
INPROJ_ROW_CHUNKS = 16


def _inproj_kernel(h_ref, w_ref, cos_ref, sin_ref, of_ref, ob_ref):
    n = pl.program_id(1)
    chunk_rows = INPROJ_TM // INPROJ_ROW_CHUNKS

    def project(slices, finish):
        o_ref = of_ref if slices[0] < N_F32_SLICES else ob_ref
        assert all((s < N_F32_SLICES) == (slices[0] < N_F32_SLICES) for s in slices)

        @pl.when(functools.reduce(jnp.logical_or, [n == s for s in slices]))
        def _():
            w = w_ref[...].astype(BF16)
            for r in range(INPROJ_ROW_CHUNKS):
                rows = slice(r * chunk_rows, (r + 1) * chunk_rows)
                acc = jnp.dot(h_ref[rows, :], w, preferred_element_type=F32)
                if finish == "rotary":
                    cos = cos_ref[rows, :]
                    sin = sin_ref[rows, :]
                    for hh in range(GROUP_HEADS):
                        sl = slice(hh * HEAD_DIM, (hh + 1) * HEAD_DIM)
                        a = acc[:, sl]
                        o_ref[rows, sl] = (a * cos + pltpu.roll(a, HEAD_DIM // 2, 1) * sin).astype(o_ref.dtype)
                else:
                    if finish is not None:
                        acc = finish(acc)
                    o_ref[rows, :] = acc.astype(o_ref.dtype)

    project((SL_A_Q, SL_A_K), "rotary")
    project((SL_A_V,), None)
    project((SL_C_Q, SL_C_K), "rotary")
    project((SL_A_G, SL_B_G, SL_C_G, SL_D_G, SL_X_G), jax.nn.silu)
    project((SL_B_U, SL_B_V, SL_C_V, SL_D_B, SL_D_C, SL_D_H, SL_X_Q), None)


def _inproj(h2d, w_stack, layer, cos_t, sin_t, seq):
    m, d = h2d.shape
    gw = GROUP_WIDTH
    n_slices = w_stack.shape[2] // gw
    tm = INPROJ_TM
    seq_tiles = seq // tm
    return pl.pallas_call(
        _inproj_kernel,
        out_shape=(jax.ShapeDtypeStruct((m, N_F32_SLICES * gw), F32),
                   jax.ShapeDtypeStruct((m, (n_slices - N_F32_SLICES) * gw), BF16)),
        grid=(m // tm, n_slices),
        in_specs=[
            pl.BlockSpec((tm, d), lambda i, n: (i, 0)),
            pl.BlockSpec((None, d, gw), lambda i, n: (layer, 0, n)),
            pl.BlockSpec((tm, HEAD_DIM), lambda i, n: (i % seq_tiles, 0)),
            pl.BlockSpec((tm, HEAD_DIM), lambda i, n: (i % seq_tiles, 0)),
        ],
        out_specs=(pl.BlockSpec((tm, gw), lambda i, n: (i, jnp.minimum(n, N_F32_SLICES - 1))),
                   pl.BlockSpec((tm, gw), lambda i, n: (i, jnp.maximum(n - N_F32_SLICES, 0)))),
        compiler_params=_params("parallel", "arbitrary"),
        name="inproj",
    )(h2d, w_stack, cos_t, sin_t)


def _dilated_kernel(q_ref, kp_ref, ko_ref, vp_ref, vo_ref, o_ref,
                    q4, k4, v4, o1, o2, o3, l1, l2, l3):
    sup = DIL_SUPER
    band = DIL_BAND
    mid = DIL_STRIDES[1]
    assert DIL_STRIDES == (1, mid, mid * mid)
    per = sup // mid
    has_prev = pl.program_id(1) > 0

    for r in range(mid):
        q4[r * per:(r + 1) * per, :] = q_ref[pl.ds(r, per, stride=mid), :]
        for dst, prev_ref, own_ref in ((k4, kp_ref, ko_ref), (v4, vp_ref, vo_ref)):
            dst[2 * r * per:(2 * r + 1) * per, :] = prev_ref[pl.ds(r, per, stride=mid), :]
            dst[(2 * r + 1) * per:(2 * r + 2) * per, :] = own_ref[pl.ds(r, per, stride=mid), :]

    qi = lax.broadcasted_iota(jnp.int32, (band, 2 * band), 0)
    kj = lax.broadcasted_iota(jnp.int32, (band, 2 * band), 1)
    in_band = (kj >= qi) & (kj <= qi + band)
    bias_any = jnp.where(in_band, 0.0, NEG_INF).astype(F32)
    bias_start = jnp.where(in_band & (kj >= band), 0.0, NEG_INF).astype(F32)
    at_start = jnp.logical_not(has_prev)
    ones = jnp.ones((2 * band, HEAD_DIM), BF16)

    def attend(qb, kb, vb, first_band):
        bias = jnp.where(at_start, bias_start, bias_any) if first_band else bias_any
        t = _dot_nt(qb.astype(BF16), kb.astype(BF16)) * EXP2_SCALE + bias
        m = jnp.max(t, axis=-1, keepdims=True)
        p = jnp.exp2(t - m)
        va = jnp.concatenate([vb.astype(BF16), ones], axis=1)
        r = jnp.dot(p.astype(BF16), va, preferred_element_type=F32)
        l = r[:, HEAD_DIM:]
        return r[:, :HEAD_DIM] / l, m + jnp.log(l) * LOG2E

    for n in range(sup // band):
        rows = slice(n * band, (n + 1) * band)
        if n == 0:
            kb = jnp.concatenate([kp_ref[sup - band:sup, :], ko_ref[0:band, :]], axis=0)
            vb = jnp.concatenate([vp_ref[sup - band:sup, :], vo_ref[0:band, :]], axis=0)
        else:
            kb = ko_ref[(n - 1) * band:(n + 1) * band, :]
            vb = vo_ref[(n - 1) * band:(n + 1) * band, :]
        o1[rows, :], l1[rows, :] = attend(q_ref[rows, :], kb, vb, n == 0)

    for r in range(mid):
        for n in range(per // band):
            rows = slice(r * per + n * band, r * per + (n + 1) * band)
            keys = slice((2 * r + 1) * per + (n - 1) * band, (2 * r + 1) * per + (n + 1) * band)
            o2[rows, :], l2[rows, :] = attend(q4[rows, :], k4[keys, :], v4[keys, :], n == 0)

    for r in range(mid):
        for a in range(mid):
            rows = pl.ds(r * per + a, band, stride=mid)
            keys = pl.ds(2 * r * per + a, 2 * band, stride=mid)
            o3[rows, :], l3[rows, :] = attend(q4[rows, :], k4[keys, :], v4[keys, :], True)

    rows_per_step = 256

    def merge(i, c):
        r = i // (per // rows_per_step)
        l0 = (i % (per // rows_per_step)) * rows_per_step
        rows = pl.ds(pl.multiple_of(r * per + l0, rows_per_step), rows_per_step)
        positions = pl.ds(l0 * mid + r, rows_per_step, stride=mid)
        a1, a2, a3 = l1[positions, :], l2[rows, :], l3[rows, :]
        mx = jnp.maximum(jnp.maximum(a1, a2), a3)
        w1, w2, w3 = jnp.exp2(a1 - mx), jnp.exp2(a2 - mx), jnp.exp2(a3 - mx)
        num = w1 * o1[positions, :] + w2 * o2[rows, :] + w3 * o3[rows, :]
        o_ref[positions, :] = num / (w1 + w2 + w3)
        return c
    lax.fori_loop(0, sup // rows_per_step, merge, 0)


def _dilated(proj, batch, seq):
    sup = DIL_SUPER
    blk = (None, sup, HEAD_DIM)
    prev = lambda sb: jnp.maximum(sb - 1, 0)
    scr = pltpu.VMEM((sup, HEAD_DIM), F32)
    win = pltpu.VMEM((2 * sup, HEAD_DIM), F32)
    return pl.pallas_call(
        _dilated_kernel,
        out_shape=jax.ShapeDtypeStruct((batch, seq, GROUP_WIDTH), F32),
        grid=(batch, seq // sup, GROUP_HEADS),
        in_specs=[
            pl.BlockSpec(blk, lambda b, sb, h: (b, sb, SL_A_Q * GROUP_HEADS + h)),
            pl.BlockSpec(blk, lambda b, sb, h: (b, prev(sb), SL_A_K * GROUP_HEADS + h)),
            pl.BlockSpec(blk, lambda b, sb, h: (b, sb, SL_A_K * GROUP_HEADS + h)),
            pl.BlockSpec(blk, lambda b, sb, h: (b, prev(sb), SL_A_V * GROUP_HEADS + h)),
            pl.BlockSpec(blk, lambda b, sb, h: (b, sb, SL_A_V * GROUP_HEADS + h)),
        ],
        out_specs=pl.BlockSpec(blk, lambda b, sb, h: (b, sb, h)),
        scratch_shapes=[scr, win, win, scr, scr, scr, scr, scr, scr],
        compiler_params=_params("parallel", "parallel", "parallel"),
        name="dilated",
    )(proj, proj, proj, proj, proj)


MOBA_TQ = 2 * MOBA_BLOCK
MOBA_BLOCK_SHIFT = 8
assert 1 << MOBA_BLOCK_SHIFT == MOBA_BLOCK
MOBA_PAST_STEPS_PER_TRIP = 6


def _moba_kernel(q_ref, k_ref, v_ref, o_ref, kaug, vaug, kmean, qaug, m_all, acc_all, s_a, s_b):
    blk = MOBA_BLOCK
    tq = MOBA_TQ
    hd = HEAD_DIM
    seq = k_ref.shape[0]
    n_blocks = seq // blk
    n_tiles = seq // tq

    lane = lax.broadcasted_iota(jnp.int32, (blk, hd), 1)
    ones = jnp.ones((blk, hd), BF16)
    for n in range(n_blocks):
        rows = slice(n * blk, (n + 1) * blk)
        kb = k_ref[rows, :]
        kmean[n:n + 1, :] = jnp.mean(kb.astype(F32), axis=0, keepdims=True)
        kaug[rows, 0:hd] = kb
        kaug[rows, hd:2 * hd] = jnp.where(lane == n, 1.0, 0.0).astype(BF16)
        vaug[rows, 0:hd] = v_ref[rows, :]
        vaug[rows, hd:2 * hd] = ones

    def tile_rows(t):
        return pl.ds(pl.multiple_of(t * tq, tq), tq)

    km = kmean[...]
    km_hi = km.astype(BF16)
    rest = km - km_hi.astype(F32)
    km_mid = rest.astype(BF16)
    km_lo = (rest - km_mid.astype(F32)).astype(BF16)
    parts = _dot_nt(jnp.concatenate([km_hi, km_mid, km_lo], axis=0), q_ref[...])
    gate = parts[0:n_blocks] + parts[n_blocks:2 * n_blocks] + parts[2 * n_blocks:3 * n_blocks]
    bidx = lax.broadcasted_iota(jnp.int32, (n_blocks, seq), 0)
    bidx_f = bidx.astype(F32)
    qblk = lax.shift_right_logical(lax.broadcasted_iota(jnp.int32, (n_blocks, seq), 1), MOBA_BLOCK_SHIFT)
    g = jnp.where(bidx < qblk, gate, NEG_INF)
    sel_bias = jnp.full((n_blocks, seq), NEG_INF, F32)
    for k in range(MOBA_TOPK):
        mx = jnp.max(g, axis=0, keepdims=True)
        first = jnp.min(jnp.where(g == mx, bidx_f, float(n_blocks)), axis=0, keepdims=True)
        hit = bidx_f == first
        sel_bias = jnp.where(hit, jnp.where(k < qblk, 0.0, NEG_INF), sel_bias)
        g = jnp.where(hit, -jnp.inf, g)
    sel_bias = jnp.where(bidx == qblk, 0.0, sel_bias)
    sel_bias = jnp.concatenate([sel_bias, jnp.zeros((hd - n_blocks, seq), F32)], axis=0)
    qaug[:, 0:hd] = q_ref[...]
    for t in range(n_tiles):
        rows = slice(t * tq, (t + 1) * tq)
        qaug[rows, hd:2 * hd] = sel_bias[:, rows].T.astype(BF16)

    def key_rows(c, width):
        return pl.ds(pl.multiple_of(c * (width * tq), width * tq), width * tq)

    def scores(t, c, width):
        return _dot_nt(qaug[tile_rows(t), :], kaug[key_rows(c, width), :])

    def run_steps(first, advance, n_steps, step, per_trip, width):
        assert n_steps % per_trip == 0 and per_trip % 2 == 0
        cols = slice(0, width * tq)
        bufs = (s_a, s_b)
        s_a[:, cols] = scores(*first, width)

        def trip(j, tc):
            for i in range(per_trip):
                nxt = advance(*tc)
                bufs[(i + 1) % 2][:, cols] = scores(jnp.minimum(nxt[0], n_tiles - 1),
                                                    jnp.minimum(nxt[1], n_tiles // width - 1), width)
                step(bufs[i % 2].at[:, cols], *tc, width)
                tc = nxt
            return tc
        lax.fori_loop(0, n_steps // per_trip, trip, tuple(jnp.int32(v) for v in first))

    def diagonal_step(s_ref, t, c, width):
        r_i = lax.broadcasted_iota(jnp.int32, (tq, width * tq), 0)
        c_i = lax.broadcasted_iota(jnp.int32, (tq, width * tq), 1)
        s = jnp.where(c_i <= r_i + (width - 1) * tq, s_ref[...], NEG_INF)
        m = jnp.max(s, axis=-1, keepdims=True)
        p = jnp.exp2((s - m) * EXP2_SCALE)
        m_all[tile_rows(t), :] = m
        acc_all[tile_rows(t), :] = jnp.dot(p.astype(BF16), vaug[key_rows(c, width), :],
                                           preferred_element_type=F32)

    def past_step(s_ref, t, c, width):
        s = s_ref[...]
        m_old = m_all[tile_rows(t), :]
        m_new = jnp.maximum(m_old, jnp.max(s, axis=-1, keepdims=True))
        alpha = jnp.exp2((m_old - m_new) * EXP2_SCALE)
        p = jnp.exp2((s - m_new) * EXP2_SCALE)
        m_all[tile_rows(t), :] = m_new
        acc_all[tile_rows(t), :] = alpha * acc_all[tile_rows(t), :] + jnp.dot(
            p.astype(BF16), vaug[key_rows(c, width), :], preferred_element_type=F32)

    def next_past(t, c):
        wrap = c + 1 == lax.shift_right_logical(t, 1)
        return jnp.where(wrap, t + 1, t), jnp.where(wrap, 0, c + 1)

    first_steps = [(t, t, 1) if t % 2 == 0 else (t, t // 2, 2) for t in range(n_tiles)]
    bufs = (s_a, s_b)
    t0, c0, w0 = first_steps[0]
    s_a[:, 0:w0 * tq] = scores(t0, c0, w0)
    for i, (t, c, w) in enumerate(first_steps):
        if i + 1 < len(first_steps):
            tn, cn, wn = first_steps[i + 1]
            bufs[(i + 1) % 2][:, 0:wn * tq] = scores(tn, cn, wn)
        diagonal_step(bufs[i % 2].at[:, 0:w * tq], t, c, w)
    run_steps((2, 0), next_past, sum(t // 2 for t in range(n_tiles)), past_step, MOBA_PAST_STEPS_PER_TRIP, 2)

    def finish(t, carry):
        acc = acc_all[tile_rows(t), :]
        o_ref[tile_rows(t), :] = (acc[:, 0:hd] / acc[:, hd:2 * hd]).astype(o_ref.dtype)
        return carry
    lax.fori_loop(0, n_tiles, finish, 0)


def _moba(proj, batch, seq):
    tq = MOBA_TQ
    hd = HEAD_DIM
    full = lambda sl: pl.BlockSpec((None, seq, hd), lambda b, h: (b, 0, _bf16_slice(sl) * GROUP_HEADS + h))
    return pl.pallas_call(
        _moba_kernel,
        out_shape=jax.ShapeDtypeStruct((batch, seq, GROUP_WIDTH), BF16),
        grid=(batch, GROUP_HEADS),
        in_specs=[full(SL_C_Q), full(SL_C_K), full(SL_C_V)],
        out_specs=pl.BlockSpec((None, seq, hd), lambda b, h: (b, 0, h)),
        scratch_shapes=[
            pltpu.VMEM((seq, 2 * hd), BF16),
            pltpu.VMEM((seq, 2 * hd), BF16),
            pltpu.VMEM((seq // MOBA_BLOCK, hd), F32),
            pltpu.VMEM((seq, 2 * hd), BF16),
            pltpu.VMEM((seq, 1), F32),
            pltpu.VMEM((seq, 2 * hd), F32),
            pltpu.VMEM((tq, 2 * tq), F32),
            pltpu.VMEM((tq, 2 * tq), F32),
        ],
        compiler_params=_params("parallel", "parallel"),
        name="moba",
    )(proj, proj, proj)


MEMKV_TK = 512


def _memkv_kernel(mem_ref, g_ref, w_ref, o_ref, y_scr, acc):
    c = pl.program_id(1)
    n_chunks = y_scr.shape[0]
    batch, n_mem, d = mem_ref.shape

    @pl.when(c == 0)
    def _():
        x = mem_ref[...].reshape(batch * n_mem, d)
        y = (x * lax.rsqrt(jnp.mean(x * x, axis=-1, keepdims=True) + NORM_EPS) * g_ref[...]).astype(BF16)
        for k in range(n_chunks):
            y_scr[k] = y[:, k * MEMKV_TK:(k + 1) * MEMKV_TK]
        acc[...] = jnp.zeros_like(acc)

    acc[...] += jnp.dot(y_scr[c], w_ref[...].astype(BF16), preferred_element_type=F32)

    @pl.when(c == n_chunks - 1)
    def _():
        o_ref[...] = acc[...].reshape(o_ref.shape).astype(o_ref.dtype)


def _memkv(mem, mem_norm_g, w_stack):
    batch, n_mem, d = mem.shape
    depth, _, n_cols = w_stack.shape
    tk = MEMKV_TK
    return pl.pallas_call(
        _memkv_kernel,
        out_shape=jax.ShapeDtypeStruct((depth, batch, n_mem, n_cols), BF16),
        grid=(depth, d // tk),
        in_specs=[
            pl.BlockSpec((batch, n_mem, d), lambda i, c: (0, 0, 0)),
            pl.BlockSpec((None, 1, d), lambda i, c: (i, 0, 0)),
            pl.BlockSpec((None, tk, n_cols), lambda i, c: (i, c, 0)),
        ],
        out_specs=pl.BlockSpec((None, batch, n_mem, n_cols), lambda i, c: (i, 0, 0, 0)),
        scratch_shapes=[pltpu.VMEM((d // tk, batch * n_mem, tk), BF16),
                        pltpu.VMEM((batch * n_mem, n_cols), F32)],
        compiler_params=_params("parallel", "arbitrary"),
        name="memkv",
    )(mem, mem_norm_g.reshape(depth, 1, d), w_stack)


LOCAL_TM = 512


def _local_kernel(bu_ref, bv_ref, db_ref, dc_ref, dh_ref, hc_ref, hh_ref, xq_ref,
                  ws_ref, bs_ref, lng_ref, lnb_ref, cw_ref, kv_ref, o_ref, zbuf):
    tm = LOCAL_TM
    gw = GROUP_WIDTH

    u = jax.nn.gelu(bu_ref[...].astype(F32))
    v = jax.nn.gelu(bv_ref[...].astype(F32))
    mu = jnp.mean(v, axis=-1, keepdims=True)
    vc = v - mu
    var = jnp.mean(vc * vc, axis=-1, keepdims=True)
    vnorm = (vc * lax.rsqrt(var + NORM_EPS) * lng_ref[...] + lnb_ref[...]).astype(BF16)
    r = lax.broadcasted_iota(jnp.int32, (SGU_CHUNK, SGU_CHUNK), 0)
    c = lax.broadcasted_iota(jnp.int32, (SGU_CHUNK, SGU_CHUNK), 1)
    for hd in range(GROUP_HEADS):
        cols = slice(hd * HEAD_DIM, (hd + 1) * HEAD_DIM)
        w_causal = jnp.where(c <= r, ws_ref[hd], 0.0).astype(BF16)
        bias = bs_ref[:, hd:hd + 1]
        for ch in range(tm // SGU_CHUNK):
            rows = slice(ch * SGU_CHUNK, (ch + 1) * SGU_CHUNK)
            mixed = jnp.dot(w_causal, vnorm[rows, cols], preferred_element_type=F32) + bias
            o_ref[rows, cols] = (u[rows, cols] * mixed).astype(o_ref.dtype)

    first_tile = pl.program_id(1) == 0
    zbuf[0:CONV_HALO, :] = jnp.where(first_tile, 0.0, hc_ref[...].astype(F32) * hh_ref[...].astype(F32))
    zbuf[CONV_HALO:CONV_HALO + tm, :] = dc_ref[...].astype(F32) * dh_ref[...].astype(F32)
    y = cw_ref[CONV_WIDTH - 1:CONV_WIDTH, :] * zbuf[CONV_HALO:CONV_HALO + tm, :]
    for tap in range(CONV_WIDTH - 1):
        back = CONV_WIDTH - 1 - tap
        y = y + cw_ref[tap:tap + 1, :] * zbuf[CONV_HALO - back:CONV_HALO - back + tm, :]
    o_ref[:, gw:2 * gw] = (db_ref[...].astype(F32) * y).astype(o_ref.dtype)

    for hd in range(GROUP_HEADS):
        cols = slice(hd * HEAD_DIM, (hd + 1) * HEAD_DIM)
        kh = kv_ref[:, hd * HEAD_DIM:(hd + 1) * HEAD_DIM]
        vh = kv_ref[:, gw + hd * HEAD_DIM:gw + (hd + 1) * HEAD_DIM]
        s = _dot_nt(xq_ref[:, cols], kh) * ATTN_SCALE
        m = jnp.max(s, axis=-1, keepdims=True)
        p = jnp.exp(s - m)
        l = jnp.sum(p, axis=-1, keepdims=True)
        o = jnp.dot(p.astype(BF16), vh, preferred_element_type=F32) / l
        o_ref[:, 2 * gw + hd * HEAD_DIM:2 * gw + (hd + 1) * HEAD_DIM] = o.astype(o_ref.dtype)


def _local(proj, memkv, layer, sgu_w, sgu_b_t, ln_g, ln_b, conv_w, batch, seq):
    tm = LOCAL_TM
    gw = GROUP_WIDTH
    halo_per_tile = tm // CONV_HALO
    tile = lambda sl: pl.BlockSpec((None, tm, gw), lambda b, i: (b, i, _bf16_slice(sl)))
    halo = lambda sl: pl.BlockSpec(
        (None, CONV_HALO, gw), lambda b, i: (b, jnp.maximum(i * halo_per_tile - 1, 0), _bf16_slice(sl)))
    whole = lambda a: pl.BlockSpec(a.shape, lambda b, i: (0,) * a.ndim)
    n_mem = memkv.shape[2]
    return pl.pallas_call(
        _local_kernel,
        out_shape=jax.ShapeDtypeStruct((batch, seq, 3 * gw), BF16),
        grid=(batch, seq // tm),
        in_specs=[
            tile(SL_B_U), tile(SL_B_V), tile(SL_D_B), tile(SL_D_C), tile(SL_D_H),
            halo(SL_D_C), halo(SL_D_H), tile(SL_X_Q),
            whole(sgu_w), whole(sgu_b_t), whole(ln_g), whole(ln_b), whole(conv_w),
            pl.BlockSpec((None, None, n_mem, 2 * gw), lambda b, i: (layer, b, 0, 0)),
        ],
        out_specs=pl.BlockSpec((None, tm, 3 * gw), lambda b, i: (b, i, 0)),
        scratch_shapes=[pltpu.VMEM((CONV_HALO + tm, gw), F32)],
        compiler_params=_params("parallel", "parallel"),
        name="local",
    )(proj, proj, proj, proj, proj, proj, proj, proj,
      sgu_w, sgu_b_t, ln_g, ln_b, conv_w, memkv)


OUTPROJ_TM = 256


def _outproj_kernel(ya_ref, yl_ref, yc_ref, ga_ref, gb_ref, gc_ref, gd_ref, gx_ref,
                    ng_ref, w_ref, x_ref, pg_ref, *outs, final):
    gw = GROUP_WIDTH
    branches = (
        (lambda: ya_ref[...], ga_ref),
        (lambda: yl_ref[:, 0:gw], gb_ref),
        (lambda: yc_ref[...], gc_ref),
        (lambda: yl_ref[:, gw:2 * gw], gd_ref),
        (lambda: yl_ref[:, 2 * gw:3 * gw], gx_ref),
    )
    acc = x_ref[...]
    for grp, (o, gate_ref) in enumerate(branches):
        rows = slice(grp * gw, (grp + 1) * gw)
        y = o().astype(F32) * gate_ref[...].astype(F32)
        y = y * lax.rsqrt(jnp.mean(y * y, axis=-1, keepdims=True) + NORM_EPS)
        y = (y * ng_ref[:, rows]).astype(BF16)
        acc = acc + jnp.dot(y, w_ref[rows, :].astype(BF16), preferred_element_type=F32)
    normed = acc * lax.rsqrt(jnp.mean(acc * acc, axis=-1, keepdims=True) + NORM_EPS) * pg_ref[...]
    if final:
        (o_ref,) = outs
        o_ref[...] = normed
    else:
        o_ref, h_ref = outs
        o_ref[...] = acc
        h_ref[...] = normed.astype(h_ref.dtype)


def _outproj(ya, yl, yc, proj_b, out_norm_g, w_stack, layer, x2d, post_g, final):
    m, d = x2d.shape
    tm = OUTPROJ_TM
    gw = GROUP_WIDTH
    gate = lambda sl: pl.BlockSpec((tm, gw), lambda i: (i, _bf16_slice(sl)))
    row_tile = pl.BlockSpec((tm, d), lambda i: (i, 0))
    x_out = jax.ShapeDtypeStruct((m, d), F32)
    return pl.pallas_call(
        functools.partial(_outproj_kernel, final=final),
        out_shape=x_out if final else (x_out, jax.ShapeDtypeStruct((m, d), BF16)),
        grid=(m // tm,),
        in_specs=[
            pl.BlockSpec((tm, gw), lambda i: (i, 0)),
            pl.BlockSpec((tm, 3 * gw), lambda i: (i, 0)),
            pl.BlockSpec((tm, gw), lambda i: (i, 0)),
            gate(SL_A_G), gate(SL_B_G), gate(SL_C_G), gate(SL_D_G), gate(SL_X_G),
            pl.BlockSpec((1, MIX_WIDTH), lambda i: (0, 0)),
            pl.BlockSpec((None, MIX_WIDTH, d), lambda i: (layer, 0, 0), pipeline_mode=pl.Buffered(1)),
            row_tile,
            pl.BlockSpec((1, d), lambda i: (0, 0)),
        ],
        out_specs=row_tile if final else (row_tile, row_tile),
        compiler_params=_params("parallel"),
        name="outproj_final" if final else "outproj",
    )(ya, yl, yc, proj_b, proj_b, proj_b, proj_b, proj_b,
      out_norm_g.reshape(1, MIX_WIDTH), w_stack, x2d, post_g.reshape(1, d))


def _rotary_tables(seq):
    half = HEAD_DIM // 2
    inv_freq = ROPE_THETA ** (-np.arange(half, dtype=np.float64) / half)
    ang = np.arange(seq, dtype=np.float64)[:, None] * inv_freq[None, :]
    cos, sin = np.cos(ang), np.sin(ang)
    return (jnp.asarray(np.concatenate([cos, cos], axis=-1), F32),
            jnp.asarray(np.concatenate([-sin, sin], axis=-1), F32))


def kernel(x, mem, norm_g, w_in, sgu_w, sgu_b, sgu_ln_g, sgu_ln_b, conv_w, mem_norm_g, w_mem_kv,
           out_norm_g, w_out, final_norm_g):
    batch, seq, d = x.shape
    depth = w_in.shape[0]
    cos_t, sin_t = _rotary_tables(seq)
    x2d = x.reshape(batch * seq, d)
    h2d = _rmsnorm(x2d, norm_g[0])
    memkv = _memkv(mem, mem_norm_g, w_mem_kv)
    flat = lambda t: t.reshape(batch * seq, t.shape[-1])
    for i in range(depth):
        final = i == depth - 1
        proj_f, proj_b = _inproj(h2d, w_in, i, cos_t, sin_t, seq)
        proj_a = proj_f.reshape(batch, seq, proj_f.shape[-1])
        proj_r = proj_b.reshape(batch, seq, proj_b.shape[-1])
        ya = _dilated(proj_a, batch, seq)
        yc = _moba(proj_r, batch, seq)
        yl = _local(proj_r, memkv, i, sgu_w[i], sgu_b[i].T, sgu_ln_g[i].reshape(1, GROUP_WIDTH),
                    sgu_ln_b[i].reshape(1, GROUP_WIDTH), conv_w[i], batch, seq)
        out = _outproj(flat(ya), flat(yl), flat(yc), proj_b, out_norm_g[i], w_out, i, x2d,
                       final_norm_g if final else norm_g[i + 1], final)
        if final:
            x2d = out
        else:
            x2d, h2d = out
    return x2d.reshape(batch, seq, d)
```

```python
import functools

import jax
import jax.numpy as jnp
import numpy as np
from jax import lax
from jax.experimental import pallas as pl
from jax.experimental.pallas import tpu as pltpu

F32 = jnp.float32
BF16 = jnp.bfloat16

HEAD_DIM = 128
GROUP_HEADS = 4
GROUP_WIDTH = GROUP_HEADS * HEAD_DIM
N_GROUPS = 5
MIX_WIDTH = N_GROUPS * GROUP_WIDTH
ROPE_THETA = 10000.0
NORM_EPS = 1e-6
NEG_INF = -1e30
ATTN_SCALE = HEAD_DIM ** -0.5
LOG2E = 1.4426950408889634
EXP2_SCALE = ATTN_SCALE * LOG2E

DIL_BAND = 128
DIL_STRIDES = (1, 4, 16)
DIL_SUPER = DIL_BAND * DIL_STRIDES[-1]
SGU_CHUNK = 128
MOBA_BLOCK = 256
MOBA_TOPK = 3
CONV_WIDTH = 3
CONV_HALO = 16

SL_A_Q, SL_A_K, SL_A_V, SL_A_G = 0, 1, 2, 3
SL_B_U, SL_B_V, SL_B_G = 4, 5, 6
SL_C_Q, SL_C_K, SL_C_V, SL_C_G = 7, 8, 9, 10
SL_D_B, SL_D_C, SL_D_H, SL_D_G = 11, 12, 13, 14
SL_X_Q, SL_X_G = 15, 16
N_F32_SLICES = 3


def _bf16_slice(sl):
    assert sl >= N_F32_SLICES
    return sl - N_F32_SLICES


VMEM_LIMIT_BYTES = 56 * 1024 * 1024


def _params(*semantics, fuse_inputs=None):
    return pltpu.CompilerParams(dimension_semantics=semantics, vmem_limit_bytes=VMEM_LIMIT_BYTES,
                                allow_input_fusion=fuse_inputs)


def _dot_nt(a, b):
    return lax.dot_general(a, b, (((1,), (1,)), ((), ())), preferred_element_type=F32)


NORM_TM = 512


def _rmsnorm_kernel(x_ref, g_ref, o_ref):
    x = x_ref[...]
    y = x * lax.rsqrt(jnp.mean(x * x, axis=-1, keepdims=True) + NORM_EPS) * g_ref[...]
    o_ref[...] = y.astype(o_ref.dtype)


def _rmsnorm(x2d, g):
    m, d = x2d.shape
    return pl.pallas_call(
        _rmsnorm_kernel,
        out_shape=jax.ShapeDtypeStruct((m, d), BF16),
        grid=(m // NORM_TM,),
        in_specs=[pl.BlockSpec((NORM_TM, d), lambda i: (i, 0)), pl.BlockSpec((1, d), lambda i: (0, 0))],
        out_specs=pl.BlockSpec((NORM_TM, d), lambda i: (i, 0)),
        compiler_params=_params("parallel", fuse_inputs=[False, True]),
        name="rmsnorm",
    )(x2d, g.reshape(1, d))


INPROJ_TM = 2048
INPROJ_ROW_CHUNKS = 16


def _inproj_kernel(h_ref, w_ref, cos_ref, sin_ref, of_ref, ob_ref):
    n = pl.program_id(1)
    chunk_rows = INPROJ_TM // INPROJ_ROW_CHUNKS

    def project(slices, finish):
        o_ref = of_ref if slices[0] < N_F32_SLICES else ob_ref
        assert all((s < N_F32_SLICES) == (slices[0] < N_F32_SLICES) for s in slices)

        @pl.when(functools.reduce(jnp.logical_or, [n == s for s in slices]))
        def _():
            w = w_ref[...].astype(BF16)
            for r in range(INPROJ_ROW_CHUNKS):
                rows = slice(r * chunk_rows, (r + 1) * chunk_rows)
                acc = jnp.dot(h_ref[rows, :], w, preferred_element_type=F32)
                if finish == "rotary":
                    cos = cos_ref[rows, :]
                    sin = sin_ref[rows, :]
                    for hh in range(GROUP_HEADS):
                        sl = slice(hh * HEAD_DIM, (hh + 1) * HEAD_DIM)
                        a = acc[:, sl]
                        o_ref[rows, sl] = (a * cos + pltpu.roll(a, HEAD_DIM // 2, 1) * sin).astype(o_ref.dtype)
                else:
                    if finish is not None:
                        acc = finish(acc)
                    o_ref[rows, :] = acc.astype(o_ref.dtype)

    project((SL_A_Q, SL_A_K), "rotary")
    project((SL_A_V,), None)
    project((SL_C_Q, SL_C_K), "rotary")
    project((SL_A_G, SL_B_G, SL_C_G, SL_D_G, SL_X_G), jax.nn.silu)
    project((SL_B_U, SL_B_V, SL_C_V, SL_D_B, SL_D_C, SL_D_H, SL_X_Q), None)


def _inproj(h2d, w_stack, layer, cos_t, sin_t, seq):
    m, d = h2d.shape
    gw = GROUP_WIDTH
    n_slices = w_stack.shape[2] // gw
    tm = INPROJ_TM
    seq_tiles = seq // tm
    return pl.pallas_call(
        _inproj_kernel,
        out_shape=(jax.ShapeDtypeStruct((m, N_F32_SLICES * gw), F32),
                   jax.ShapeDtypeStruct((m, (n_slices - N_F32_SLICES) * gw), BF16)),
        grid=(m // tm, n_slices),
        in_specs=[
            pl.BlockSpec((tm, d), lambda i, n: (i, 0)),
            pl.BlockSpec((None, d, gw), lambda i, n: (layer, 0, n)),
            pl.BlockSpec((tm, HEAD_DIM), lambda i, n: (i % seq_tiles, 0)),
            pl.BlockSpec((tm, HEAD_DIM), lambda i, n: (i % seq_tiles, 0)),
        ],
        out_specs=(pl.BlockSpec((tm, gw), lambda i, n: (i, jnp.minimum(n, N_F32_SLICES - 1))),
                   pl.BlockSpec((tm, gw), lambda i, n: (i, jnp.maximum(n - N_F32_SLICES, 0)))),
        compiler_params=_params("parallel", "arbitrary"),
        name="inproj",
    )(h2d, w_stack, cos_t, sin_t)


def _dilated_kernel(q_ref, kp_ref, ko_ref, vp_ref, vo_ref, o_ref,
                    q4, k4, v4, o1, o2, o3, l1, l2, l3):
    sup = DIL_SUPER
    band = DIL_BAND
    mid = DIL_STRIDES[1]
    assert DIL_STRIDES == (1, mid, mid * mid)
    per = sup // mid
    has_prev = pl.program_id(1) > 0

    for r in range(mid):
        q4[r * per:(r + 1) * per, :] = q_ref[pl.ds(r, per, stride=mid), :]
        for dst, prev_ref, own_ref in ((k4, kp_ref, ko_ref), (v4, vp_ref, vo_ref)):
            dst[2 * r * per:(2 * r + 1) * per, :] = prev_ref[pl.ds(r, per, stride=mid), :]
            dst[(2 * r + 1) * per:(2 * r + 2) * per, :] = own_ref[pl.ds(r, per, stride=mid), :]

    qi = lax.broadcasted_iota(jnp.int32, (band, 2 * band), 0)
    kj = lax.broadcasted_iota(jnp.int32, (band, 2 * band), 1)
    in_band = (kj >= qi) & (kj <= qi + band)
    bias_any = jnp.where(in_band, 0.0, NEG_INF).astype(F32)
    bias_start = jnp.where(in_band & (kj >= band), 0.0, NEG_INF).astype(F32)
    at_start = jnp.logical_not(has_prev)
    ones = jnp.ones((2 * band, HEAD_DIM), BF16)

    def attend(qb, kb, vb, first_band):
        bias = jnp.where(at_start, bias_start, bias_any) if first_band else bias_any
        t = _dot_nt(qb.astype(BF16), kb.astype(BF16)) * EXP2_SCALE + bias
        m = jnp.max(t, axis=-1, keepdims=True)
        p = jnp.exp2(t - m)
        va = jnp.concatenate([vb.astype(BF16), ones], axis=1)
        r = jnp.dot(p.astype(BF16), va, preferred_element_type=F32)
        l = r[:, HEAD_DIM:]
        return r[:, :HEAD_DIM] / l, m + jnp.log(l) * LOG2E

    for n in range(sup // band):
        rows = slice(n * band, (n + 1) * band)
        if n == 0:
            kb = jnp.concatenate([kp_ref[sup - band:sup, :], ko_ref[0:band, :]], axis=0)
            vb = jnp.concatenate([vp_ref[sup - band:sup, :], vo_ref[0:band, :]], axis=0)
        else:
            kb = ko_ref[(n - 1) * band:(n + 1) * band, :]
            vb = vo_ref[(n - 1) * band:(n + 1) * band, :]
        o1[rows, :], l1[rows, :] = attend(q_ref[rows, :], kb, vb, n == 0)

    for r in range(mid):
        for n in range(per // band):
            rows = slice(r * per + n * band, r * per + (n + 1) * band)
            keys = slice((2 * r + 1) * per + (n - 1) * band, (2 * r + 1) * per + (n + 1) * band)
            o2[rows, :], l2[rows, :] = attend(q4[rows, :], k4[keys, :], v4[keys, :], n == 0)

    for r in range(mid):
        for a in range(mid):
            rows = pl.ds(r * per + a, band, stride=mid)
            keys = pl.ds(2 * r * per + a, 2 * band, stride=mid)
            o3[rows, :], l3[rows, :] = attend(q4[rows, :], k4[keys, :], v4[keys, :], True)

    rows_per_step = 256

    def merge(i, c):
        r = i // (per // rows_per_step)
        l0 = (i % (per // rows_per_step)) * rows_per_step
        rows = pl.ds(pl.multiple_of(r * per + l0, rows_per_step), rows_per_step)
        positions = pl.ds(l0 * mid + r, rows_per_step, stride=mid)
        a1, a2, a3 = l1[positions, :], l2[rows, :], l3[rows, :]
        mx = jnp.maximum(jnp.maximum(a1, a2), a3)
        w1, w2, w3 = jnp.exp2(a1 - mx), jnp.exp2(a2 - mx), jnp.exp2(a3 - mx)
        num = w1 * o1[positions, :] + w2 * o2[rows, :] + w3 * o3[rows, :]
        o_ref[positions, :] = num / (w1 + w2 + w3)
        return c
    lax.fori_loop(0, sup // rows_per_step, merge, 0)


def _dilated(proj, batch, seq):
    sup = DIL_SUPER
    blk = (None, sup, HEAD_DIM)
    prev = lambda sb: jnp.maximum(sb - 1, 0)
    scr = pltpu.VMEM((sup, HEAD_DIM), F32)
    win = pltpu.VMEM((2 * sup, HEAD_DIM), F32)
    return pl.pallas_call(
        _dilated_kernel,
        out_shape=jax.ShapeDtypeStruct((batch, seq, GROUP_WIDTH), F32),
        grid=(batch, seq // sup, GROUP_HEADS),
        in_specs=[
            pl.BlockSpec(blk, lambda b, sb, h: (b, sb, SL_A_Q * GROUP_HEADS + h)),
            pl.BlockSpec(blk, lambda b, sb, h: (b, prev(sb), SL_A_K * GROUP_HEADS + h)),
            pl.BlockSpec(blk, lambda b, sb, h: (b, sb, SL_A_K * GROUP_HEADS + h)),
            pl.BlockSpec(blk, lambda b, sb, h: (b, prev(sb), SL_A_V * GROUP_HEADS + h)),
            pl.BlockSpec(blk, lambda b, sb, h: (b, sb, SL_A_V * GROUP_HEADS + h)),
        ],
        out_specs=pl.BlockSpec(blk, lambda b, sb, h: (b, sb, h)),
        scratch_shapes=[scr, win, win, scr, scr, scr, scr, scr, scr],
        compiler_params=_params("parallel", "parallel", "parallel"),
        name="dilated",
    )(proj, proj, proj, proj, proj)


MOBA_TQ = 2 * MOBA_BLOCK
MOBA_BLOCK_SHIFT = 8
assert 1 << MOBA_BLOCK_SHIFT == MOBA_BLOCK
MOBA_PAST_STEPS_PER_TRIP = 6


def _moba_kernel(q_ref, k_ref, v_ref, o_ref, kaug, vaug, kmean, qaug, m_all, acc_all, s_a, s_b):
    blk = MOBA_BLOCK
    tq = MOBA_TQ
    hd = HEAD_DIM
    seq = k_ref.shape[0]
    n_blocks = seq // blk
    n_tiles = seq // tq

    lane = lax.broadcasted_iota(jnp.int32, (blk, hd), 1)
    ones = jnp.ones((blk, hd), BF16)
    for n in range(n_blocks):
        rows = slice(n * blk, (n + 1) * blk)
        kb = k_ref[rows, :]
        kmean[n:n + 1, :] = jnp.mean(kb.astype(F32), axis=0, keepdims=True)
        kaug[rows, 0:hd] = kb
        kaug[rows, hd:2 * hd] = jnp.where(lane == n, 1.0, 0.0).astype(BF16)
        vaug[rows, 0:hd] = v_ref[rows, :]
        vaug[rows, hd:2 * hd] = ones

    def tile_rows(t):
        return pl.ds(pl.multiple_of(t * tq, tq), tq)

    km = kmean[...]
    km_hi = km.astype(BF16)
    rest = km - km_hi.astype(F32)
    km_mid = rest.astype(BF16)
    km_lo = (rest - km_mid.astype(F32)).astype(BF16)
    parts = _dot_nt(jnp.concatenate([km_hi, km_mid, km_lo], axis=0), q_ref[...])
    gate = parts[0:n_blocks] + parts[n_blocks:2 * n_blocks] + parts[2 * n_blocks:3 * n_blocks]
    bidx = lax.broadcasted_iota(jnp.int32, (n_blocks, seq), 0)
    bidx_f = bidx.astype(F32)
    qblk = lax.shift_right_logical(lax.broadcasted_iota(jnp.int32, (n_blocks, seq), 1), MOBA_BLOCK_SHIFT)
    g = jnp.where(bidx < qblk, gate, NEG_INF)
    sel_bias = jnp.full((n_blocks, seq), NEG_INF, F32)
    for k in range(MOBA_TOPK):
        mx = jnp.max(g, axis=0, keepdims=True)
        first = jnp.min(jnp.where(g == mx, bidx_f, float(n_blocks)), axis=0, keepdims=True)
        hit = bidx_f == first
        sel_bias = jnp.where(hit, jnp.where(k < qblk, 0.0, NEG_INF), sel_bias)
        g = jnp.where(hit, -jnp.inf, g)
    sel_bias = jnp.where(bidx == qblk, 0.0, sel_bias)
    sel_bias = jnp.concatenate([sel_bias, jnp.zeros((hd - n_blocks, seq), F32)], axis=0)
    qaug[:, 0:hd] = q_ref[...]
    for t in range(n_tiles):
        rows = slice(t * tq, (t + 1) * tq)
        qaug[rows, hd:2 * hd] = sel_bias[:, rows].T.astype(BF16)

    def key_rows(c, width):
        return pl.ds(pl.multiple_of(c * (width * tq), width * tq), width * tq)

    def scores(t, c, width):
        return _dot_nt(qaug[tile_rows(t), :], kaug[key_rows(c, width), :])

    def run_steps(first, advance, n_steps, step, per_trip, width):
        assert n_steps % per_trip == 0 and per_trip % 2 == 0
        cols = slice(0, width * tq)
        bufs = (s_a, s_b)
        s_a[:, cols] = scores(*first, width)

        def trip(j, tc):
            for i in range(per_trip):
                nxt = advance(*tc)
                bufs[(i + 1) % 2][:, cols] = scores(jnp.minimum(nxt[0], n_tiles - 1),
                                                    jnp.minimum(nxt[1], n_tiles // width - 1), width)
                step(bufs[i % 2].at[:, cols], *tc, width)
                tc = nxt
            return tc
        lax.fori_loop(0, n_steps // per_trip, trip, tuple(jnp.int32(v) for v in first))

    def diagonal_step(s_ref, t, c, width):
        r_i = lax.broadcasted_iota(jnp.int32, (tq, width * tq), 0)
        c_i = lax.broadcasted_iota(jnp.int32, (tq, width * tq), 1)
        s = jnp.where(c_i <= r_i + (width - 1) * tq, s_ref[...], NEG_INF)
        m = jnp.max(s, axis=-1, keepdims=True)
        p = jnp.exp2((s - m) * EXP2_SCALE)
        m_all[tile_rows(t), :] = m
        acc_all[tile_rows(t), :] = jnp.dot(p.astype(BF16), vaug[key_rows(c, width), :],
                                           preferred_element_type=F32)

    def past_step(s_ref, t, c, width):
        s = s_ref[...]
        m_old = m_all[tile_rows(t), :]
        m_new = jnp.maximum(m_old, jnp.max(s, axis=-1, keepdims=True))
        alpha = jnp.exp2((m_old - m_new) * EXP2_SCALE)
        p = jnp.exp2((s - m_new) * EXP2_SCALE)
        m_all[tile_rows(t), :] = m_new
        acc_all[tile_rows(t), :] = alpha * acc_all[tile_rows(t), :] + jnp.dot(
            p.astype(BF16), vaug[key_rows(c, width), :], preferred_element_type=F32)

    def next_past(t, c):
        wrap = c + 1 == lax.shift_right_logical(t, 1)
        return jnp.where(wrap, t + 1, t), jnp.where(wrap, 0, c + 1)

    first_steps = [(t, t, 1) if t % 2 == 0 else (t, t // 2, 2) for t in range(n_tiles)]
    bufs = (s_a, s_b)
    t0, c0, w0 = first_steps[0]
    s_a[:, 0:w0 * tq] = scores(t0, c0, w0)
    for i, (t, c, w) in enumerate(first_steps):
        if i + 1 < len(first_steps):
            tn, cn, wn = first_steps[i + 1]
            bufs[(i + 1) % 2][:, 0:wn * tq] = scores(tn, cn, wn)
        diagonal_step(bufs[i % 2].at[:, 0:w * tq], t, c, w)
    run_steps((2, 0), next_past, sum(t // 2 for t in range(n_tiles)), past_step, MOBA_PAST_STEPS_PER_TRIP, 2)

    def finish(t, carry):
        acc = acc_all[tile_rows(t), :]
        o_ref[tile_rows(t), :] = (acc[:, 0:hd] / acc[:, hd:2 * hd]).astype(o_ref.dtype)
        return carry
    lax.fori_loop(0, n_tiles, finish, 0)


def _moba(proj, batch, seq):
    tq = MOBA_TQ
    hd = HEAD_DIM
    full = lambda sl: pl.BlockSpec((None, seq, hd), lambda b, h: (b, 0, _bf16_slice(sl) * GROUP_HEADS + h))
    return pl.pallas_call(
        _moba_kernel,
        out_shape=jax.ShapeDtypeStruct((batch, seq, GROUP_WIDTH), BF16),
        grid=(batch, GROUP_HEADS),
        in_specs=[full(SL_C_Q), full(SL_C_K), full(SL_C_V)],
        out_specs=pl.BlockSpec((None, seq, hd), lambda b, h: (b, 0, h)),
        scratch_shapes=[
            pltpu.VMEM((seq, 2 * hd), BF16),
            pltpu.VMEM((seq, 2 * hd), BF16),
            pltpu.VMEM((seq // MOBA_BLOCK, hd), F32),
            pltpu.VMEM((seq, 2 * hd), BF16),
            pltpu.VMEM((seq, 1), F32),
            pltpu.VMEM((seq, 2 * hd), F32),
            pltpu.VMEM((tq, 2 * tq), F32),
            pltpu.VMEM((tq, 2 * tq), F32),
        ],
        compiler_params=_params("parallel", "parallel"),
        name="moba",
    )(proj, proj, proj)


def _memkv_kernel(mem_ref, g_ref, w_ref, o_ref):
    x = mem_ref[...]
    y = x * lax.rsqrt(jnp.mean(x * x, axis=-1, keepdims=True) + NORM_EPS) * g_ref[...]
    o_ref[...] = jnp.dot(y.astype(BF16), w_ref[...].astype(BF16),
                         preferred_element_type=F32).astype(o_ref.dtype)


def _memkv(mem, mem_norm_g, w_stack):
    batch, n_mem, d = mem.shape
    depth, _, n_cols = w_stack.shape
    return pl.pallas_call(
        _memkv_kernel,
        out_shape=jax.ShapeDtypeStruct((depth, batch, n_mem, n_cols), BF16),
        grid=(depth, batch),
        in_specs=[
            pl.BlockSpec((None, n_mem, d), lambda i, b: (b, 0, 0)),
            pl.BlockSpec((None, 1, d), lambda i, b: (i, 0, 0)),
            pl.BlockSpec((None, d, n_cols), lambda i, b: (i, 0, 0)),
        ],
        out_specs=pl.BlockSpec((None, None, n_mem, n_cols), lambda i, b: (i, b, 0, 0)),
        compiler_params=_params("parallel", "parallel", fuse_inputs=[False, True, False]),
        name="memkv",
    )(mem, mem_norm_g.reshape(depth, 1, d), w_stack)


LOCAL_TM = 512


def _local_kernel(bu_ref, bv_ref, db_ref, dc_ref, dh_ref, hc_ref, hh_ref, xq_ref,
                  ws_ref, bs_ref, lng_ref, lnb_ref, cw_ref, kv_ref, o_ref, zbuf):
    tm = LOCAL_TM
    gw = GROUP_WIDTH

    u = jax.nn.gelu(bu_ref[...].astype(F32))
    v = jax.nn.gelu(bv_ref[...].astype(F32))
    mu = jnp.mean(v, axis=-1, keepdims=True)
    vc = v - mu
    var = jnp.mean(vc * vc, axis=-1, keepdims=True)
    vnorm = (vc * lax.rsqrt(var + NORM_EPS) * lng_ref[...] + lnb_ref[...]).astype(BF16)
    r = lax.broadcasted_iota(jnp.int32, (SGU_CHUNK, SGU_CHUNK), 0)
    c = lax.broadcasted_iota(jnp.int32, (SGU_CHUNK, SGU_CHUNK), 1)
    for hd in range(GROUP_HEADS):
        cols = slice(hd * HEAD_DIM, (hd + 1) * HEAD_DIM)
        w_causal = jnp.where(c <= r, ws_ref[hd], 0.0).astype(BF16)
        bias = bs_ref[:, hd:hd + 1]
        for ch in range(tm // SGU_CHUNK):
            rows = slice(ch * SGU_CHUNK, (ch + 1) * SGU_CHUNK)
            mixed = jnp.dot(w_causal, vnorm[rows, cols], preferred_element_type=F32) + bias
            o_ref[rows, cols] = (u[rows, cols] * mixed).astype(o_ref.dtype)

    first_tile = pl.program_id(1) == 0
    zbuf[0:CONV_HALO, :] = jnp.where(first_tile, 0.0, hc_ref[...].astype(F32) * hh_ref[...].astype(F32))
    zbuf[CONV_HALO:CONV_HALO + tm, :] = dc_ref[...].astype(F32) * dh_ref[...].astype(F32)
    y = cw_ref[CONV_WIDTH - 1:CONV_WIDTH, :] * zbuf[CONV_HALO:CONV_HALO + tm, :]
    for tap in range(CONV_WIDTH - 1):
        back = CONV_WIDTH - 1 - tap
        y = y + cw_ref[tap:tap + 1, :] * zbuf[CONV_HALO - back:CONV_HALO - back + tm, :]
    o_ref[:, gw:2 * gw] = (db_ref[...].astype(F32) * y).astype(o_ref.dtype)

    for hd in range(GROUP_HEADS):
        cols = slice(hd * HEAD_DIM, (hd + 1) * HEAD_DIM)
        kh = kv_ref[:, hd * HEAD_DIM:(hd + 1) * HEAD_DIM]
        vh = kv_ref[:, gw + hd * HEAD_DIM:gw + (hd + 1) * HEAD_DIM]
        s = _dot_nt(xq_ref[:, cols], kh) * ATTN_SCALE
        m = jnp.max(s, axis=-1, keepdims=True)
        p = jnp.exp(s - m)
        l = jnp.sum(p, axis=-1, keepdims=True)
        o = jnp.dot(p.astype(BF16), vh, preferred_element_type=F32) / l
        o_ref[:, 2 * gw + hd * HEAD_DIM:2 * gw + (hd + 1) * HEAD_DIM] = o.astype(o_ref.dtype)


def _local(proj, memkv, layer, sgu_w, sgu_b_t, ln_g, ln_b, conv_w, batch, seq):
    tm = LOCAL_TM
    gw = GROUP_WIDTH
    halo_per_tile = tm // CONV_HALO
    tile = lambda sl: pl.BlockSpec((None, tm, gw), lambda b, i: (b, i, _bf16_slice(sl)))
    halo = lambda sl: pl.BlockSpec(
        (None, CONV_HALO, gw), lambda b, i: (b, jnp.maximum(i * halo_per_tile - 1, 0), _bf16_slice(sl)))
    whole = lambda a: pl.BlockSpec(a.shape, lambda b, i: (0,) * a.ndim)
    n_mem = memkv.shape[2]
    return pl.pallas_call(
        _local_kernel,
        out_shape=jax.ShapeDtypeStruct((batch, seq, 3 * gw), BF16),
        grid=(batch, seq // tm),
        in_specs=[
            tile(SL_B_U), tile(SL_B_V), tile(SL_D_B), tile(SL_D_C), tile(SL_D_H),
            halo(SL_D_C), halo(SL_D_H), tile(SL_X_Q),
            whole(sgu_w), whole(sgu_b_t), whole(ln_g), whole(ln_b), whole(conv_w),
            pl.BlockSpec((None, None, n_mem, 2 * gw), lambda b, i: (layer, b, 0, 0)),
        ],
        out_specs=pl.BlockSpec((None, tm, 3 * gw), lambda b, i: (b, i, 0)),
        scratch_shapes=[pltpu.VMEM((CONV_HALO + tm, gw), F32)],
        compiler_params=_params("parallel", "parallel", fuse_inputs=[False] * 8 + [True] * 5 + [False]),
        name="local",
    )(proj, proj, proj, proj, proj, proj, proj, proj,
      sgu_w, sgu_b_t, ln_g, ln_b, conv_w, memkv)


OUTPROJ_TM = 256


def _outproj_kernel(ya_ref, yl_ref, yc_ref, ga_ref, gb_ref, gc_ref, gd_ref, gx_ref,
                    ng_ref, w_ref, x_ref, pg_ref, *outs, final):
    gw = GROUP_WIDTH
    branches = (
        (lambda: ya_ref[...], ga_ref),
        (lambda: yl_ref[:, 0:gw], gb_ref),
        (lambda: yc_ref[...], gc_ref),
        (lambda: yl_ref[:, gw:2 * gw], gd_ref),
        (lambda: yl_ref[:, 2 * gw:3 * gw], gx_ref),
    )
    acc = x_ref[...]
    for grp, (o, gate_ref) in enumerate(branches):
        rows = slice(grp * gw, (grp + 1) * gw)
        y = o().astype(F32) * gate_ref[...].astype(F32)
        y = y * lax.rsqrt(jnp.mean(y * y, axis=-1, keepdims=True) + NORM_EPS)
        y = (y * ng_ref[:, rows]).astype(BF16)
        acc = acc + jnp.dot(y, w_ref[rows, :].astype(BF16), preferred_element_type=F32)
    normed = acc * lax.rsqrt(jnp.mean(acc * acc, axis=-1, keepdims=True) + NORM_EPS) * pg_ref[...]
    if final:
        (o_ref,) = outs
        o_ref[...] = normed
    else:
        o_ref, h_ref = outs
        o_ref[...] = acc
        h_ref[...] = normed.astype(h_ref.dtype)


def _outproj(ya, yl, yc, proj_b, out_norm_g, w_stack, layer, x2d, post_g, final):
    m, d = x2d.shape
    tm = OUTPROJ_TM
    gw = GROUP_WIDTH
    gate = lambda sl: pl.BlockSpec((tm, gw), lambda i: (i, _bf16_slice(sl)))
    row_tile = pl.BlockSpec((tm, d), lambda i: (i, 0))
    x_out = jax.ShapeDtypeStruct((m, d), F32)
    return pl.pallas_call(
        functools.partial(_outproj_kernel, final=final),
        out_shape=x_out if final else (x_out, jax.ShapeDtypeStruct((m, d), BF16)),
        grid=(m // tm,),
        in_specs=[
            pl.BlockSpec((tm, gw), lambda i: (i, 0)),
            pl.BlockSpec((tm, 3 * gw), lambda i: (i, 0)),
            pl.BlockSpec((tm, gw), lambda i: (i, 0)),
            gate(SL_A_G), gate(SL_B_G), gate(SL_C_G), gate(SL_D_G), gate(SL_X_G),
            pl.BlockSpec((1, MIX_WIDTH), lambda i: (0, 0)),
            pl.BlockSpec((None, MIX_WIDTH, d), lambda i: (layer, 0, 0), pipeline_mode=pl.Buffered(1)),
            row_tile,
            pl.BlockSpec((1, d), lambda i: (0, 0)),
        ],
        out_specs=row_tile if final else (row_tile, row_tile),
        compiler_params=_params("parallel", fuse_inputs=[False] * 8 + [True, False, False, True]),
        name="outproj_final" if final else "outproj",
    )(ya, yl, yc, proj_b, proj_b, proj_b, proj_b, proj_b,
      out_norm_g.reshape(1, MIX_WIDTH), w_stack, x2d, post_g.reshape(1, d))


def _rotary_tables(seq):
    half = HEAD_DIM // 2
    inv_freq = ROPE_THETA ** (-np.arange(half, dtype=np.float64) / half)
    ang = np.arange(seq, dtype=np.float64)[:, None] * inv_freq[None, :]
    cos, sin = np.cos(ang), np.sin(ang)
    return (jnp.asarray(np.concatenate([cos, cos], axis=-1), F32),
            jnp.asarray(np.concatenate([-sin, sin], axis=-1), F32))


def kernel(x, mem, norm_g, w_in, sgu_w, sgu_b, sgu_ln_g, sgu_ln_b, conv_w, mem_norm_g, w_mem_kv,
           out_norm_g, w_out, final_norm_g):
    batch, seq, d = x.shape
    depth = w_in.shape[0]
    cos_t, sin_t = _rotary_tables(seq)
    x2d = x.reshape(batch * seq, d)
    h2d = _rmsnorm(x2d, norm_g[0])
    memkv = _memkv(mem, mem_norm_g, w_mem_kv)
    flat = lambda t: t.reshape(batch * seq, t.shape[-1])
    for i in range(depth):
        final = i == depth - 1
        proj_f, proj_b = _inproj(h2d, w_in, i, cos_t, sin_t, seq)
        proj_a = proj_f.reshape(batch, seq, proj_f.shape[-1])
        proj_r = proj_b.reshape(batch, seq, proj_b.shape[-1])
        ya = _dilated(proj_a, batch, seq)
        yc = _moba(proj_r, batch, seq)
        yl = _local(proj_r, memkv, i, sgu_w[i], sgu_b[i].T, sgu_ln_g[i].reshape(1, GROUP_WIDTH),
                    sgu_ln_b[i].reshape(1, GROUP_WIDTH), conv_w[i], batch, seq)
        out = _outproj(flat(ya), flat(yl), flat(yc), proj_b, out_norm_g[i], w_out, i, x2d,
                       final_norm_g if final else norm_g[i + 1], final)
        if final:
            x2d = out
        else:
            x2d, h2d = out
    return x2d.reshape(batch, seq, d)
```

```python
import functools

import jax
import jax.numpy as jnp
import numpy as np
from jax import lax
from jax.experimental import pallas as pl
from jax.experimental.pallas import tpu as pltpu

F32 = jnp.float32
BF16 = jnp.bfloat16

HEAD_DIM = 128
GROUP_HEADS = 4
GROUP_WIDTH = GROUP_HEADS * HEAD_DIM
N_GROUPS = 5
MIX_WIDTH = N_GROUPS * GROUP_WIDTH
ROPE_THETA = 10000.0
NORM_EPS = 1e-6
NEG_INF = -1e30
ATTN_SCALE = HEAD_DIM ** -0.5
LOG2E = 1.4426950408889634
EXP2_SCALE = ATTN_SCALE * LOG2E

DIL_BAND = 128
DIL_STRIDES = (1, 4, 16)
DIL_SUPER = DIL_BAND * DIL_STRIDES[-1]
SGU_CHUNK = 128
MOBA_BLOCK = 256
MOBA_TOPK = 3
CONV_WIDTH = 3
CONV_HALO = 16

SL_A_Q, SL_A_K, SL_A_V, SL_A_G = 0, 1, 2, 3
SL_B_U, SL_B_V, SL_B_G = 4, 5, 6
SL_C_Q, SL_C_K, SL_C_V, SL_C_G = 7, 8, 9, 10
SL_D_B, SL_D_C, SL_D_H, SL_D_G = 11, 12, 13, 14
SL_X_Q, SL_X_G = 15, 16
N_F32_SLICES = 3


def _bf16_slice(sl):
    assert sl >= N_F32_SLICES
    return sl - N_F32_SLICES


VMEM_LIMIT_BYTES = 56 * 1024 * 1024


def _params(*semantics, fuse_inputs=None):
    return pltpu.CompilerParams(dimension_semantics=semantics, vmem_limit_bytes=VMEM_LIMIT_BYTES,
                                allow_input_fusion=fuse_inputs)


def _dot_nt(a, b):
    return lax.dot_general(a, b, (((1,), (1,)), ((), ())), preferred_element_type=F32)


NORM_TM = 512


def _rmsnorm_kernel(x_ref, g_ref, o_ref):
    x = x_ref[...]
    y = x * lax.rsqrt(jnp.mean(x * x, axis=-1, keepdims=True) + NORM_EPS) * g_ref[...]
    o_ref[...] = y.astype(o_ref.dtype)


def _rmsnorm(x2d, g):
    m, d = x2d.shape
    return pl.pallas_call(
        _rmsnorm_kernel,
        out_shape=jax.ShapeDtypeStruct((m, d), BF16),
        grid=(m // NORM_TM,),
        in_specs=[pl.BlockSpec((NORM_TM, d), lambda i: (i, 0)), pl.BlockSpec((1, d), lambda i: (0, 0))],
        out_specs=pl.BlockSpec((NORM_TM, d), lambda i: (i, 0)),
        compiler_params=_params("parallel", fuse_inputs=[False, True]),
        name="rmsnorm",
    )(x2d, g.reshape(1, d))


INPROJ_TM = 2048
INPROJ_ROW_CHUNKS = 16


def _inproj_kernel(h_ref, w_ref, cos_ref, sin_ref, of_ref, ob_ref):
    n = pl.program_id(1)
    chunk_rows = INPROJ_TM // INPROJ_ROW_CHUNKS

    def project(slices, finish):
        o_ref = of_ref if slices[0] < N_F32_SLICES else ob_ref
        assert all((s < N_F32_SLICES) == (slices[0] < N_F32_SLICES) for s in slices)

        @pl.when(functools.reduce(jnp.logical_or, [n == s for s in slices]))
        def _():
            w = w_ref[...].astype(BF16)
            for r in range(INPROJ_ROW_CHUNKS):
                rows = slice(r * chunk_rows, (r + 1) * chunk_rows)
                acc = jnp.dot(h_ref[rows, :], w, preferred_element_type=F32)
                if finish == "rotary":
                    cos = cos_ref[rows, :]
                    sin = sin_ref[rows, :]
                    for hh in range(GROUP_HEADS):
                        sl = slice(hh * HEAD_DIM, (hh + 1) * HEAD_DIM)
                        a = acc[:, sl]
                        o_ref[rows, sl] = (a * cos + pltpu.roll(a, HEAD_DIM // 2, 1) * sin).astype(o_ref.dtype)
                else:
                    if finish is not None:
                        acc = finish(acc)
                    o_ref[rows, :] = acc.astype(o_ref.dtype)

    project((SL_A_Q, SL_A_K), "rotary")
    project((SL_A_V,), None)
    project((SL_C_Q, SL_C_K), "rotary")
    project((SL_A_G, SL_B_G, SL_C_G, SL_D_G, SL_X_G), jax.nn.silu)
    project((SL_B_U, SL_B_V, SL_C_V, SL_D_B, SL_D_C, SL_D_H, SL_X_Q), None)


def _inproj(h2d, w_stack, layer, cos_t, sin_t, seq):
    m, d = h2d.shape
    gw = GROUP_WIDTH
    n_slices = w_stack.shape[2] // gw
    tm = INPROJ_TM
    seq_tiles = seq // tm
    return pl.pallas_call(
        _inproj_kernel,
        out_shape=(jax.ShapeDtypeStruct((m, N_F32_SLICES * gw), F32),
                   jax.ShapeDtypeStruct((m, (n_slices - N_F32_SLICES) * gw), BF16)),
        grid=(m // tm, n_slices),
        in_specs=[
            pl.BlockSpec((tm, d), lambda i, n: (i, 0)),
            pl.BlockSpec((None, d, gw), lambda i, n: (layer, 0, n)),
            pl.BlockSpec((tm, HEAD_DIM), lambda i, n: (i % seq_tiles, 0)),
            pl.BlockSpec((tm, HEAD_DIM), lambda i, n: (i % seq_tiles, 0)),
        ],
        out_specs=(pl.BlockSpec((tm, gw), lambda i, n: (i, jnp.minimum(n, N_F32_SLICES - 1))),
                   pl.BlockSpec((tm, gw), lambda i, n: (i, jnp.maximum(n - N_F32_SLICES, 0)))),
        compiler_params=_params("parallel", "arbitrary"),
        name="inproj",
    )(h2d, w_stack, cos_t, sin_t)


def _dilated_kernel(q_ref, kp_ref, ko_ref, vp_ref, vo_ref, o_ref,
                    q4, k4, v4, o1, o2, o3, l1, l2, l3):
    sup = DIL_SUPER
    band = DIL_BAND
    mid = DIL_STRIDES[1]
    assert DIL_STRIDES == (1, mid, mid * mid)
    per = sup // mid
    has_prev = pl.program_id(1) > 0

    for r in range(mid):
        q4[r * per:(r + 1) * per, :] = q_ref[pl.ds(r, per, stride=mid), :]
        for dst, prev_ref, own_ref in ((k4, kp_ref, ko_ref), (v4, vp_ref, vo_ref)):
            dst[2 * r * per:(2 * r + 1) * per, :] = prev_ref[pl.ds(r, per, stride=mid), :]
            dst[(2 * r + 1) * per:(2 * r + 2) * per, :] = own_ref[pl.ds(r, per, stride=mid), :]

    qi = lax.broadcasted_iota(jnp.int32, (band, 2 * band), 0)
    kj = lax.broadcasted_iota(jnp.int32, (band, 2 * band), 1)
    in_band = (kj >= qi) & (kj <= qi + band)
    bias_any = jnp.where(in_band, 0.0, NEG_INF).astype(F32)
    bias_start = jnp.where(in_band & (kj >= band), 0.0, NEG_INF).astype(F32)
    at_start = jnp.logical_not(has_prev)
    ones = jnp.ones((2 * band, HEAD_DIM), BF16)

    def attend(qb, kb, vb, first_band):
        bias = jnp.where(at_start, bias_start, bias_any) if first_band else bias_any
        t = _dot_nt(qb.astype(BF16), kb.astype(BF16)) * EXP2_SCALE + bias
        m = jnp.max(t, axis=-1, keepdims=True)
        p = jnp.exp2(t - m)
        va = jnp.concatenate([vb.astype(BF16), ones], axis=1)
        r = jnp.dot(p.astype(BF16), va, preferred_element_type=F32)
        l = r[:, HEAD_DIM:]
        return r[:, :HEAD_DIM] / l, m + jnp.log(l) * LOG2E

    for n in range(sup // band):
        rows = slice(n * band, (n + 1) * band)
        if n == 0:
            kb = jnp.concatenate([kp_ref[sup - band:sup, :], ko_ref[0:band, :]], axis=0)
            vb = jnp.concatenate([vp_ref[sup - band:sup, :], vo_ref[0:band, :]], axis=0)
        else:
            kb = ko_ref[(n - 1) * band:(n + 1) * band, :]
            vb = vo_ref[(n - 1) * band:(n + 1) * band, :]
        o1[rows, :], l1[rows, :] = attend(q_ref[rows, :], kb, vb, n == 0)

    for r in range(mid):
        for n in range(per // band):
            rows = slice(r * per + n * band, r * per + (n + 1) * band)
            keys = slice((2 * r + 1) * per + (n - 1) * band, (2 * r + 1) * per + (n + 1) * band)
            o2[rows, :], l2[rows, :] = attend(q4[rows, :], k4[keys, :], v4[keys, :], n == 0)

    for r in range(mid):
        for a in range(mid):
            rows = pl.ds(r * per + a, band, stride=mid)
            keys = pl.ds(2 * r * per + a, 2 * band, stride=mid)
            o3[rows, :], l3[rows, :] = attend(q4[rows, :], k4[keys, :], v4[keys, :], True)

    rows_per_step = 256

    def merge(i, c):
        r = i // (per // rows_per_step)
        l0 = (i % (per // rows_per_step)) * rows_per_step
        rows = pl.ds(pl.multiple_of(r * per + l0, rows_per_step), rows_per_step)
        positions = pl.ds(l0 * mid + r, rows_per_step, stride=mid)
        a1, a2, a3 = l1[positions, :], l2[rows, :], l3[rows, :]
        mx = jnp.maximum(jnp.maximum(a1, a2), a3)
        w1, w2, w3 = jnp.exp2(a1 - mx), jnp.exp2(a2 - mx), jnp.exp2(a3 - mx)
        num = w1 * o1[positions, :] + w2 * o2[rows, :] + w3 * o3[rows, :]
        o_ref[positions, :] = num / (w1 + w2 + w3)
        return c
    lax.fori_loop(0, sup // rows_per_step, merge, 0)


def _dilated(proj, batch, seq):
    sup = DIL_SUPER
    blk = (None, sup, HEAD_DIM)
    prev = lambda sb: jnp.maximum(sb - 1, 0)
    scr = pltpu.VMEM((sup, HEAD_DIM), F32)
    win = pltpu.VMEM((2 * sup, HEAD_DIM), F32)
    return pl.pallas_call(
        _dilated_kernel,
        out_shape=jax.ShapeDtypeStruct((batch, seq, GROUP_WIDTH), F32),
        grid=(batch, seq // sup, GROUP_HEADS),
        in_specs=[
            pl.BlockSpec(blk, lambda b, sb, h: (b, sb, SL_A_Q * GROUP_HEADS + h)),
            pl.BlockSpec(blk, lambda b, sb, h: (b, prev(sb), SL_A_K * GROUP_HEADS + h)),
            pl.BlockSpec(blk, lambda b, sb, h: (b, sb, SL_A_K * GROUP_HEADS + h)),
            pl.BlockSpec(blk, lambda b, sb, h: (b, prev(sb), SL_A_V * GROUP_HEADS + h)),
            pl.BlockSpec(blk, lambda b, sb, h: (b, sb, SL_A_V * GROUP_HEADS + h)),
        ],
        out_specs=pl.BlockSpec(blk, lambda b, sb, h: (b, sb, h)),
        scratch_shapes=[scr, win, win, scr, scr, scr, scr, scr, scr],
        compiler_params=_params("parallel", "parallel", "parallel"),
        name="dilated",
    )(proj, proj, proj, proj, proj)


MOBA_TQ = 2 * MOBA_BLOCK
MOBA_BLOCK_SHIFT = 8
assert 1 << MOBA_BLOCK_SHIFT == MOBA_BLOCK
MOBA_PAST_STEPS_PER_TRIP = 6


def _moba_kernel(q_ref, k_ref, v_ref, o_ref, kaug, vaug, kmean, qaug, m_all, acc_all, s_a, s_b):
    blk = MOBA_BLOCK
    tq = MOBA_TQ
    hd = HEAD_DIM
    seq = k_ref.shape[0]
    n_blocks = seq // blk
    n_tiles = seq // tq

    for n in range(n_blocks):
        rows = slice(n * blk, (n + 1) * blk)
        kb = k_ref[rows, :]
        kmean[n:n + 1, :] = jnp.mean(kb.astype(F32), axis=0, keepdims=True)
        kaug[rows, 0:hd] = kb
        vaug[rows, 0:hd] = v_ref[rows, :]

    @pl.when(jnp.logical_and(pl.program_id(0) == 0, pl.program_id(1) == 0))
    def _():
        lane = lax.broadcasted_iota(jnp.int32, (blk, hd), 1)
        ones = jnp.ones((blk, hd), BF16)
        for n in range(n_blocks):
            rows = slice(n * blk, (n + 1) * blk)
            kaug[rows, hd:2 * hd] = jnp.where(lane == n, 1.0, 0.0).astype(BF16)
            vaug[rows, hd:2 * hd] = ones

    def tile_rows(t):
        return pl.ds(pl.multiple_of(t * tq, tq), tq)

    km = kmean[...]
    km_hi = km.astype(BF16)
    rest = km - km_hi.astype(F32)
    km_mid = rest.astype(BF16)
    km_lo = (rest - km_mid.astype(F32)).astype(BF16)
    parts = _dot_nt(jnp.concatenate([km_hi, km_mid, km_lo], axis=0), q_ref[...])
    gate = parts[0:n_blocks] + parts[n_blocks:2 * n_blocks] + parts[2 * n_blocks:3 * n_blocks]
    bidx = lax.broadcasted_iota(jnp.int32, (n_blocks, seq), 0)
    bidx_f = bidx.astype(F32)
    qblk = lax.shift_right_logical(lax.broadcasted_iota(jnp.int32, (n_blocks, seq), 1), MOBA_BLOCK_SHIFT)
    g = jnp.where(bidx < qblk, gate, NEG_INF)
    sel_bias = jnp.full((n_blocks, seq), NEG_INF, F32)
    for k in range(MOBA_TOPK):
        mx = jnp.max(g, axis=0, keepdims=True)
        first = jnp.min(jnp.where(g == mx, bidx_f, float(n_blocks)), axis=0, keepdims=True)
        hit = bidx_f == first
        sel_bias = jnp.where(hit, jnp.where(k < qblk, 0.0, NEG_INF), sel_bias)
        g = jnp.where(hit, -jnp.inf, g)
    sel_bias = jnp.where(bidx == qblk, 0.0, sel_bias)
    sel_bias = jnp.concatenate([sel_bias, jnp.zeros((hd - n_blocks, seq), F32)], axis=0)
    qaug[:, 0:hd] = q_ref[...]
    for t in range(n_tiles):
        rows = slice(t * tq, (t + 1) * tq)
        qaug[rows, hd:2 * hd] = sel_bias[:, rows].T.astype(BF16)

    def key_rows(c, width):
        return pl.ds(pl.multiple_of(c * (width * tq), width * tq), width * tq)

    def scores(t, c, width):
        return _dot_nt(qaug[tile_rows(t), :], kaug[key_rows(c, width), :])

    def run_steps(first, advance, n_steps, step, per_trip, width):
        assert n_steps % per_trip == 0 and per_trip % 2 == 0
        cols = slice(0, width * tq)
        bufs = (s_a, s_b)
        s_a[:, cols] = scores(*first, width)

        def trip(j, tc):
            for i in range(per_trip):
                nxt = advance(*tc)
                bufs[(i + 1) % 2][:, cols] = scores(jnp.minimum(nxt[0], n_tiles - 1),
                                                    jnp.minimum(nxt[1], n_tiles // width - 1), width)
                step(bufs[i % 2].at[:, cols], *tc, width)
                tc = nxt
            return tc
        lax.fori_loop(0, n_steps // per_trip, trip, tuple(jnp.int32(v) for v in first))

    def diagonal_step(s_ref, t, c, width):
        r_i = lax.broadcasted_iota(jnp.int32, (tq, width * tq), 0)
        c_i = lax.broadcasted_iota(jnp.int32, (tq, width * tq), 1)
        s = jnp.where(c_i <= r_i + (width - 1) * tq, s_ref[...], NEG_INF)
        m = jnp.max(s, axis=-1, keepdims=True)
        p = jnp.exp2((s - m) * EXP2_SCALE)
        m_all[tile_rows(t), :] = m
        acc_all[tile_rows(t), :] = jnp.dot(p.astype(BF16), vaug[key_rows(c, width), :],
                                           preferred_element_type=F32)

    def past_step(s_ref, t, c, width):
        s = s_ref[...]
        m_old = m_all[tile_rows(t), :]
        m_new = jnp.maximum(m_old, jnp.max(s, axis=-1, keepdims=True))
        alpha = jnp.exp2((m_old - m_new) * EXP2_SCALE)
        p = jnp.exp2((s - m_new) * EXP2_SCALE)
        m_all[tile_rows(t), :] = m_new
        acc_all[tile_rows(t), :] = alpha * acc_all[tile_rows(t), :] + jnp.dot(
            p.astype(BF16), vaug[key_rows(c, width), :], preferred_element_type=F32)

    def next_past(t, c):
        wrap = c + 1 == lax.shift_right_logical(t, 1)
        return jnp.where(wrap, t + 1, t), jnp.where(wrap, 0, c + 1)

    first_steps = [(t, t, 1) if t % 2 == 0 else (t, t // 2, 2) for t in range(n_tiles)]
    bufs = (s_a, s_b)
    t0, c0, w0 = first_steps[0]
    s_a[:, 0:w0 * tq] = scores(t0, c0, w0)
    for i, (t, c, w) in enumerate(first_steps):
        if i + 1 < len(first_steps):
            tn, cn, wn = first_steps[i + 1]
            bufs[(i + 1) % 2][:, 0:wn * tq] = scores(tn, cn, wn)
        diagonal_step(bufs[i % 2].at[:, 0:w * tq], t, c, w)
    run_steps((2, 0), next_past, sum(t // 2 for t in range(n_tiles)), past_step, MOBA_PAST_STEPS_PER_TRIP, 2)

    def finish(t, carry):
        acc = acc_all[tile_rows(t), :]
        o_ref[tile_rows(t), :] = (acc[:, 0:hd] / acc[:, hd:2 * hd]).astype(o_ref.dtype)
        return carry
    lax.fori_loop(0, n_tiles, finish, 0)


def _moba(proj, batch, seq):
    tq = MOBA_TQ
    hd = HEAD_DIM
    full = lambda sl: pl.BlockSpec((None, seq, hd), lambda b, h: (b, 0, _bf16_slice(sl) * GROUP_HEADS + h))
    return pl.pallas_call(
        _moba_kernel,
        out_shape=jax.ShapeDtypeStruct((batch, seq, GROUP_WIDTH), BF16),
        grid=(batch, GROUP_HEADS),
        in_specs=[full(SL_C_Q), full(SL_C_K), full(SL_C_V)],
        out_specs=pl.BlockSpec((None, seq, hd), lambda b, h: (b, 0, h)),
        scratch_shapes=[
            pltpu.VMEM((seq, 2 * hd), BF16),
            pltpu.VMEM((seq, 2 * hd), BF16),
            pltpu.VMEM((seq // MOBA_BLOCK, hd), F32),
            pltpu.VMEM((seq, 2 * hd), BF16),
            pltpu.VMEM((seq, 1), F32),
            pltpu.VMEM((seq, 2 * hd), F32),
            pltpu.VMEM((tq, 2 * tq), F32),
            pltpu.VMEM((tq, 2 * tq), F32),
        ],
        compiler_params=_params("arbitrary", "arbitrary"),
        name="moba",
    )(proj, proj, proj)


def _memkv_kernel(mem_ref, g_ref, w_ref, o_ref):
    x = mem_ref[...]
    y = x * lax.rsqrt(jnp.mean(x * x, axis=-1, keepdims=True) + NORM_EPS) * g_ref[...]
    o_ref[...] = jnp.dot(y.astype(BF16), w_ref[...].astype(BF16),
                         preferred_element_type=F32).astype(o_ref.dtype)


def _memkv(mem, mem_norm_g, w_stack):
    batch, n_mem, d = mem.shape
    depth, _, n_cols = w_stack.shape
    return pl.pallas_call(
        _memkv_kernel,
        out_shape=jax.ShapeDtypeStruct((depth, batch, n_mem, n_cols), BF16),
        grid=(depth, batch),
        in_specs=[
            pl.BlockSpec((None, n_mem, d), lambda i, b: (b, 0, 0)),
            pl.BlockSpec((None, 1, d), lambda i, b: (i, 0, 0)),
            pl.BlockSpec((None, d, n_cols), lambda i, b: (i, 0, 0)),
        ],
        out_specs=pl.BlockSpec((None, None, n_mem, n_cols), lambda i, b: (i, b, 0, 0)),
        compiler_params=_params("parallel", "parallel", fuse_inputs=[False, True, False]),
        name="memkv",
    )(mem, mem_norm_g.reshape(depth, 1, d), w_stack)


LOCAL_TM = 512


def _local_kernel(bu_ref, bv_ref, db_ref, dc_ref, dh_ref, hc_ref, hh_ref, xq_ref,
                  ws_ref, bs_ref, lng_ref, lnb_ref, cw_ref, kv_ref, o_ref, zbuf):
    tm = LOCAL_TM
    gw = GROUP_WIDTH

    u = jax.nn.gelu(bu_ref[...].astype(F32))
    v = jax.nn.gelu(bv_ref[...].astype(F32))
    mu = jnp.mean(v, axis=-1, keepdims=True)
    vc = v - mu
    var = jnp.mean(vc * vc, axis=-1, keepdims=True)
    vnorm = (vc * lax.rsqrt(var + NORM_EPS) * lng_ref[...] + lnb_ref[...]).astype(BF16)
    r = lax.broadcasted_iota(jnp.int32, (SGU_CHUNK, SGU_CHUNK), 0)
    c = lax.broadcasted_iota(jnp.int32, (SGU_CHUNK, SGU_CHUNK), 1)
    for hd in range(GROUP_HEADS):
        cols = slice(hd * HEAD_DIM, (hd + 1) * HEAD_DIM)
        w_causal = jnp.where(c <= r, ws_ref[hd], 0.0).astype(BF16)
        bias = bs_ref[:, hd:hd + 1]
        for ch in range(tm // SGU_CHUNK):
            rows = slice(ch * SGU_CHUNK, (ch + 1) * SGU_CHUNK)
            mixed = jnp.dot(w_causal, vnorm[rows, cols], preferred_element_type=F32) + bias
            o_ref[rows, cols] = (u[rows, cols] * mixed).astype(o_ref.dtype)

    first_tile = pl.program_id(1) == 0
    zbuf[0:CONV_HALO, :] = jnp.where(first_tile, 0.0, hc_ref[...].astype(F32) * hh_ref[...].astype(F32))
    zbuf[CONV_HALO:CONV_HALO + tm, :] = dc_ref[...].astype(F32) * dh_ref[...].astype(F32)
    y = cw_ref[CONV_WIDTH - 1:CONV_WIDTH, :] * zbuf[CONV_HALO:CONV_HALO + tm, :]
    for tap in range(CONV_WIDTH - 1):
        back = CONV_WIDTH - 1 - tap
        y = y + cw_ref[tap:tap + 1, :] * zbuf[CONV_HALO - back:CONV_HALO - back + tm, :]
    o_ref[:, gw:2 * gw] = (db_ref[...].astype(F32) * y).astype(o_ref.dtype)

    for hd in range(GROUP_HEADS):
        cols = slice(hd * HEAD_DIM, (hd + 1) * HEAD_DIM)
        kh = kv_ref[:, hd * HEAD_DIM:(hd + 1) * HEAD_DIM]
        vh = kv_ref[:, gw + hd * HEAD_DIM:gw + (hd + 1) * HEAD_DIM]
        s = _dot_nt(xq_ref[:, cols], kh) * ATTN_SCALE
        m = jnp.max(s, axis=-1, keepdims=True)
        p = jnp.exp(s - m)
        l = jnp.sum(p, axis=-1, keepdims=True)
        o = jnp.dot(p.astype(BF16), vh, preferred_element_type=F32) / l
        o_ref[:, 2 * gw + hd * HEAD_DIM:2 * gw + (hd + 1) * HEAD_DIM] = o.astype(o_ref.dtype)


def _local(proj, memkv, layer, sgu_w, sgu_b_t, ln_g, ln_b, conv_w, batch, seq):
    tm = LOCAL_TM
    gw = GROUP_WIDTH
    halo_per_tile = tm // CONV_HALO
    tile = lambda sl: pl.BlockSpec((None, tm, gw), lambda b, i: (b, i, _bf16_slice(sl)))
    halo = lambda sl: pl.BlockSpec(
        (None, CONV_HALO, gw), lambda b, i: (b, jnp.maximum(i * halo_per_tile - 1, 0), _bf16_slice(sl)))
    whole = lambda a: pl.BlockSpec(a.shape, lambda b, i: (0,) * a.ndim)
    n_mem = memkv.shape[2]
    return pl.pallas_call(
        _local_kernel,
        out_shape=jax.ShapeDtypeStruct((batch, seq, 3 * gw), BF16),
        grid=(batch, seq // tm),
        in_specs=[
            tile(SL_B_U), tile(SL_B_V), tile(SL_D_B), tile(SL_D_C), tile(SL_D_H),
            halo(SL_D_C), halo(SL_D_H), tile(SL_X_Q),
            whole(sgu_w), whole(sgu_b_t), whole(ln_g), whole(ln_b), whole(conv_w),
            pl.BlockSpec((None, None, n_mem, 2 * gw), lambda b, i: (layer, b, 0, 0)),
        ],
        out_specs=pl.BlockSpec((None, tm, 3 * gw), lambda b, i: (b, i, 0)),
        scratch_shapes=[pltpu.VMEM((CONV_HALO + tm, gw), F32)],
        compiler_params=_params("parallel", "parallel", fuse_inputs=[False] * 8 + [True] * 5 + [False]),
        name="local",
    )(proj, proj, proj, proj, proj, proj, proj, proj,
      sgu_w, sgu_b_t, ln_g, ln_b, conv_w, memkv)


OUTPROJ_TM = 256


def _outproj_kernel(ya_ref, yl_ref, yc_ref, ga_ref, gb_ref, gc_ref, gd_ref, gx_ref,
                    ng_ref, w_ref, x_ref, pg_ref, *outs, final):
    gw = GROUP_WIDTH
    branches = (
        (lambda: ya_ref[...], ga_ref),
        (lambda: yl_ref[:, 0:gw], gb_ref),
        (lambda: yc_ref[...], gc_ref),
        (lambda: yl_ref[:, gw:2 * gw], gd_ref),
        (lambda: yl_ref[:, 2 * gw:3 * gw], gx_ref),
    )
    acc = x_ref[...]
    for grp, (o, gate_ref) in enumerate(branches):
        rows = slice(grp * gw, (grp + 1) * gw)
        y = o().astype(F32) * gate_ref[...].astype(F32)
        y = y * lax.rsqrt(jnp.mean(y * y, axis=-1, keepdims=True) + NORM_EPS)
        y = (y * ng_ref[:, rows]).astype(BF16)
        acc = acc + jnp.dot(y, w_ref[rows, :].astype(BF16), preferred_element_type=F32)
    normed = acc * lax.rsqrt(jnp.mean(acc * acc, axis=-1, keepdims=True) + NORM_EPS) * pg_ref[...]
    if final:
        (o_ref,) = outs
        o_ref[...] = normed
    else:
        o_ref, h_ref = outs
        o_ref[...] = acc
        h_ref[...] = normed.astype(h_ref.dtype)


def _outproj(ya, yl, yc, proj_b, out_norm_g, w_stack, layer, x2d, post_g, final):
    m, d = x2d.shape
    tm = OUTPROJ_TM
    gw = GROUP_WIDTH
    gate = lambda sl: pl.BlockSpec((tm, gw), lambda i: (i, _bf16_slice(sl)))
    row_tile = pl.BlockSpec((tm, d), lambda i: (i, 0))
    x_out = jax.ShapeDtypeStruct((m, d), F32)
    return pl.pallas_call(
        functools.partial(_outproj_kernel, final=final),
        out_shape=x_out if final else (x_out, jax.ShapeDtypeStruct((m, d), BF16)),
        grid=(m // tm,),
        in_specs=[
            pl.BlockSpec((tm, gw), lambda i: (i, 0)),
            pl.BlockSpec((tm, 3 * gw), lambda i: (i, 0)),
            pl.BlockSpec((tm, gw), lambda i: (i, 0)),
            gate(SL_A_G), gate(SL_B_G), gate(SL_C_G), gate(SL_D_G), gate(SL_X_G),
            pl.BlockSpec((1, MIX_WIDTH), lambda i: (0, 0)),
            pl.BlockSpec((None, MIX_WIDTH, d), lambda i: (layer, 0, 0), pipeline_mode=pl.Buffered(1)),
            row_tile,
            pl.BlockSpec((1, d), lambda i: (0, 0)),
        ],
        out_specs=row_tile if final else (row_tile, row_tile),
        compiler_params=_params("parallel", fuse_inputs=[False] * 8 + [True, False, False, True]),
        name="outproj_final" if final else "outproj",
    )(ya, yl, yc, proj_b, proj_b, proj_b, proj_b, proj_b,
      out_norm_g.reshape(1, MIX_WIDTH), w_stack, x2d, post_g.reshape(1, d))


def _rotary_tables(seq):
    half = HEAD_DIM // 2
    inv_freq = ROPE_THETA ** (-np.arange(half, dtype=np.float64) / half)
    ang = np.arange(seq, dtype=np.float64)[:, None] * inv_freq[None, :]
    cos, sin = np.cos(ang), np.sin(ang)
    return (jnp.asarray(np.concatenate([cos, cos], axis=-1), F32),
            jnp.asarray(np.concatenate([-sin, sin], axis=-1), F32))


def kernel(x, mem, norm_g, w_in, sgu_w, sgu_b, sgu_ln_g, sgu_ln_b, conv_w, mem_norm_g, w_mem_kv,
           out_norm_g, w_out, final_norm_g):
    batch, seq, d = x.shape
    depth = w_in.shape[0]
    cos_t, sin_t = _rotary_tables(seq)
    x2d = x.reshape(batch * seq, d)
    h2d = _rmsnorm(x2d, norm_g[0])
    memkv = _memkv(mem, mem_norm_g, w_mem_kv)
    flat = lambda t: t.reshape(batch * seq, t.shape[-1])
    for i in range(depth):
        final = i == depth - 1
        proj_f, proj_b = _inproj(h2d, w_in, i, cos_t, sin_t, seq)
        proj_a = proj_f.reshape(batch, seq, proj_f.shape[-1])
        proj_r = proj_b.reshape(batch, seq, proj_b.shape[-1])
        ya = _dilated(proj_a, batch, seq)
        yc = _moba(proj_r, batch, seq)
        yl = _local(proj_r, memkv, i, sgu_w[i], sgu_b[i].T, sgu_ln_g[i].reshape(1, GROUP_WIDTH),
                    sgu_ln_b[i].reshape(1, GROUP_WIDTH), conv_w[i], batch, seq)
        out = _outproj(flat(ya), flat(yl), flat(yc), proj_b, out_norm_g[i], w_out, i, x2d,
                       final_norm_g if final else norm_g[i + 1], final)
        if final:
            x2d = out
        else:
            x2d, h2d = out
    return x2d.reshape(batch, seq, d)
```

```python
import functools

import jax
import jax.numpy as jnp
import numpy as np
from jax import lax
from jax.experimental import pallas as pl
from jax.experimental.pallas import tpu as pltpu

F32 = jnp.float32
BF16 = jnp.bfloat16

HEAD_DIM = 128
GROUP_HEADS = 4
GROUP_WIDTH = GROUP_HEADS * HEAD_DIM
N_GROUPS = 5
MIX_WIDTH = N_GROUPS * GROUP_WIDTH
ROPE_THETA = 10000.0
NORM_EPS = 1e-6
NEG_INF = -1e30
ATTN_SCALE = HEAD_DIM ** -0.5
LOG2E = 1.4426950408889634
EXP2_SCALE = ATTN_SCALE * LOG2E

DIL_BAND = 128
DIL_STRIDES = (1, 4, 16)
DIL_SUPER = DIL_BAND * DIL_STRIDES[-1]
SGU_CHUNK = 128
MOBA_BLOCK = 256
MOBA_TOPK = 3
CONV_WIDTH = 3
CONV_HALO = 16

SL_A_Q, SL_A_K, SL_A_V, SL_A_G = 0, 1, 2, 3
SL_B_U, SL_B_V, SL_B_G = 4, 5, 6
SL_C_Q, SL_C_K, SL_C_V, SL_C_G = 7, 8, 9, 10
SL_D_B, SL_D_C, SL_D_H, SL_D_G = 11, 12, 13, 14
SL_X_Q, SL_X_G = 15, 16
N_F32_SLICES = 3


def _bf16_slice(sl):
    assert sl >= N_F32_SLICES
    return sl - N_F32_SLICES


VMEM_LIMIT_BYTES = 56 * 1024 * 1024


def _params(*semantics, fuse_inputs=None):
    return pltpu.CompilerParams(dimension_semantics=semantics, vmem_limit_bytes=VMEM_LIMIT_BYTES,
                                allow_input_fusion=fuse_inputs)


def _dot_nt(a, b):
    return lax.dot_general(a, b, (((1,), (1,)), ((), ())), preferred_element_type=F32)


NORM_TM = 512


def _rmsnorm_kernel(x_ref, g_ref, o_ref):
    x = x_ref[...]
    y = x * lax.rsqrt(jnp.mean(x * x, axis=-1, keepdims=True) + NORM_EPS) * g_ref[...]
    o_ref[...] = y.astype(o_ref.dtype)


def _rmsnorm(x2d, g):
    m, d = x2d.shape
    return pl.pallas_call(
        _rmsnorm_kernel,
        out_shape=jax.ShapeDtypeStruct((m, d), BF16),
        grid=(m // NORM_TM,),
        in_specs=[pl.BlockSpec((NORM_TM, d), lambda i: (i, 0)), pl.BlockSpec((1, d), lambda i: (0, 0))],
        out_specs=pl.BlockSpec((NORM_TM, d), lambda i: (i, 0)),
        compiler_params=_params("parallel", fuse_inputs=[False, True]),
        name="rmsnorm",
    )(x2d, g.reshape(1, d))


INPROJ_TM = 2048
INPROJ_ROW_CHUNKS = 16


def _inproj_kernel(h_ref, w_ref, cos_ref, sin_ref, of_ref, ob_ref):
    n = pl.program_id(1)
    chunk_rows = INPROJ_TM // INPROJ_ROW_CHUNKS

    def project(slices, finish):
        o_ref = of_ref if slices[0] < N_F32_SLICES else ob_ref
        assert all((s < N_F32_SLICES) == (slices[0] < N_F32_SLICES) for s in slices)

        @pl.when(functools.reduce(jnp.logical_or, [n == s for s in slices]))
        def _():
            w = w_ref[...].astype(BF16)
            for r in range(INPROJ_ROW_CHUNKS):
                rows = slice(r * chunk_rows, (r + 1) * chunk_rows)
                acc = jnp.dot(h_ref[rows, :], w, preferred_element_type=F32)
                if finish == "rotary":
                    cos = cos_ref[rows, :]
                    sin = sin_ref[rows, :]
                    for hh in range(GROUP_HEADS):
                        sl = slice(hh * HEAD_DIM, (hh + 1) * HEAD_DIM)
                        a = acc[:, sl]
                        o_ref[rows, sl] = (a * cos + pltpu.roll(a, HEAD_DIM // 2, 1) * sin).astype(o_ref.dtype)
                else:
                    if finish is not None:
                        acc = finish(acc)
                    o_ref[rows, :] = acc.astype(o_ref.dtype)

    project((SL_A_Q, SL_A_K), "rotary")
    project((SL_A_V,), None)
    project((SL_C_Q, SL_C_K), "rotary")
    project((SL_A_G, SL_B_G, SL_C_G, SL_D_G, SL_X_G), jax.nn.silu)
    project((SL_B_U, SL_B_V, SL_C_V, SL_D_B, SL_D_C, SL_D_H, SL_X_Q), None)


def _inproj(h2d, w_stack, layer, cos_t, sin_t, seq):
    m, d = h2d.shape
    gw = GROUP_WIDTH
    n_slices = w_stack.shape[2] // gw
    tm = INPROJ_TM
    seq_tiles = seq // tm
    return pl.pallas_call(
        _inproj_kernel,
        out_shape=(jax.ShapeDtypeStruct((m, N_F32_SLICES * gw), F32),
                   jax.ShapeDtypeStruct((m, (n_slices - N_F32_SLICES) * gw), BF16)),
        grid=(m // tm, n_slices),
        in_specs=[
            pl.BlockSpec((tm, d), lambda i, n: (i, 0)),
            pl.BlockSpec((None, d, gw), lambda i, n: (layer, 0, n)),
            pl.BlockSpec((tm, HEAD_DIM), lambda i, n: (i % seq_tiles, 0)),
            pl.BlockSpec((tm, HEAD_DIM), lambda i, n: (i % seq_tiles, 0)),
        ],
        out_specs=(pl.BlockSpec((tm, gw), lambda i, n: (i, jnp.minimum(n, N_F32_SLICES - 1))),
                   pl.BlockSpec((tm, gw), lambda i, n: (i, jnp.maximum(n - N_F32_SLICES, 0)))),
        compiler_params=_params("parallel", "arbitrary"),
        name="inproj",
    )(h2d, w_stack, cos_t, sin_t)


def _dilated_kernel(q_ref, kp_ref, ko_ref, vp_ref, vo_ref, o_ref,
                    q4, k4, v4, o1, o2, o3, l1, l2, l3):
    sup = DIL_SUPER
    band = DIL_BAND
    mid = DIL_STRIDES[1]
    assert DIL_STRIDES == (1, mid, mid * mid)
    per = sup // mid
    has_prev = pl.program_id(1) > 0

    for r in range(mid):
        q4[r * per:(r + 1) * per, :] = q_ref[pl.ds(r, per, stride=mid), :]
        for dst, prev_ref, own_ref in ((k4, kp_ref, ko_ref), (v4, vp_ref, vo_ref)):
            dst[2 * r * per:(2 * r + 1) * per, :] = prev_ref[pl.ds(r, per, stride=mid), :]
            dst[(2 * r + 1) * per:(2 * r + 2) * per, :] = own_ref[pl.ds(r, per, stride=mid), :]

    qi = lax.broadcasted_iota(jnp.int32, (band, 2 * band), 0)
    kj = lax.broadcasted_iota(jnp.int32, (band, 2 * band), 1)
    in_band = (kj >= qi) & (kj <= qi + band)
    bias_any = jnp.where(in_band, 0.0, NEG_INF).astype(F32)
    bias_start = jnp.where(in_band & (kj >= band), 0.0, NEG_INF).astype(F32)
    at_start = jnp.logical_not(has_prev)
    ones = jnp.ones((2 * band, HEAD_DIM), BF16)

    def attend(qb, kb, vb, first_band):
        bias = jnp.where(at_start, bias_start, bias_any) if first_band else bias_any
        t = _dot_nt(qb.astype(BF16), kb.astype(BF16)) * EXP2_SCALE + bias
        m = jnp.max(t, axis=-1, keepdims=True)
        p = jnp.exp2(t - m)
        va = jnp.concatenate([vb.astype(BF16), ones], axis=1)
        r = jnp.dot(p.astype(BF16), va, preferred_element_type=F32)
        l = r[:, HEAD_DIM:]
        return r[:, :HEAD_DIM] / l, m + jnp.log(l) * LOG2E

    for n in range(sup // band):
        rows = slice(n * band, (n + 1) * band)
        if n == 0:
            kb = jnp.concatenate([kp_ref[sup - band:sup, :], ko_ref[0:band, :]], axis=0)
            vb = jnp.concatenate([vp_ref[sup - band:sup, :], vo_ref[0:band, :]], axis=0)
        else:
            kb = ko_ref[(n - 1) * band:(n + 1) * band, :]
            vb = vo_ref[(n - 1) * band:(n + 1) * band, :]
        o1[rows, :], l1[rows, :] = attend(q_ref[rows, :], kb, vb, n == 0)

    for r in range(mid):
        for n in range(per // band):
            rows = slice(r * per + n * band, r * per + (n + 1) * band)
            keys = slice((2 * r + 1) * per + (n - 1) * band, (2 * r + 1) * per + (n + 1) * band)
            o2[rows, :], l2[rows, :] = attend(q4[rows, :], k4[keys, :], v4[keys, :], n == 0)

    for r in range(mid):
        for a in range(mid):
            rows = pl.ds(r * per + a, band, stride=mid)
            keys = pl.ds(2 * r * per + a, 2 * band, stride=mid)
            o3[rows, :], l3[rows, :] = attend(q4[rows, :], k4[keys, :], v4[keys, :], True)

    rows_per_step = 256

    def merge(i, c):
        r = i // (per // rows_per_step)
        l0 = (i % (per // rows_per_step)) * rows_per_step
        rows = pl.ds(pl.multiple_of(r * per + l0, rows_per_step), rows_per_step)
        positions = pl.ds(l0 * mid + r, rows_per_step, stride=mid)
        a1, a2, a3 = l1[positions, :], l2[rows, :], l3[rows, :]
        mx = jnp.maximum(jnp.maximum(a1, a2), a3)
        w1, w2, w3 = jnp.exp2(a1 - mx), jnp.exp2(a2 - mx), jnp.exp2(a3 - mx)
        num = w1 * o1[positions, :] + w2 * o2[rows, :] + w3 * o3[rows, :]
        o_ref[positions, :] = num / (w1 + w2 + w3)
        return c
    lax.fori_loop(0, sup // rows_per_step, merge, 0)


def _dilated(proj, batch, seq):
    sup = DIL_SUPER
    blk = (None, sup, HEAD_DIM)
    prev = lambda sb: jnp.maximum(sb - 1, 0)
    scr = pltpu.VMEM((sup, HEAD_DIM), F32)
    win = pltpu.VMEM((2 * sup, HEAD_DIM), F32)
    return pl.pallas_call(
        _dilated_kernel,
        out_shape=jax.ShapeDtypeStruct((batch, seq, GROUP_WIDTH), F32),
        grid=(batch, seq // sup, GROUP_HEADS),
        in_specs=[
            pl.BlockSpec(blk, lambda b, sb, h: (b, sb, SL_A_Q * GROUP_HEADS + h)),
            pl.BlockSpec(blk, lambda b, sb, h: (b, prev(sb), SL_A_K * GROUP_HEADS + h)),
            pl.BlockSpec(blk, lambda b, sb, h: (b, sb, SL_A_K * GROUP_HEADS + h)),
            pl.BlockSpec(blk, lambda b, sb, h: (b, prev(sb), SL_A_V * GROUP_HEADS + h)),
            pl.BlockSpec(blk, lambda b, sb, h: (b, sb, SL_A_V * GROUP_HEADS + h)),
        ],
        out_specs=pl.BlockSpec(blk, lambda b, sb, h: (b, sb, h)),
        scratch_shapes=[scr, win, win, scr, scr, scr, scr, scr, scr],
        compiler_params=_params("parallel", "parallel", "parallel"),
        name="dilated",
    )(proj, proj, proj, proj, proj)


MOBA_TQ = 2 * MOBA_BLOCK
MOBA_BLOCK_SHIFT = 8
assert 1 << MOBA_BLOCK_SHIFT == MOBA_BLOCK
MOBA_PAST_STEPS_PER_TRIP = 6


def _moba_kernel(q_ref, k_ref, v_ref, o_ref, kaug, vaug, kmean, qaug, m_all, acc_all, s_a, s_b):
    blk = MOBA_BLOCK
    tq = MOBA_TQ
    hd = HEAD_DIM
    seq = k_ref.shape[0]
    n_blocks = seq // blk
    n_tiles = seq // tq

    for n in range(n_blocks):
        rows = slice(n * blk, (n + 1) * blk)
        kb = k_ref[rows, :]
        kmean[n:n + 1, :] = jnp.mean(kb.astype(F32), axis=0, keepdims=True)
        kaug[rows, 0:hd] = kb
        vaug[rows, 0:hd] = v_ref[rows, :]

    @pl.when(jnp.logical_and(pl.program_id(0) == 0, pl.program_id(1) == 0))
    def _():
        lane = lax.broadcasted_iota(jnp.int32, (blk, hd), 1)
        ones = jnp.ones((blk, hd), BF16)
        for n in range(n_blocks):
            rows = slice(n * blk, (n + 1) * blk)
            kaug[rows, hd:2 * hd] = jnp.where(lane == n, 1.0, 0.0).astype(BF16)
            vaug[rows, hd:2 * hd] = ones

    def tile_rows(t):
        return pl.ds(pl.multiple_of(t * tq, tq), tq)

    km = kmean[...]
    km_hi = km.astype(BF16)
    rest = km - km_hi.astype(F32)
    km_mid = rest.astype(BF16)
    km_lo = (rest - km_mid.astype(F32)).astype(BF16)
    parts = _dot_nt(jnp.concatenate([km_hi, km_mid, km_lo], axis=0), q_ref[...])
    gate = parts[0:n_blocks] + parts[n_blocks:2 * n_blocks] + parts[2 * n_blocks:3 * n_blocks]
    bidx = lax.broadcasted_iota(jnp.int32, (n_blocks, seq), 0)
    bidx_f = bidx.astype(F32)
    qblk = lax.shift_right_logical(lax.broadcasted_iota(jnp.int32, (n_blocks, seq), 1), MOBA_BLOCK_SHIFT)
    g = jnp.where(bidx < qblk, gate, NEG_INF)
    sel_bias = jnp.full((n_blocks, seq), NEG_INF, F32)
    for k in range(MOBA_TOPK):
        mx = jnp.max(g, axis=0, keepdims=True)
        first = jnp.min(jnp.where(g == mx, bidx_f, float(n_blocks)), axis=0, keepdims=True)
        hit = bidx_f == first
        sel_bias = jnp.where(hit, jnp.where(k < qblk, 0.0, NEG_INF), sel_bias)
        g = jnp.where(hit, -jnp.inf, g)
    sel_bias = jnp.where(bidx == qblk, 0.0, sel_bias)
    sel_bias = jnp.concatenate([sel_bias, jnp.zeros((hd - n_blocks, seq), F32)], axis=0)
    qaug[:, 0:hd] = q_ref[...]
    for t in range(n_tiles):
        rows = slice(t * tq, (t + 1) * tq)
        qaug[rows, hd:2 * hd] = sel_bias[:, rows].T.astype(BF16)

    def key_rows(c, width):
        return pl.ds(pl.multiple_of(c * (width * tq), width * tq), width * tq)

    def scores(t, c, width):
        return _dot_nt(qaug[tile_rows(t), :], kaug[key_rows(c, width), :])

    def run_steps(first, advance, n_steps, step, per_trip, width):
        assert n_steps % per_trip == 0 and per_trip % 2 == 0
        cols = slice(0, width * tq)
        bufs = (s_a, s_b)
        s_a[:, cols] = scores(*first, width)

        def trip(j, tc):
            for i in range(per_trip):
                nxt = advance(*tc)
                bufs[(i + 1) % 2][:, cols] = scores(jnp.minimum(nxt[0], n_tiles - 1),
                                                    jnp.minimum(nxt[1], n_tiles // width - 1), width)
                step(bufs[i % 2].at[:, cols], *tc, width)
                tc = nxt
            return tc
        lax.fori_loop(0, n_steps // per_trip, trip, tuple(jnp.int32(v) for v in first))

    def diagonal_step(s_ref, t, c, width):
        r_i = lax.broadcasted_iota(jnp.int32, (tq, width * tq), 0)
        c_i = lax.broadcasted_iota(jnp.int32, (tq, width * tq), 1)
        s = jnp.where(c_i <= r_i + (width - 1) * tq, s_ref[...], NEG_INF)
        m = jnp.max(s, axis=-1, keepdims=True)
        p = jnp.exp2((s - m) * EXP2_SCALE)
        m_all[tile_rows(t), :] = m
        acc_all[tile_rows(t), :] = jnp.dot(p.astype(BF16), vaug[key_rows(c, width), :],
                                           preferred_element_type=F32)

    def past_step(s_ref, t, c, width):
        s = s_ref[...]
        m_old = m_all[tile_rows(t), :]
        m_new = jnp.maximum(m_old, jnp.max(s, axis=-1, keepdims=True))
        alpha = jnp.exp2((m_old - m_new) * EXP2_SCALE)
        p = jnp.exp2((s - m_new) * EXP2_SCALE)
        m_all[tile_rows(t), :] = m_new
        acc_all[tile_rows(t), :] = alpha * acc_all[tile_rows(t), :] + jnp.dot(
            p.astype(BF16), vaug[key_rows(c, width), :], preferred_element_type=F32)

    def next_past(t, c):
        wrap = c + 1 == lax.shift_right_logical(t, 1)
        return jnp.where(wrap, t + 1, t), jnp.where(wrap, 0, c + 1)

    first_steps = [(t, t, 1) if t % 2 == 0 else (t, t // 2, 2) for t in range(n_tiles)]
    bufs = (s_a, s_b)
    t0, c0, w0 = first_steps[0]
    s_a[:, 0:w0 * tq] = scores(t0, c0, w0)
    for i, (t, c, w) in enumerate(first_steps):
        if i + 1 < len(first_steps):
            tn, cn, wn = first_steps[i + 1]
            bufs[(i + 1) % 2][:, 0:wn * tq] = scores(tn, cn, wn)
        diagonal_step(bufs[i % 2].at[:, 0:w * tq], t, c, w)
    run_steps((2, 0), next_past, sum(t // 2 for t in range(n_tiles)), past_step, MOBA_PAST_STEPS_PER_TRIP, 2)

    def finish(t, carry):
        acc = acc_all[tile_rows(t), :]
        o_ref[tile_rows(t), :] = (acc[:, 0:hd] / acc[:, hd:2 * hd]).astype(o_ref.dtype)
        return carry
    lax.fori_loop(0, n_tiles, finish, 0)


def _moba(proj, batch, seq):
    tq = MOBA_TQ
    hd = HEAD_DIM
    full = lambda sl: pl.BlockSpec((None, seq, hd), lambda b, h: (b, 0, _bf16_slice(sl) * GROUP_HEADS + h))
    return pl.pallas_call(
        _moba_kernel,
        out_shape=jax.ShapeDtypeStruct((batch, seq, GROUP_WIDTH), BF16),
        grid=(batch, GROUP_HEADS),
        in_specs=[full(SL_C_Q), full(SL_C_K), full(SL_C_V)],
        out_specs=pl.BlockSpec((None, seq, hd), lambda b, h: (b, 0, h)),
        scratch_shapes=[
            pltpu.VMEM((seq, 2 * hd), BF16),
            pltpu.VMEM((seq, 2 * hd), BF16),
            pltpu.VMEM((seq // MOBA_BLOCK, hd), F32),
            pltpu.VMEM((seq, 2 * hd), BF16),
            pltpu.VMEM((seq, 1), F32),
            pltpu.VMEM((seq, 2 * hd), F32),
            pltpu.VMEM((tq, 2 * tq), F32),
            pltpu.VMEM((tq, 2 * tq), F32),
        ],
        compiler_params=_params("arbitrary", "arbitrary"),
        name="moba",
    )(proj, proj, proj)


def _memkv_kernel(mem_ref, g_ref, w_ref, o_ref):
    x = mem_ref[...]
    y = x * lax.rsqrt(jnp.mean(x * x, axis=-1, keepdims=True) + NORM_EPS) * g_ref[...]
    o_ref[...] = jnp.dot(y.astype(BF16), w_ref[...].astype(BF16),
                         preferred_element_type=F32).astype(o_ref.dtype)


def _memkv(mem, mem_norm_g, w_stack):
    batch, n_mem, d = mem.shape
    depth, _, n_cols = w_stack.shape
    return pl.pallas_call(
        _memkv_kernel,
        out_shape=jax.ShapeDtypeStruct((depth, batch, n_mem, n_cols), BF16),
        grid=(depth, batch),
        in_specs=[
            pl.BlockSpec((None, n_mem, d), lambda i, b: (b, 0, 0)),
            pl.BlockSpec((None, 1, d), lambda i, b: (i, 0, 0)),
            pl.BlockSpec((None, d, n_cols), lambda i, b: (i, 0, 0)),
        ],
        out_specs=pl.BlockSpec((None, None, n_mem, n_cols), lambda i, b: (i, b, 0, 0)),
        compiler_params=_params("parallel", "parallel", fuse_inputs=[False, True, False]),
        name="memkv",
    )(mem, mem_norm_g.reshape(depth, 1, d), w_stack)


LOCAL_TM = 512


def _local_kernel(bu_ref, bv_ref, db_ref, dc_ref, dh_ref, hc_ref, hh_ref, xq_ref,
                  ws_ref, bs_ref, lng_ref, lnb_ref, cw_ref, kv_ref, o_ref, zbuf, w_causal_scr):
    tm = LOCAL_TM
    gw = GROUP_WIDTH

    @pl.when(jnp.logical_and(pl.program_id(0) == 0, pl.program_id(1) == 0))
    def _():
        r = lax.broadcasted_iota(jnp.int32, (SGU_CHUNK, SGU_CHUNK), 0)
        c = lax.broadcasted_iota(jnp.int32, (SGU_CHUNK, SGU_CHUNK), 1)
        for hd in range(GROUP_HEADS):
            w_causal_scr[hd] = jnp.where(c <= r, ws_ref[hd], 0.0).astype(BF16)

    u = jax.nn.gelu(bu_ref[...].astype(F32))
    v = jax.nn.gelu(bv_ref[...].astype(F32))
    mu = jnp.mean(v, axis=-1, keepdims=True)
    vc = v - mu
    var = jnp.mean(vc * vc, axis=-1, keepdims=True)
    vnorm = (vc * lax.rsqrt(var + NORM_EPS) * lng_ref[...] + lnb_ref[...]).astype(BF16)
    for hd in range(GROUP_HEADS):
        cols = slice(hd * HEAD_DIM, (hd + 1) * HEAD_DIM)
        w_causal = w_causal_scr[hd]
        bias = bs_ref[:, hd:hd + 1]
        for ch in range(tm // SGU_CHUNK):
            rows = slice(ch * SGU_CHUNK, (ch + 1) * SGU_CHUNK)
            mixed = jnp.dot(w_causal, vnorm[rows, cols], preferred_element_type=F32) + bias
            o_ref[rows, cols] = (u[rows, cols] * mixed).astype(o_ref.dtype)

    first_tile = pl.program_id(1) == 0
    zbuf[0:CONV_HALO, :] = jnp.where(first_tile, 0.0, hc_ref[...].astype(F32) * hh_ref[...].astype(F32))
    zbuf[CONV_HALO:CONV_HALO + tm, :] = dc_ref[...].astype(F32) * dh_ref[...].astype(F32)
    y = cw_ref[CONV_WIDTH - 1:CONV_WIDTH, :] * zbuf[CONV_HALO:CONV_HALO + tm, :]
    for tap in range(CONV_WIDTH - 1):
        back = CONV_WIDTH - 1 - tap
        y = y + cw_ref[tap:tap + 1, :] * zbuf[CONV_HALO - back:CONV_HALO - back + tm, :]
    o_ref[:, gw:2 * gw] = (db_ref[...].astype(F32) * y).astype(o_ref.dtype)

    for hd in range(GROUP_HEADS):
        cols = slice(hd * HEAD_DIM, (hd + 1) * HEAD_DIM)
        kh = kv_ref[:, hd * HEAD_DIM:(hd + 1) * HEAD_DIM]
        vh = kv_ref[:, gw + hd * HEAD_DIM:gw + (hd + 1) * HEAD_DIM]
        s = _dot_nt(xq_ref[:, cols], kh) * ATTN_SCALE
        m = jnp.max(s, axis=-1, keepdims=True)
        p = jnp.exp(s - m)
        l = jnp.sum(p, axis=-1, keepdims=True)
        o = jnp.dot(p.astype(BF16), vh, preferred_element_type=F32) / l
        o_ref[:, 2 * gw + hd * HEAD_DIM:2 * gw + (hd + 1) * HEAD_DIM] = o.astype(o_ref.dtype)


def _local(proj, memkv, layer, sgu_w, sgu_b_t, ln_g, ln_b, conv_w, batch, seq):
    tm = LOCAL_TM
    gw = GROUP_WIDTH
    halo_per_tile = tm // CONV_HALO
    tile = lambda sl: pl.BlockSpec((None, tm, gw), lambda b, i: (b, i, _bf16_slice(sl)))
    halo = lambda sl: pl.BlockSpec(
        (None, CONV_HALO, gw), lambda b, i: (b, jnp.maximum(i * halo_per_tile - 1, 0), _bf16_slice(sl)))
    whole = lambda a: pl.BlockSpec(a.shape, lambda b, i: (0,) * a.ndim)
    n_mem = memkv.shape[2]
    return pl.pallas_call(
        _local_kernel,
        out_shape=jax.ShapeDtypeStruct((batch, seq, 3 * gw), BF16),
        grid=(batch, seq // tm),
        in_specs=[
            tile(SL_B_U), tile(SL_B_V), tile(SL_D_B), tile(SL_D_C), tile(SL_D_H),
            halo(SL_D_C), halo(SL_D_H), tile(SL_X_Q),
            whole(sgu_w), whole(sgu_b_t), whole(ln_g), whole(ln_b), whole(conv_w),
            pl.BlockSpec((None, None, n_mem, 2 * gw), lambda b, i: (layer, b, 0, 0)),
        ],
        out_specs=pl.BlockSpec((None, tm, 3 * gw), lambda b, i: (b, i, 0)),
        scratch_shapes=[pltpu.VMEM((CONV_HALO + tm, gw), F32),
                        pltpu.VMEM((GROUP_HEADS, SGU_CHUNK, SGU_CHUNK), BF16)],
        compiler_params=_params("arbitrary", "arbitrary", fuse_inputs=[False] * 8 + [True] * 5 + [False]),
        name="local",
    )(proj, proj, proj, proj, proj, proj, proj, proj,
      sgu_w, sgu_b_t, ln_g, ln_b, conv_w, memkv)


OUTPROJ_TM = 256


def _outproj_kernel(ya_ref, yl_ref, yc_ref, ga_ref, gb_ref, gc_ref, gd_ref, gx_ref,
                    ng_ref, w_ref, x_ref, pg_ref, *outs, final):
    gw = GROUP_WIDTH
    branches = (
        (lambda: ya_ref[...], ga_ref),
        (lambda: yl_ref[:, 0:gw], gb_ref),
        (lambda: yc_ref[...], gc_ref),
        (lambda: yl_ref[:, gw:2 * gw], gd_ref),
        (lambda: yl_ref[:, 2 * gw:3 * gw], gx_ref),
    )
    acc = x_ref[...]
    for grp, (o, gate_ref) in enumerate(branches):
        rows = slice(grp * gw, (grp + 1) * gw)
        y = o().astype(F32) * gate_ref[...].astype(F32)
        y = y * lax.rsqrt(jnp.mean(y * y, axis=-1, keepdims=True) + NORM_EPS)
        y = (y * ng_ref[:, rows]).astype(BF16)
        acc = acc + jnp.dot(y, w_ref[rows, :].astype(BF16), preferred_element_type=F32)
    normed = acc * lax.rsqrt(jnp.mean(acc * acc, axis=-1, keepdims=True) + NORM_EPS) * pg_ref[...]
    if final:
        (o_ref,) = outs
        o_ref[...] = normed
    else:
        o_ref, h_ref = outs
        o_ref[...] = acc
        h_ref[...] = normed.astype(h_ref.dtype)


def _outproj(ya, yl, yc, proj_b, out_norm_g, w_stack, layer, x2d, post_g, final):
    m, d = x2d.shape
    tm = OUTPROJ_TM
    gw = GROUP_WIDTH
    gate = lambda sl: pl.BlockSpec((tm, gw), lambda i: (i, _bf16_slice(sl)))
    row_tile = pl.BlockSpec((tm, d), lambda i: (i, 0))
    x_out = jax.ShapeDtypeStruct((m, d), F32)
    return pl.pallas_call(
        functools.partial(_outproj_kernel, final=final),
        out_shape=x_out if final else (x_out, jax.ShapeDtypeStruct((m, d), BF16)),
        grid=(m // tm,),
        in_specs=[
            pl.BlockSpec((tm, gw), lambda i: (i, 0)),
            pl.BlockSpec((tm, 3 * gw), lambda i: (i, 0)),
            pl.BlockSpec((tm, gw), lambda i: (i, 0)),
            gate(SL_A_G), gate(SL_B_G), gate(SL_C_G), gate(SL_D_G), gate(SL_X_G),
            pl.BlockSpec((1, MIX_WIDTH), lambda i: (0, 0)),
            pl.BlockSpec((None, MIX_WIDTH, d), lambda i: (layer, 0, 0), pipeline_mode=pl.Buffered(1)),
            row_tile,
            pl.BlockSpec((1, d), lambda i: (0, 0)),
        ],
        out_specs=row_tile if final else (row_tile, row_tile),
        compiler_params=_params("parallel", fuse_inputs=[False] * 8 + [True, False, False, True]),
        name="outproj_final" if final else "outproj",
    )(ya, yl, yc, proj_b, proj_b, proj_b, proj_b, proj_b,
      out_norm_g.reshape(1, MIX_WIDTH), w_stack, x2d, post_g.reshape(1, d))


def _rotary_tables(seq):
    half = HEAD_DIM // 2
    inv_freq = ROPE_THETA ** (-np.arange(half, dtype=np.float64) / half)
    ang = np.arange(seq, dtype=np.float64)[:, None] * inv_freq[None, :]
    cos, sin = np.cos(ang), np.sin(ang)
    return (jnp.asarray(np.concatenate([cos, cos], axis=-1), F32),
            jnp.asarray(np.concatenate([-sin, sin], axis=-1), F32))


def kernel(x, mem, norm_g, w_in, sgu_w, sgu_b, sgu_ln_g, sgu_ln_b, conv_w, mem_norm_g, w_mem_kv,
           out_norm_g, w_out, final_norm_g):
    batch, seq, d = x.shape
    depth = w_in.shape[0]
    cos_t, sin_t = _rotary_tables(seq)
    x2d = x.reshape(batch * seq, d)
    h2d = _rmsnorm(x2d, norm_g[0])
    memkv = _memkv(mem, mem_norm_g, w_mem_kv)
    flat = lambda t: t.reshape(batch * seq, t.shape[-1])
    for i in range(depth):
        final = i == depth - 1
        proj_f, proj_b = _inproj(h2d, w_in, i, cos_t, sin_t, seq)
        proj_a = proj_f.reshape(batch, seq, proj_f.shape[-1])
        proj_r = proj_b.reshape(batch, seq, proj_b.shape[-1])
        ya = _dilated(proj_a, batch, seq)
        yc = _moba(proj_r, batch, seq)
        yl = _local(proj_r, memkv, i, sgu_w[i], sgu_b[i].T, sgu_ln_g[i].reshape(1, GROUP_WIDTH),
                    sgu_ln_b[i].reshape(1, GROUP_WIDTH), conv_w[i], batch, seq)
        out = _outproj(flat(ya), flat(yl), flat(yc), proj_b, out_norm_g[i], w_out, i, x2d,
                       final_norm_g if final else norm_g[i + 1], final)
        if final:
            x2d = out
        else:
            x2d, h2d = out
    return x2d.reshape(batch, seq, d)
```
